```python
import jax, jax.numpy as jnp
from jax import lax
import numpy as np

D_MODEL = 1024
BATCH = 8
SEQ = 8192
DEPTH = 4
DEC_BATCH = 8
DEC_SEQ = 32
PAST_LEN = 1024

CHUNK = 64
D_RWKV = D_MODEL
RWKV_HEAD_DIM = 64
N_RWKV_HEADS = D_RWKV // RWKV_HEAD_DIM
DECAY_LORA = 64
ICLR_LORA = 64
GATE_LORA = 128
RWKV_COLS = 3 * D_RWKV + DECAY_LORA + ICLR_LORA + GATE_LORA
RWKV_SPLITS = (D_RWKV, 2 * D_RWKV, 3 * D_RWKV, 3 * D_RWKV + DECAY_LORA, 3 * D_RWKV + DECAY_LORA + ICLR_LORA)
GMLP_WIDTH = D_MODEL
GMLP_CHUNK = 128
GMLP_GROUP_CH = 128
GMLP_GROUPS = GMLP_WIDTH // GMLP_GROUP_CH
N_MEM = 256
X_HEADS = 4
X_HEAD_DIM = D_MODEL // X_HEADS
D_X = X_HEADS * X_HEAD_DIM
N_BRANCH = 3
BRANCH_WIDTH = D_MODEL
COL_GMLP = RWKV_COLS
COL_Q = COL_GMLP + 2 * GMLP_WIDTH
COL_GATE = COL_Q + D_X
IN_COLS = COL_GATE + N_BRANCH * D_MODEL
D_FF = 4 * D_MODEL
RMS_EPS = 1e-6
LN_EPS = 1e-5
GN_EPS = 64e-5

kernel_name = 'hybrid_rwkv7_sgu_memxattn_stream_step'


def _rmsnorm(x, g):
    xf = x.astype(jnp.float32)
    y = xf * lax.rsqrt(jnp.mean(xf * xf, axis=-1, keepdims=True) + RMS_EPS)
    return (y * g.astype(jnp.float32)).astype(x.dtype)


def _layernorm(x, g, b, eps):
    xf = x.astype(jnp.float32)
    xc = xf - jnp.mean(xf, axis=-1, keepdims=True)
    var = jnp.mean(xc * xc, axis=-1, keepdims=True)
    y = xc * lax.rsqrt(var + eps) * g.astype(jnp.float32) + b.astype(jnp.float32)
    return y.astype(x.dtype)


def _rwkv7_recurrence(r, decay, k, v, a_vec, b_vec, s0):
    def step(s, inp):
        r_t, w_t, k_t, v_t, a_t, b_t = inp
        sa = jnp.einsum('bhvk,bhk->bhv', s, a_t)
        s = s * w_t[:, :, None, :] + sa[..., None] * b_t[:, :, None, :] + v_t[..., None] * k_t[:, :, None, :]
        y = jnp.einsum('bhvk,bhk->bhv', s, r_t)
        return s, y
    xs = tuple(jnp.swapaxes(t, 0, 1) for t in (r, decay, k, v, a_vec, b_vec))
    s_final, ys = lax.scan(step, s0, xs)
    return jnp.swapaxes(ys, 0, 1), s_final


def _rwkv7_mixer(cols, prev_row, wkv0, mu, w0, w2, a0, a2, g2, k_k, k_a, r_k, lnx_g, lnx_b):
    B, T, _ = cols.shape
    dt = cols.dtype
    f32 = jnp.float32
    H, N = N_RWKV_HEADS, RWKV_HEAD_DIM
    shifted = jnp.concatenate([prev_row.astype(dt), cols[:, :-1]], axis=1)
    mixed = cols + (shifted - cols) * mu.astype(dt)
    r, k, v, wl, al, gl = jnp.split(mixed, RWKV_SPLITS, axis=-1)
    z = (w0 + jnp.tanh(wl) @ w2).astype(f32)
    decay = jnp.exp(-jnp.exp(-jax.nn.softplus(-z) - 0.5))
    a = jax.nn.sigmoid((a0 + al @ a2).astype(f32))
    g = jax.nn.sigmoid(gl) @ g2
    heads = lambda t: t.astype(f32).reshape(B, T, H, N)
    r_h, k_h, v_h, a_h, decay_h = heads(r), heads(k), heads(v), heads(a), heads(decay)
    kk = k_h * k_k.astype(f32).reshape(H, N)
    kk = kk * lax.rsqrt(jnp.maximum(jnp.sum(kk * kk, axis=-1, keepdims=True), 1e-24))
    k_h = k_h * (1.0 + (a_h - 1.0) * k_a.astype(f32).reshape(H, N))
    y, wkv = _rwkv7_recurrence(r_h, decay_h, k_h, v_h, -kk, kk * a_h, wkv0.astype(f32))
    y = _layernorm(y, lnx_g.reshape(H, N), lnx_b.reshape(H, N), GN_EPS)
    y = y + jnp.sum(r_h * k_h * r_k.astype(f32), axis=-1, keepdims=True) * v_h
    y = y.reshape(B, T, D_RWKV).astype(dt) * g
    return y, cols[:, -1:], wkv


def _sgu_mixer(cols, ln_g, ln_b, w_s, b_s):
    B, T, _ = cols.shape
    z = jax.nn.gelu(cols)
    u, v = jnp.split(z, 2, axis=-1)
    v = _layernorm(v, ln_g, ln_b, LN_EPS)
    pad = (-T) % GMLP_CHUNK
    n_chunks = (T + pad) // GMLP_CHUNK
    vc = jnp.pad(v, ((0, 0), (0, pad), (0, 0))).reshape(B, n_chunks, GMLP_CHUNK, GMLP_GROUPS, GMLP_GROUP_CH)
    mask = jnp.tril(jnp.ones((GMLP_CHUNK, GMLP_CHUNK), dtype=bool))
    ws = jnp.where(mask, w_s, jnp.zeros((), w_s.dtype)).astype(v.dtype)
    sv = jnp.einsum('gij,bnjgc->bnigc', ws, vc) + b_s.T.astype(v.dtype)[None, None, :, :, None]
    sv = sv.reshape(B, n_chunks * GMLP_CHUNK, GMLP_WIDTH)[:, :T]
    return u * sv, v


def _memory_kv(mem, g, w_kv):
    B = mem.shape[0]
    kv = _rmsnorm(mem, g) @ w_kv
    k, v = jnp.split(kv, 2, axis=-1)
    return k.reshape(B, N_MEM, X_HEADS, X_HEAD_DIM), v.reshape(B, N_MEM, X_HEADS, X_HEAD_DIM)


def _memory_attention(q_cols, mem_k, mem_v):
    B, T, _ = q_cols.shape
    q = q_cols.reshape(B, T, X_HEADS, X_HEAD_DIM)
    s = jnp.einsum('bthd,bmhd->bhtm', q, mem_k.astype(q.dtype)).astype(jnp.float32) * (X_HEAD_DIM ** -0.5)
    p = jax.nn.softmax(s, axis=-1).astype(q.dtype)
    o = jnp.einsum('bhtm,bmhd->bthd', p, mem_v.astype(q.dtype))
    return o.reshape(B, T, D_X)


def _layer(x, mem_k, mem_v, prev_row, wkv0, lp):
    h = _rmsnorm(x, lp['norm_mix_g'])
    w_in = lp['w_in']
    y_a, new_row, wkv = _rwkv7_mixer(h @ w_in[:, :COL_GMLP], prev_row, wkv0, lp['rwkv_mu'], lp['rwkv_w0'],
                                     lp['rwkv_w2'], lp['rwkv_a0'], lp['rwkv_a2'], lp['rwkv_g2'], lp['rwkv_k_k'],
                                     lp['rwkv_k_a'], lp['rwkv_r_k'], lp['rwkv_lnx_g'], lp['rwkv_lnx_b'])
    y_b, v_rows = _sgu_mixer(h @ w_in[:, COL_GMLP:COL_Q], lp['sgu_ln_g'], lp['sgu_ln_b'], lp['sgu_w_s'], lp['sgu_b_s'])
    y_c = _memory_attention(h @ w_in[:, COL_Q:COL_GATE], mem_k, mem_v)
    gate = jax.nn.sigmoid((h @ w_in[:, COL_GATE:]).astype(jnp.float32)).astype(x.dtype)
    w_b = lp['w_branch']
    merged = (gate[..., :D_MODEL] * (y_a @ w_b[0])
              + gate[..., D_MODEL:2 * D_MODEL] * (y_b @ w_b[1])
              + gate[..., 2 * D_MODEL:] * (y_c @ w_b[2]))
    x = x + merged @ lp['w_out']
    up = _rmsnorm(x, lp['norm_ffn_g']) @ lp['w_ffn_up']
    x = x + jnp.square(jax.nn.relu(up)) @ lp['w_ffn_down']
    return x, new_row, wkv, v_rows


def setup_inputs(seed: int = 0) -> dict:
    key = jax.random.key(seed)
    ks = iter(jax.random.split(key, 40))
    f32 = jnp.float32
    nrm = lambda shape, scale: scale * jax.random.normal(next(ks), shape, f32)
    uni = lambda shape: jax.random.uniform(next(ks), shape, f32)
    H, N = N_RWKV_HEADS, RWKV_HEAD_DIM
    return {
        'x_prompt': nrm((BATCH, SEQ, D_MODEL), 1.0),
        'x_sample': nrm((DEC_BATCH, DEC_SEQ, D_MODEL), 1.0),
        'cache_mem_k': nrm((DEPTH, DEC_BATCH, N_MEM, X_HEADS, X_HEAD_DIM), 1.0),
        'cache_mem_v': nrm((DEPTH, DEC_BATCH, N_MEM, X_HEADS, X_HEAD_DIM), 1.0),
        'state_wkv': nrm((DEPTH, DEC_BATCH, H, N, N), 0.3),
        'state_shift': nrm((DEPTH, DEC_BATCH, 1, RWKV_COLS), 1.0),
        'mem_prompt': nrm((BATCH, N_MEM, D_MODEL), 1.0),
        'norm_mix_g': 1.0 + nrm((DEPTH, D_MODEL), 0.01),
        'norm_mem_g': 1.0 + nrm((DEPTH, D_MODEL), 0.01),
        'norm_ffn_g': 1.0 + nrm((DEPTH, D_MODEL), 0.01),
        'norm_final_g': 1.0 + nrm((D_MODEL,), 0.01),
        'w_in': nrm((DEPTH, D_MODEL, IN_COLS), D_MODEL ** -0.5),
        'w_mem_kv': nrm((DEPTH, D_MODEL, 2 * D_X), D_MODEL ** -0.5),
        'rwkv_mu': uni((DEPTH, RWKV_COLS)),
        'rwkv_w0': -7.0 + 5.0 * uni((DEPTH, D_RWKV)),
        'rwkv_w2': nrm((DEPTH, DECAY_LORA, D_RWKV), 0.1),
        'rwkv_a0': nrm((DEPTH, D_RWKV), 0.1),
        'rwkv_a2': nrm((DEPTH, ICLR_LORA, D_RWKV), 0.1),
        'rwkv_g2': nrm((DEPTH, GATE_LORA, D_RWKV), GATE_LORA ** -0.5),
        'rwkv_k_k': 0.85 + nrm((DEPTH, D_RWKV), 0.05),
        'rwkv_k_a': 1.0 + nrm((DEPTH, D_RWKV), 0.05),
        'rwkv_r_k': nrm((DEPTH, H, N), 0.1),
        'rwkv_lnx_g': 1.0 + nrm((DEPTH, D_RWKV), 0.01),
        'rwkv_lnx_b': nrm((DEPTH, D_RWKV), 0.01),
        'sgu_ln_g': 1.0 + nrm((DEPTH, GMLP_WIDTH), 0.01),
        'sgu_ln_b': nrm((DEPTH, GMLP_WIDTH), 0.01),
        'sgu_w_s': nrm((DEPTH, GMLP_GROUPS, GMLP_CHUNK, GMLP_CHUNK), GMLP_CHUNK ** -0.5),
        'sgu_b_s': 1.0 + nrm((DEPTH, GMLP_GROUPS, GMLP_CHUNK), 0.01),
        'w_branch': nrm((DEPTH, N_BRANCH, BRANCH_WIDTH, D_MODEL), BRANCH_WIDTH ** -0.5),
        'w_out': nrm((DEPTH, D_MODEL, D_MODEL), D_MODEL ** -0.5),
        'w_ffn_up': nrm((DEPTH, D_MODEL, D_FF), D_MODEL ** -0.5),
        'w_ffn_down': nrm((DEPTH, D_FF, D_MODEL), D_FF ** -0.5),
    }


def reference(x_prompt, x_sample, cache_mem_k, cache_mem_v, state_wkv, state_shift, mem_prompt,
              norm_mix_g, norm_mem_g, norm_ffn_g, norm_final_g, w_in, w_mem_kv,
              rwkv_mu, rwkv_w0, rwkv_w2, rwkv_a0, rwkv_a2, rwkv_g2, rwkv_k_k, rwkv_k_a, rwkv_r_k,
              rwkv_lnx_g, rwkv_lnx_b, sgu_ln_g, sgu_ln_b, sgu_w_s, sgu_b_s,
              w_branch, w_out, w_ffn_up, w_ffn_down):
    b_p = x_prompt.shape[0]
    H, N = N_RWKV_HEADS, RWKV_HEAD_DIM
    prompt_row0 = jnp.zeros((b_p, 1, RWKV_COLS), x_prompt.dtype)
    prompt_wkv0 = jnp.zeros((b_p, H, N, N), jnp.float32)
    xp, xs = x_prompt, x_sample
    mk_p, mv_p, wkv_p, row_p, wkv_s, row_s, v_s = [], [], [], [], [], [], []
    for l in range(DEPTH):
        lp = {
            'norm_mix_g': norm_mix_g[l], 'w_in': w_in[l],
            'rwkv_mu': rwkv_mu[l], 'rwkv_w0': rwkv_w0[l], 'rwkv_w2': rwkv_w2[l],
            'rwkv_a0': rwkv_a0[l], 'rwkv_a2': rwkv_a2[l], 'rwkv_g2': rwkv_g2[l],
            'rwkv_k_k': rwkv_k_k[l], 'rwkv_k_a': rwkv_k_a[l], 'rwkv_r_k': rwkv_r_k[l],
            'rwkv_lnx_g': rwkv_lnx_g[l], 'rwkv_lnx_b': rwkv_lnx_b[l],
            'sgu_ln_g': sgu_ln_g[l], 'sgu_ln_b': sgu_ln_b[l], 'sgu_w_s': sgu_w_s[l], 'sgu_b_s': sgu_b_s[l],
            'w_branch': w_branch[l], 'w_out': w_out[l],
            'norm_ffn_g': norm_ffn_g[l], 'w_ffn_up': w_ffn_up[l], 'w_ffn_down': w_ffn_down[l],
        }
        mem_k, mem_v = _memory_kv(mem_prompt, norm_mem_g[l], w_mem_kv[l])
        xp, r_p, s_p, _ = _layer(xp, mem_k, mem_v, prompt_row0, prompt_wkv0, lp)
        mk_p.append(mem_k)
        mv_p.append(mem_v)
        wkv_p.append(s_p.astype(state_wkv.dtype))
        row_p.append(r_p)
        xs, r_s, s_s, vr = _layer(xs, cache_mem_k[l], cache_mem_v[l], state_shift[l], state_wkv[l], lp)
        wkv_s.append(s_s.astype(state_wkv.dtype))
        row_s.append(r_s)
        v_s.append(vr)
    y_prompt = _rmsnorm(xp, norm_final_g)
    y_sample = _rmsnorm(xs, norm_final_g)
    new_cache_mem_k_prompt = jnp.stack(mk_p)
    new_cache_mem_v_prompt = jnp.stack(mv_p)
    new_state_wkv_prompt = jnp.stack(wkv_p)
    new_state_shift_prompt = jnp.stack(row_p)
    new_state_wkv_sample = jnp.stack(wkv_s)
    new_state_shift_sample = jnp.stack(row_s)
    new_sgu_v_sample = jnp.stack(v_s)
    return (y_prompt, y_sample, new_cache_mem_k_prompt, new_cache_mem_v_prompt, new_state_wkv_prompt,
            new_state_shift_prompt, new_state_wkv_sample, new_state_shift_sample, new_sgu_v_sample)
```

```python
import functools
import math

import jax
import jax.numpy as jnp
from jax import lax
from jax.experimental import pallas as pl
from jax.experimental.pallas import tpu as pltpu

F32 = jnp.float32
BF16 = jnp.bfloat16

D_MODEL = 1024
HEAD_DIM = 64
N_HEADS = D_MODEL // HEAD_DIM
PAIR = 2 * HEAD_DIM
N_PAIRS = D_MODEL // PAIR
LORA_COLS = 256
RWKV_COLS = 3 * D_MODEL + LORA_COLS
GMLP_CHUNK = 128
GMLP_GROUPS = 8
N_MEM = 256
X_HEADS = 4
X_HEAD_DIM = D_MODEL // X_HEADS
D_FF = 4 * D_MODEL
IN_COLS = 9 * D_MODEL + LORA_COLS
RMS_EPS = 1e-6
LN_EPS = 1e-5
GN_EPS = 64e-5
CHUNK = 64
DECAY_SCALE = math.exp(-0.5)
VMEM_LIMIT = 56 * 1024 * 1024

COL_R, COL_K, COL_V, COL_U, COL_GV, COL_Q = 0, 1, 2, 3, 4, 5
COL_GATE_BLOCK3 = 2
COL_LORA_BLOCK = 9 * D_MODEL // LORA_COLS


def _cparams(sem):
    return pltpu.CompilerParams(dimension_semantics=sem, vmem_limit_bytes=VMEM_LIMIT)


def _dot(a, b):
    return jnp.dot(a, b, preferred_element_type=F32)


def _dot_nt(a, b):
    return lax.dot_general(a, b, (((1,), (1,)), ((), ())), preferred_element_type=F32)


def _dot_tn(a, b):
    return lax.dot_general(a, b, (((0,), (0,)), ((), ())), preferred_element_type=F32)


def _split(x):
    hi = x.astype(BF16)
    lo = (x - hi.astype(F32)).astype(BF16)
    return hi, lo


def _rmsnorm(x, g):
    return x * lax.rsqrt(jnp.mean(x * x, axis=-1, keepdims=True) + RMS_EPS) * g


def _sigmoid(x):
    return 1.0 / (1.0 + jnp.exp(-x))


def _proj_kernel(x_ref, g_ref, w_ref, o_ref):
    h = _rmsnorm(x_ref[...], g_ref[...]).astype(BF16)
    o_ref[...] = _dot(h, w_ref[...]).astype(o_ref.dtype)


def _proj(x, g, w, out_dtype, tm, tn):
    t, d = x.shape
    nc = w.shape[1]
    return pl.pallas_call(
        _proj_kernel,
        grid=(nc // tn, t // tm),
        in_specs=[pl.BlockSpec((tm, d), lambda j, i: (i, 0)),
                  pl.BlockSpec((1, d), lambda j, i: (0, 0)),
                  pl.BlockSpec((d, tn), lambda j, i: (0, j))],
        out_specs=pl.BlockSpec((tm, tn), lambda j, i: (i, j)),
        out_shape=jax.ShapeDtypeStruct((t, nc), out_dtype),
        compiler_params=_cparams(("arbitrary", "arbitrary")),
        name="proj",
    )(x, g, w)


def _head_sum(x, e_ref):
    c = x.shape[0]
    q = 4 * HEAD_DIM
    xs = jnp.concatenate([x[:, i * q:(i + 1) * q] for i in range(D_MODEL // q)], axis=0)
    r = _dot(xs.astype(BF16), e_ref[...])
    return jnp.concatenate([r[i * c:(i + 1) * c] for i in range(D_MODEL // q)], axis=1)


def _rwkv_kernel(r_ref, k_ref, v_ref, lo_ref, pr_ref, pk_ref, pv_ref, plo_ref, s0_ref,
                 vec_ref, mulo_ref, lw_ref, e_ref,
                 y_ref, sout_ref,
                 s_scr, prr, prk, prv, prlo, *, t_valid):
    c_idx = pl.program_id(1)
    C = CHUNK

    @pl.when(c_idx == 0)
    def _init():
        s_scr[...] = s0_ref[...]
        prr[...] = pr_ref[...]
        prk[...] = pk_ref[...]
        prv[...] = pv_ref[...]
        prlo[...] = plo_ref[...]

    def vec(i):
        return vec_ref[i:i + 1, :]

    def mixed(ref, prev, mu):
        x = ref[...].astype(F32)
        rolled = pltpu.roll(x, 1, 0)
        row = lax.broadcasted_iota(jnp.int32, x.shape, 0)
        shifted = jnp.where(row == 0, prev[...], rolled)
        prev[...] = x[C - 1:C, :]
        return x + (shifted - x) * mu

    r = mixed(r_ref, prr, vec(0))
    k = mixed(k_ref, prk, vec(1))
    v = mixed(v_ref, prv, vec(2))
    lo = mixed(lo_ref, prlo, mulo_ref[...])

    lane_lo = lax.broadcasted_iota(jnp.int32, lo.shape, 1)
    act = jnp.where(lane_lo < 64, jnp.tanh(lo), jnp.where(lane_lo < 128, lo, _sigmoid(lo))).astype(BF16)
    z = vec(3) + _dot(act, lw_ref[0])
    a = _sigmoid(vec(4) + _dot(act, lw_ref[1]))
    g = _dot(act, lw_ref[2])
    logw = -DECAY_SCALE * _sigmoid(z)

    kk = k * vec(5)
    kk = kk * lax.rsqrt(jnp.maximum(_head_sum(kk * kk, e_ref), 1e-24))
    k = k * (1.0 + (a - 1.0) * vec(6))
    bonus = _head_sum(r * k * vec(7), e_ref)

    if t_valid < C:
        valid = lax.broadcasted_iota(jnp.int32, (C, 1), 0) < t_valid
        logw = jnp.where(valid, logw, 0.0)
        kk = jnp.where(valid, kk, 0.0)
        k = jnp.where(valid, k, 0.0)
        v = jnp.where(valid, v, 0.0)

    ti = lax.broadcasted_iota(jnp.int32, (C, C), 0)
    tj = lax.broadcasted_iota(jnp.int32, (C, C), 1)
    tril = jnp.where(tj <= ti, 1.0, 0.0).astype(BF16)
    lw_hi, lw_lo = _split(logw)
    cum = _dot(tril, lw_hi) + _dot(tril, lw_lo)
    e_inc = jnp.exp(cum)
    e_dec = jnp.exp(-cum)
    a_t = -kk * jnp.exp(cum - logw)
    r_t = r * e_inc
    b_t = kk * a * e_dec
    k_t = k * e_dec
    w_end = e_inc[C - 1:C, :]
    b_h = b_t * w_end
    k_h = k_t * w_end

    lane = lax.broadcasted_iota(jnp.int32, (C, PAIR), 1)
    first = lane < HEAD_DIM
    ri = lax.broadcasted_iota(jnp.int32, (2 * C, 2 * C), 0)
    ci = lax.broadcasted_iota(jnp.int32, (2 * C, 2 * C), 1)
    strict = ci < ri
    incl = ci <= ri
    eye = jnp.where(ci == ri, 1.0, 0.0)

    def stack2(x):
        return jnp.concatenate([jnp.where(first, x, 0.0), jnp.where(first, 0.0, x)], axis=0)

    def mm3(x, y):
        xh, xl = _split(x)
        yh, yl = _split(y)
        return _dot(xh, yh) + _dot(xh, yl) + _dot(xl, yh)

    y_pairs = []
    for p in range(N_PAIRS):
        sl = slice(p * PAIR, (p + 1) * PAIR)
        a2, r2, b2, k2, v2 = stack2(a_t[:, sl]), stack2(r_t[:, sl]), stack2(b_t[:, sl]), stack2(k_t[:, sl]), stack2(v[:, sl])
        bh2, kh2 = stack2(b_h[:, sl]), stack2(k_h[:, sl])
        lhs = jnp.concatenate([a2, r2], axis=0).astype(BF16)
        rhs = jnp.concatenate([b2, k2], axis=0).astype(BF16)
        pm = _dot_nt(lhs, rhs)
        ab = jnp.where(strict, pm[0:2 * C, 0:2 * C], 0.0)
        ak = jnp.where(strict, pm[0:2 * C, 2 * C:4 * C], 0.0)
        rb = jnp.where(incl, pm[2 * C:4 * C, 0:2 * C], 0.0)
        rk = jnp.where(incl, pm[2 * C:4 * C, 2 * C:4 * C], 0.0)
        m = ab
        tinv = eye + m
        for _ in range(int(math.log2(C)) - 1):
            m = mm3(m, m)
            tinv = tinv + mm3(tinv, m)
        s2 = s_scr[p]
        s_hi, s_lo = _split(s2)
        as_rs = _dot_nt(lhs, s_hi) + _dot_nt(lhs, s_lo)
        v2b = v2.astype(BF16)
        pv = _dot(jnp.concatenate([ak, rk], axis=0).astype(BF16), v2b)
        sa = mm3(tinv, as_rs[0:2 * C] + pv[0:2 * C])
        y2 = as_rs[2 * C:] + pv[2 * C:] + _dot(rb.astype(BF16), sa.astype(BF16))
        y_pairs.append(y2[0:C] + y2[C:2 * C])
        upd = _dot_tn(jnp.concatenate([sa, v2], axis=0).astype(BF16),
                      jnp.concatenate([bh2, kh2], axis=0).astype(BF16))
        s_scr[p] = s2 * w_end[:, sl] + upd

    y = jnp.concatenate(y_pairs, axis=1)
    mean = _head_sum(y, e_ref) * (1.0 / HEAD_DIM)
    yc = y - mean
    var = _head_sum(yc * yc, e_ref) * (1.0 / HEAD_DIM)
    yn = yc * lax.rsqrt(var + GN_EPS) * vec(8) + vec(9)
    y_ref[...] = ((yn + bonus * v) * g).astype(y_ref.dtype)

    @pl.when(c_idx == pl.num_programs(1) - 1)
    def _fin():
        sout_ref[...] = s_scr[...]


def _rwkv(cols3, prev, s0, vec, mulo, lw, e256, t_valid):
    b, tp, _ = cols3.shape
    C = CHUNK
    col = lambda j: pl.BlockSpec((None, C, D_MODEL), lambda bi, ci: (bi, ci, j))
    rowspec = lambda w: pl.BlockSpec((None, 1, w), lambda bi, ci: (bi, 0, 0))
    full = lambda a: pl.BlockSpec(a.shape, lambda bi, ci: (0,) * a.ndim)
    sspec = pl.BlockSpec((None, N_PAIRS, PAIR, PAIR), lambda bi, ci: (bi, 0, 0, 0))
    return pl.pallas_call(
        functools.partial(_rwkv_kernel, t_valid=min(t_valid, C)),
        grid=(b, tp // C),
        in_specs=[col(COL_R), col(COL_K), col(COL_V),
                  pl.BlockSpec((None, C, LORA_COLS), lambda bi, ci: (bi, ci, COL_LORA_BLOCK)),
                  rowspec(D_MODEL), rowspec(D_MODEL), rowspec(D_MODEL), rowspec(LORA_COLS),
                  sspec, full(vec), full(mulo), full(lw), full(e256)],
        out_specs=[pl.BlockSpec((None, C, D_MODEL), lambda bi, ci: (bi, ci, 0)), sspec],
        out_shape=[jax.ShapeDtypeStruct((b, tp, D_MODEL), BF16),
                   jax.ShapeDtypeStruct(s0.shape, F32)],
        scratch_shapes=[pltpu.VMEM((N_PAIRS, PAIR, PAIR), F32),
                        pltpu.VMEM((1, D_MODEL), F32), pltpu.VMEM((1, D_MODEL), F32),
                        pltpu.VMEM((1, D_MODEL), F32), pltpu.VMEM((1, LORA_COLS), F32)],
        compiler_params=_cparams(("arbitrary", "arbitrary")),
        name="rwkv7",
    )(cols3, cols3, cols3, cols3, *prev, s0, vec, mulo, lw, e256)


def _gelu(x):
    return 0.5 * x * (1.0 + jnp.tanh(math.sqrt(2.0 / math.pi) * (x + 0.044715 * (x * x * x))))


def _sgu_kernel(u_ref, v_ref, ln_ref, ws_ref, bs_ref, y_ref, *maybe_v_out, lc):
    u = _gelu(u_ref[...].astype(F32))
    v = _gelu(v_ref[...].astype(F32))
    mu = jnp.mean(v, axis=-1, keepdims=True)
    vc = v - mu
    var = jnp.mean(vc * vc, axis=-1, keepdims=True)
    v = vc * lax.rsqrt(var + LN_EPS) * ln_ref[0:1, :] + ln_ref[1:2, :]
    if maybe_v_out:
        maybe_v_out[0][...] = v
    vb = v.astype(BF16)
    n_chunks = v.shape[0] // lc
    ri = lax.broadcasted_iota(jnp.int32, (lc, lc), 0)
    ci = lax.broadcasted_iota(jnp.int32, (lc, lc), 1)
    bias = bs_ref[...]
    outs = []
    for gi in range(GMLP_GROUPS):
        gs = slice(gi * GMLP_CHUNK, (gi + 1) * GMLP_CHUNK)
        w = jnp.where(ci <= ri, ws_ref[gi], 0.0).astype(BF16)
        rhs = jnp.concatenate([vb[n * lc:(n + 1) * lc, gs] for n in range(n_chunks)], axis=1)
        sv = _dot(w, rhs)
        sv = jnp.concatenate([sv[:, n * GMLP_CHUNK:(n + 1) * GMLP_CHUNK] for n in range(n_chunks)], axis=0)
        outs.append(sv + bias[:, gs])
    sv = jnp.concatenate(outs, axis=1)
    y_ref[...] = (u * sv).astype(y_ref.dtype)


def _sgu(cols3, ln, ws, bs_full, lc, n_chunks, want_v):
    b, t, _ = cols3.shape
    tt = lc * n_chunks
    col = lambda j: pl.BlockSpec((None, tt, D_MODEL), lambda bi, ti: (bi, ti, j))
    full = lambda a: pl.BlockSpec(a.shape, lambda bi, ti: (0,) * a.ndim)
    ospec = pl.BlockSpec((None, tt, D_MODEL), lambda bi, ti: (bi, ti, 0))
    out_specs = [ospec]
    out_shape = [jax.ShapeDtypeStruct((b, t, D_MODEL), BF16)]
    if want_v:
        out_specs.append(ospec)
        out_shape.append(jax.ShapeDtypeStruct((b, t, D_MODEL), F32))
    return pl.pallas_call(
        functools.partial(_sgu_kernel, lc=lc),
        grid=(b, t // tt),
        in_specs=[col(COL_U), col(COL_GV), full(ln), full(ws), full(bs_full)],
        out_specs=out_specs,
        out_shape=out_shape,
        compiler_params=_cparams(("arbitrary", "arbitrary")),
        name="sgu",
    )(cols3, cols3, ln, ws, bs_full)


def _xattn_kernel(q_ref, k_ref, v_ref, o_ref):
    outs = []
    for h in range(X_HEADS):
        hs = slice(h * X_HEAD_DIM, (h + 1) * X_HEAD_DIM)
        s = _dot_nt(q_ref[:, hs], k_ref[:, hs]) * (X_HEAD_DIM ** -0.5)
        s = s - jnp.max(s, axis=-1, keepdims=True)
        e = jnp.exp(s)
        p = e / jnp.sum(e, axis=-1, keepdims=True)
        outs.append(_dot(p.astype(BF16), v_ref[:, hs]))
    o_ref[...] = jnp.concatenate(outs, axis=1).astype(o_ref.dtype)


def _xattn(cols3, mem_k, mem_v, tq):
    b, t, _ = cols3.shape
    mspec = pl.BlockSpec((None, N_MEM, D_MODEL), lambda bi, ti: (bi, 0, 0))
    return pl.pallas_call(
        _xattn_kernel,
        grid=(b, t // tq),
        in_specs=[pl.BlockSpec((None, tq, D_MODEL), lambda bi, ti: (bi, ti, COL_Q)), mspec, mspec],
        out_specs=pl.BlockSpec((None, tq, D_MODEL), lambda bi, ti: (bi, ti, 0)),
        out_shape=jax.ShapeDtypeStruct((b, t, D_MODEL), BF16),
        compiler_params=_cparams(("arbitrary", "arbitrary")),
        name="xattn",
    )(cols3, mem_k, mem_v)


def _merge_kernel(x_ref, gate_ref, ya_ref, yb_ref, yc_ref, wb_ref, wo_ref, o_ref):
    gate = _sigmoid(gate_ref[...].astype(F32))
    merged = None
    for bi, y_ref in enumerate((ya_ref, yb_ref, yc_ref)):
        term = gate[:, bi * D_MODEL:(bi + 1) * D_MODEL] * _dot(y_ref[...], wb_ref[bi])
        merged = term if merged is None else merged + term
    o_ref[...] = x_ref[...] + _dot(merged.astype(BF16), wo_ref[...])


def _merge(x, cols, ya, yb, yc, wb, wo, tm):
    t, d = x.shape
    tok = lambda w: pl.BlockSpec((tm, w), lambda i: (i, 0))
    return pl.pallas_call(
        _merge_kernel,
        grid=(t // tm,),
        in_specs=[tok(d), pl.BlockSpec((tm, 3 * d), lambda i: (i, COL_GATE_BLOCK3)), tok(d), tok(d), tok(d),
                  pl.BlockSpec(wb.shape, lambda i: (0, 0, 0)), pl.BlockSpec(wo.shape, lambda i: (0, 0))],
        out_specs=tok(d),
        out_shape=jax.ShapeDtypeStruct((t, d), F32),
        compiler_params=_cparams(("arbitrary",)),
        name="merge",
    )(x, cols, ya, yb, yc, wb, wo)


def _ffn_kernel(x_ref, g_ref, wu_ref, wd_ref, o_ref, h_scr):
    f = pl.program_id(1)

    @pl.when(f == 0)
    def _init():
        x = x_ref[...]
        h_scr[...] = _rmsnorm(x, g_ref[...]).astype(BF16)
        o_ref[...] = x

    up = _dot(h_scr[...], wu_ref[...])
    act = jnp.square(jnp.maximum(up, 0.0)).astype(BF16)
    o_ref[...] += _dot(act, wd_ref[...])


def _ffn(x, g, wu, wd, tm, tf):
    t, d = x.shape
    return pl.pallas_call(
        _ffn_kernel,
        grid=(t // tm, D_FF // tf),
        in_specs=[pl.BlockSpec((tm, d), lambda i, f: (i, 0)),
                  pl.BlockSpec((1, d), lambda i, f: (0, 0)),
                  pl.BlockSpec((d, tf), lambda i, f: (0, f)),
                  pl.BlockSpec((tf, d), lambda i, f: (f, 0))],
        out_specs=pl.BlockSpec((tm, d), lambda i, f: (i, 0)),
        out_shape=jax.ShapeDtypeStruct((t, d), F32),
        scratch_shapes=[pltpu.VMEM((tm, d), BF16)],
        compiler_params=_cparams(("arbitrary", "arbitrary")),
        name="ffn",
    )(x, g, wu, wd)


def _norm_kernel(x_ref, g_ref, o_ref):
    o_ref[...] = _rmsnorm(x_ref[...], g_ref[...])


def _final_norm(x, g, tm):
    t, d = x.shape
    return pl.pallas_call(
        _norm_kernel,
        grid=(t // tm,),
        in_specs=[pl.BlockSpec((tm, d), lambda i: (i, 0)), pl.BlockSpec((1, d), lambda i: (0, 0))],
        out_specs=pl.BlockSpec((tm, d), lambda i: (i, 0)),
        out_shape=jax.ShapeDtypeStruct((t, d), F32),
        compiler_params=_cparams(("arbitrary",)),
        name="final_norm",
    )(x, g)


def _permute_cols(a):
    return jnp.concatenate([a[..., :3 * D_MODEL], a[..., RWKV_COLS:], a[..., 3 * D_MODEL:RWKV_COLS]], axis=-1)


def _unpermute_rwkv_row(row):
    return jnp.concatenate([row[..., :3 * D_MODEL], row[..., 9 * D_MODEL:]], axis=-1)


def _state_to_pairs(s):
    b = s.shape[0]
    s = s.reshape(b, N_PAIRS, 2, HEAD_DIM, HEAD_DIM)
    z = jnp.zeros_like(s[:, :, 0])
    top = jnp.concatenate([s[:, :, 0], z], axis=-1)
    bot = jnp.concatenate([z, s[:, :, 1]], axis=-1)
    return jnp.concatenate([top, bot], axis=-2)


def _pairs_to_state(s2):
    b = s2.shape[0]
    s0 = s2[:, :, :HEAD_DIM, :HEAD_DIM]
    s1 = s2[:, :, HEAD_DIM:, HEAD_DIM:]
    return jnp.stack([s0, s1], axis=2).reshape(b, N_HEADS, HEAD_DIM, HEAD_DIM)


def _pick(t, candidates):
    for c in candidates:
        if t % c == 0:
            return c
    return t


def _group_layer(x, b, t, mem_k, mem_v, prev_row, s0_pairs, lp):
    n_tok = b * t
    cols = _proj(x, lp["norm_mix_g"], lp["w_in"], BF16, _pick(n_tok, (512, 256)), IN_COLS // 2)
    cols3 = cols.reshape(b, t, IN_COLS)
    new_row = _unpermute_rwkv_row(cols3[:, t - 1:t, :]).astype(F32)

    tp = -(-t // CHUNK) * CHUNK
    cols3p = cols3 if tp == t else jnp.pad(cols3, ((0, 0), (0, tp - t), (0, 0)))
    prev = (prev_row[..., :D_MODEL], prev_row[..., D_MODEL:2 * D_MODEL], prev_row[..., 2 * D_MODEL:3 * D_MODEL],
            prev_row[..., 3 * D_MODEL:])
    y_a, s_new = _rwkv(cols3p, prev, s0_pairs, lp["rwkv_vec"], lp["rwkv_mu_lo"], lp["rwkv_lw"], lp["e256"], t)
    y_a = y_a[:, :t].reshape(n_tok, D_MODEL)

    lc = min(t, GMLP_CHUNK)
    n_chunks = _pick(t // lc, (4, 2, 1))
    ws = lp["sgu_w_s"][:, :lc, :lc]
    bs_full = jnp.tile(jnp.repeat(lp["sgu_b_s"][:, :lc].T, GMLP_CHUNK, axis=1), (n_chunks, 1))
    sgu_out = _sgu(cols3, lp["sgu_ln"], ws, bs_full, lc, n_chunks, want_v=t < GMLP_CHUNK)
    y_b = sgu_out[0].reshape(n_tok, D_MODEL)
    v_rows = sgu_out[1] if t < GMLP_CHUNK else None

    y_c = _xattn(cols3, mem_k, mem_v, _pick(t, (512, 256, 128))).reshape(n_tok, D_MODEL)

    x = _merge(x, cols, y_a, y_b, y_c, lp["w_branch"], lp["w_out"], _pick(n_tok, (512, 256)))
    x = _ffn(x, lp["norm_ffn_g"], lp["w_ffn_up"], lp["w_ffn_down"], _pick(n_tok, (1024, 512, 256)), 1024)
    return x, new_row, s_new, v_rows


def kernel(x_prompt, x_sample, cache_mem_k, cache_mem_v, state_wkv, state_shift, mem_prompt, norm_mix_g, norm_mem_g, norm_ffn_g, norm_final_g, w_in, w_mem_kv, rwkv_mu, rwkv_w0, rwkv_w2, rwkv_a0, rwkv_a2, rwkv_g2, rwkv_k_k, rwkv_k_a, rwkv_r_k, rwkv_lnx_g, rwkv_lnx_b, sgu_ln_g, sgu_ln_b, sgu_w_s, sgu_b_s, w_branch, w_out, w_ffn_up, w_ffn_down):
    depth = w_in.shape[0]
    bp, tp, _ = x_prompt.shape
    bs, ts, _ = x_sample.shape

    w_in_b = _permute_cols(w_in).astype(BF16)
    w_kv_b = w_mem_kv.astype(BF16)
    w_branch_b = w_branch.astype(BF16)
    w_out_b = w_out.astype(BF16)
    w_up_b = w_ffn_up.astype(BF16)
    w_down_b = w_ffn_down.astype(BF16)
    zeros = lambda n: jnp.zeros((depth, n, D_MODEL), F32)
    lw = jnp.stack([jnp.concatenate([rwkv_w2, zeros(192)], axis=1),
                    jnp.concatenate([zeros(64), rwkv_a2, zeros(128)], axis=1),
                    jnp.concatenate([zeros(128), rwkv_g2], axis=1)], axis=1).astype(BF16)
    rwkv_vec = jnp.stack([rwkv_mu[:, :D_MODEL], rwkv_mu[:, D_MODEL:2 * D_MODEL], rwkv_mu[:, 2 * D_MODEL:3 * D_MODEL],
                          rwkv_w0, rwkv_a0, rwkv_k_k, rwkv_k_a, rwkv_r_k.reshape(depth, D_MODEL),
                          rwkv_lnx_g, rwkv_lnx_b] + [jnp.zeros((depth, D_MODEL), F32)] * 6, axis=1)
    sgu_ln = jnp.stack([sgu_ln_g, sgu_ln_b] + [jnp.zeros((depth, D_MODEL), F32)] * 6, axis=1)
    hq = 4 * HEAD_DIM
    e256 = (jnp.arange(hq)[:, None] // HEAD_DIM == jnp.arange(hq)[None, :] // HEAD_DIM).astype(BF16)

    xp = x_prompt.reshape(bp * tp, D_MODEL)
    xs = x_sample.reshape(bs * ts, D_MODEL)
    mem_flat = mem_prompt.reshape(bp * N_MEM, D_MODEL)
    prompt_row0 = jnp.zeros((bp, 1, RWKV_COLS), F32)
    prompt_s0 = jnp.zeros((bp, N_PAIRS, PAIR, PAIR), F32)

    mk_p, mv_p, wkv_p, row_p, wkv_s, row_s, v_s = [], [], [], [], [], [], []
    for l in range(depth):
        lp = {
            "norm_mix_g": norm_mix_g[l][None], "w_in": w_in_b[l],
            "rwkv_vec": rwkv_vec[l], "rwkv_mu_lo": rwkv_mu[l][None, 3 * D_MODEL:], "rwkv_lw": lw[l], "e256": e256,
            "sgu_ln": sgu_ln[l], "sgu_w_s": sgu_w_s[l], "sgu_b_s": sgu_b_s[l],
            "w_branch": w_branch_b[l], "w_out": w_out_b[l],
            "norm_ffn_g": norm_ffn_g[l][None], "w_ffn_up": w_up_b[l], "w_ffn_down": w_down_b[l],
        }
        kv = _proj(mem_flat, norm_mem_g[l][None], w_kv_b[l], F32, _pick(bp * N_MEM, (512, 256)), 2 * D_MODEL)
        mem_k = kv[:, :D_MODEL].reshape(bp, N_MEM, D_MODEL)
        mem_v = kv[:, D_MODEL:].reshape(bp, N_MEM, D_MODEL)
        xp, r_p, s_p, _ = _group_layer(xp, bp, tp, mem_k.astype(BF16), mem_v.astype(BF16), prompt_row0, prompt_s0, lp)
        mk_p.append(mem_k.reshape(bp, N_MEM, X_HEADS, X_HEAD_DIM))
        mv_p.append(mem_v.reshape(bp, N_MEM, X_HEADS, X_HEAD_DIM))
        wkv_p.append(_pairs_to_state(s_p))
        row_p.append(r_p)

        ck = cache_mem_k[l].reshape(bs, N_MEM, D_MODEL).astype(BF16)
        cv = cache_mem_v[l].reshape(bs, N_MEM, D_MODEL).astype(BF16)
        xs, r_s, s_s, vr = _group_layer(xs, bs, ts, ck, cv, _permute_rwkv_prev(state_shift[l]),
                                        _state_to_pairs(state_wkv[l]), lp)
        wkv_s.append(_pairs_to_state(s_s))
        row_s.append(r_s)
        v_s.append(vr)

    g_fin = norm_final_g[None]
    y_prompt = _final_norm(xp, g_fin, _pick(bp * tp, (1024, 512, 256))).reshape(bp, tp, D_MODEL)
    y_sample = _final_norm(xs, g_fin, _pick(bs * ts, (1024, 512, 256))).reshape(bs, ts, D_MODEL)
    return (y_prompt, y_sample, jnp.stack(mk_p), jnp.stack(mv_p), jnp.stack(wkv_p), jnp.stack(row_p),
            jnp.stack(wkv_s), jnp.stack(row_s), jnp.stack(v_s))


def _permute_rwkv_prev(row):
    return row
```

```python
import functools
import math

import jax
import jax.numpy as jnp
from jax import lax
from jax.experimental import pallas as pl
from jax.experimental.pallas import tpu as pltpu

F32 = jnp.float32
BF16 = jnp.bfloat16

D_MODEL = 1024
HEAD_DIM = 64
N_HEADS = D_MODEL // HEAD_DIM
PAIR = 2 * HEAD_DIM
N_PAIRS = D_MODEL // PAIR
LORA_COLS = 256
RWKV_COLS = 3 * D_MODEL + LORA_COLS
GMLP_CHUNK = 128
GMLP_GROUPS = 8
N_MEM = 256
X_HEADS = 4
X_HEAD_DIM = D_MODEL // X_HEADS
D_FF = 4 * D_MODEL
IN_COLS = 9 * D_MODEL + LORA_COLS
RMS_EPS = 1e-6
LN_EPS = 1e-5
GN_EPS = 64e-5
CHUNK = 64
DECAY_SCALE = math.exp(-0.5)
VMEM_LIMIT = 56 * 1024 * 1024
PASSES_INVERSE = 3
PASSES_STATE = 1
PASSES_SOLVE = 1

COL_R, COL_K, COL_V, COL_U, COL_GV, COL_Q = 0, 1, 2, 3, 4, 5
COL_GATE_BLOCK3 = 2
COL_LORA_BLOCK = 9 * D_MODEL // LORA_COLS


def _cparams(sem):
    return pltpu.CompilerParams(dimension_semantics=sem, vmem_limit_bytes=VMEM_LIMIT)


def _dot(a, b):
    return jnp.dot(a, b, preferred_element_type=F32)


def _dot_nt(a, b):
    return lax.dot_general(a, b, (((1,), (1,)), ((), ())), preferred_element_type=F32)


def _dot_tn(a, b):
    return lax.dot_general(a, b, (((0,), (0,)), ((), ())), preferred_element_type=F32)


def _split(x):
    hi = x.astype(BF16)
    lo = (x - hi.astype(F32)).astype(BF16)
    return hi, lo


def _rmsnorm(x, g):
    return x * lax.rsqrt(jnp.mean(x * x, axis=-1, keepdims=True) + RMS_EPS) * g


def _sigmoid(x):
    return 1.0 / (1.0 + jnp.exp(-x))


def _proj_kernel(x_ref, g_ref, w_ref, o_ref):
    h = _rmsnorm(x_ref[...], g_ref[...]).astype(BF16)
    o_ref[...] = _dot(h, w_ref[...]).astype(o_ref.dtype)


def _proj(x, g, w, out_dtype, tm, tn):
    t, d = x.shape
    nc = w.shape[1]
    return pl.pallas_call(
        _proj_kernel,
        grid=(nc // tn, t // tm),
        in_specs=[pl.BlockSpec((tm, d), lambda j, i: (i, 0)),
                  pl.BlockSpec((1, d), lambda j, i: (0, 0)),
                  pl.BlockSpec((d, tn), lambda j, i: (0, j))],
        out_specs=pl.BlockSpec((tm, tn), lambda j, i: (i, j)),
        out_shape=jax.ShapeDtypeStruct((t, nc), out_dtype),
        compiler_params=_cparams(("arbitrary", "arbitrary")),
        name="proj",
    )(x, g, w)


def _head_sum(x, e_ref):
    c = x.shape[0]
    q = 4 * HEAD_DIM
    xs = jnp.concatenate([x[:, i * q:(i + 1) * q] for i in range(D_MODEL // q)], axis=0)
    r = _dot(xs.astype(BF16), e_ref[...])
    return jnp.concatenate([r[i * c:(i + 1) * c] for i in range(D_MODEL // q)], axis=1)


def _rwkv_kernel(r_ref, k_ref, v_ref, lo_ref, pr_ref, pk_ref, pv_ref, plo_ref, s0_ref,
                 vec_ref, mulo_ref, lw_ref, e_ref,
                 y_ref, sout_ref,
                 s_scr, prr, prk, prv, prlo, *, t_valid):
    c_idx = pl.program_id(1)
    C = CHUNK

    @pl.when(c_idx == 0)
    def _init():
        s_scr[...] = s0_ref[...]
        prr[...] = pr_ref[...]
        prk[...] = pk_ref[...]
        prv[...] = pv_ref[...]
        prlo[...] = plo_ref[...]

    def vec(i):
        return vec_ref[i:i + 1, :]

    def mixed(ref, prev, mu):
        x = ref[...].astype(F32)
        rolled = pltpu.roll(x, 1, 0)
        row = lax.broadcasted_iota(jnp.int32, x.shape, 0)
        shifted = jnp.where(row == 0, prev[...], rolled)
        prev[...] = x[C - 1:C, :]
        return x + (shifted - x) * mu

    r = mixed(r_ref, prr, vec(0))
    k = mixed(k_ref, prk, vec(1))
    v = mixed(v_ref, prv, vec(2))
    lo = mixed(lo_ref, prlo, mulo_ref[...])

    lane_lo = lax.broadcasted_iota(jnp.int32, lo.shape, 1)
    act = jnp.where(lane_lo < 64, jnp.tanh(lo), jnp.where(lane_lo < 128, lo, _sigmoid(lo))).astype(BF16)
    z = vec(3) + _dot(act, lw_ref[0])
    a = _sigmoid(vec(4) + _dot(act, lw_ref[1]))
    g = _dot(act, lw_ref[2])
    logw = -DECAY_SCALE * _sigmoid(z)

    kk = k * vec(5)
    kk = kk * lax.rsqrt(jnp.maximum(_head_sum(kk * kk, e_ref), 1e-24))
    k = k * (1.0 + (a - 1.0) * vec(6))
    bonus = _head_sum(r * k * vec(7), e_ref)

    if t_valid < C:
        valid = lax.broadcasted_iota(jnp.int32, (C, 1), 0) < t_valid
        logw = jnp.where(valid, logw, 0.0)
        kk = jnp.where(valid, kk, 0.0)
        k = jnp.where(valid, k, 0.0)
        v = jnp.where(valid, v, 0.0)

    ti = lax.broadcasted_iota(jnp.int32, (C, C), 0)
    tj = lax.broadcasted_iota(jnp.int32, (C, C), 1)
    tril = jnp.where(tj <= ti, 1.0, 0.0).astype(BF16)
    lw_hi, lw_lo = _split(logw)
    cum = _dot(tril, lw_hi) + _dot(tril, lw_lo)
    e_inc = jnp.exp(cum)
    e_dec = jnp.exp(-cum)
    a_t = -kk * jnp.exp(cum - logw)
    r_t = r * e_inc
    b_t = kk * a * e_dec
    k_t = k * e_dec
    w_end = e_inc[C - 1:C, :]
    b_h = b_t * w_end
    k_h = k_t * w_end

    lane = lax.broadcasted_iota(jnp.int32, (C, PAIR), 1)
    first = lane < HEAD_DIM
    ri = lax.broadcasted_iota(jnp.int32, (2 * C, 2 * C), 0)
    ci = lax.broadcasted_iota(jnp.int32, (2 * C, 2 * C), 1)
    strict = ci < ri
    incl = ci <= ri
    eye = jnp.where(ci == ri, 1.0, 0.0)

    def stack2(x):
        return jnp.concatenate([jnp.where(first, x, 0.0), jnp.where(first, 0.0, x)], axis=0)

    def mm3(x, y):
        xh, xl = _split(x)
        yh, yl = _split(y)
        return _dot(xh, yh) + _dot(xh, yl) + _dot(xl, yh)

    def mm(x, y, passes):
        if passes == 1:
            return _dot(x.astype(BF16), y.astype(BF16))
        return mm3(x, y)

    P = range(N_PAIRS)
    sls = [slice(p * PAIR, (p + 1) * PAIR) for p in P]
    lhs = [jnp.concatenate([stack2(a_t[:, s]), stack2(r_t[:, s])], axis=0).astype(BF16) for s in sls]
    rhs = [jnp.concatenate([stack2(b_t[:, s]), stack2(k_t[:, s])], axis=0).astype(BF16) for s in sls]
    pm = [_dot_nt(lhs[p], rhs[p]) for p in P]
    m = [jnp.where(strict, pm[p][0:2 * C, 0:2 * C], 0.0) for p in P]
    akrk = [jnp.concatenate([jnp.where(strict, pm[p][0:2 * C, 2 * C:4 * C], 0.0),
                             jnp.where(incl, pm[p][2 * C:4 * C, 2 * C:4 * C], 0.0)], axis=0).astype(BF16) for p in P]
    rb = [jnp.where(incl, pm[p][2 * C:4 * C, 0:2 * C], 0.0).astype(BF16) for p in P]
    tinv = [eye + m[p] for p in P]
    for _ in range(int(math.log2(C)) - 1):
        m = [mm(m[p], m[p], PASSES_INVERSE) for p in P]
        tinv = [tinv[p] + mm(tinv[p], m[p], PASSES_INVERSE) for p in P]
    s2 = [s_scr[p] for p in P]
    if PASSES_STATE == 1:
        as_rs = [_dot_nt(lhs[p], s2[p].astype(BF16)) for p in P]
    else:
        s_hl = [_split(s2[p]) for p in P]
        as_rs = [_dot_nt(lhs[p], s_hl[p][0]) + _dot_nt(lhs[p], s_hl[p][1]) for p in P]
    v2 = [stack2(v[:, s]).astype(BF16) for s in sls]
    pv = [_dot(akrk[p], v2[p]) for p in P]
    sa = [mm(tinv[p], as_rs[p][0:2 * C] + pv[p][0:2 * C], PASSES_SOLVE) for p in P]
    sab = [sa[p].astype(BF16) for p in P]
    y2 = [as_rs[p][2 * C:] + pv[p][2 * C:] + _dot(rb[p], sab[p]) for p in P]
    y_pairs = [y2[p][0:C] + y2[p][C:2 * C] for p in P]
    bhkh = [jnp.concatenate([stack2(b_h[:, s]), stack2(k_h[:, s])], axis=0).astype(BF16) for s in sls]
    for p in P:
        upd = _dot_tn(jnp.concatenate([sab[p], v2[p]], axis=0), bhkh[p])
        s_scr[p] = s2[p] * w_end[:, sls[p]] + upd

    y = jnp.concatenate(y_pairs, axis=1)
    mean = _head_sum(y, e_ref) * (1.0 / HEAD_DIM)
    yc = y - mean
    var = _head_sum(yc * yc, e_ref) * (1.0 / HEAD_DIM)
    yn = yc * lax.rsqrt(var + GN_EPS) * vec(8) + vec(9)
    y_ref[...] = ((yn + bonus * v) * g).astype(y_ref.dtype)

    @pl.when(c_idx == pl.num_programs(1) - 1)
    def _fin():
        sout_ref[...] = s_scr[...]


def _rwkv(cols3, prev, s0, vec, mulo, lw, e256, t_valid):
    b, tp, _ = cols3.shape
    C = CHUNK
    col = lambda j: pl.BlockSpec((None, C, D_MODEL), lambda bi, ci: (bi, ci, j))
    rowspec = lambda w: pl.BlockSpec((None, 1, w), lambda bi, ci: (bi, 0, 0))
    full = lambda a: pl.BlockSpec(a.shape, lambda bi, ci: (0,) * a.ndim)
    sspec = pl.BlockSpec((None, N_PAIRS, PAIR, PAIR), lambda bi, ci: (bi, 0, 0, 0))
    return pl.pallas_call(
        functools.partial(_rwkv_kernel, t_valid=min(t_valid, C)),
        grid=(b, tp // C),
        in_specs=[col(COL_R), col(COL_K), col(COL_V),
                  pl.BlockSpec((None, C, LORA_COLS), lambda bi, ci: (bi, ci, COL_LORA_BLOCK)),
                  rowspec(D_MODEL), rowspec(D_MODEL), rowspec(D_MODEL), rowspec(LORA_COLS),
                  sspec, full(vec), full(mulo), full(lw), full(e256)],
        out_specs=[pl.BlockSpec((None, C, D_MODEL), lambda bi, ci: (bi, ci, 0)), sspec],
        out_shape=[jax.ShapeDtypeStruct((b, tp, D_MODEL), BF16),
                   jax.ShapeDtypeStruct(s0.shape, F32)],
        scratch_shapes=[pltpu.VMEM((N_PAIRS, PAIR, PAIR), F32),
                        pltpu.VMEM((1, D_MODEL), F32), pltpu.VMEM((1, D_MODEL), F32),
                        pltpu.VMEM((1, D_MODEL), F32), pltpu.VMEM((1, LORA_COLS), F32)],
        compiler_params=_cparams(("arbitrary", "arbitrary")),
        name="rwkv7",
    )(cols3, cols3, cols3, cols3, *prev, s0, vec, mulo, lw, e256)


def _gelu(x):
    return 0.5 * x * (1.0 + jnp.tanh(math.sqrt(2.0 / math.pi) * (x + 0.044715 * (x * x * x))))


def _sgu_kernel(u_ref, v_ref, ln_ref, ws_ref, bs_ref, y_ref, *maybe_v_out, lc):
    u = _gelu(u_ref[...].astype(F32))
    v = _gelu(v_ref[...].astype(F32))
    mu = jnp.mean(v, axis=-1, keepdims=True)
    vc = v - mu
    var = jnp.mean(vc * vc, axis=-1, keepdims=True)
    v = vc * lax.rsqrt(var + LN_EPS) * ln_ref[0:1, :] + ln_ref[1:2, :]
    if maybe_v_out:
        maybe_v_out[0][...] = v
    vb = v.astype(BF16)
    n_chunks = v.shape[0] // lc
    ri = lax.broadcasted_iota(jnp.int32, (lc, lc), 0)
    ci = lax.broadcasted_iota(jnp.int32, (lc, lc), 1)
    bias = bs_ref[...]
    outs = []
    for gi in range(GMLP_GROUPS):
        gs = slice(gi * GMLP_CHUNK, (gi + 1) * GMLP_CHUNK)
        w = jnp.where(ci <= ri, ws_ref[gi], 0.0).astype(BF16)
        rhs = jnp.concatenate([vb[n * lc:(n + 1) * lc, gs] for n in range(n_chunks)], axis=1)
        sv = _dot(w, rhs)
        sv = jnp.concatenate([sv[:, n * GMLP_CHUNK:(n + 1) * GMLP_CHUNK] for n in range(n_chunks)], axis=0)
        outs.append(sv + bias[:, gs])
    sv = jnp.concatenate(outs, axis=1)
    y_ref[...] = (u * sv).astype(y_ref.dtype)


def _sgu(cols3, ln, ws, bs_full, lc, n_chunks, want_v):
    b, t, _ = cols3.shape
    tt = lc * n_chunks
    col = lambda j: pl.BlockSpec((None, tt, D_MODEL), lambda bi, ti: (bi, ti, j))
    full = lambda a: pl.BlockSpec(a.shape, lambda bi, ti: (0,) * a.ndim)
    ospec = pl.BlockSpec((None, tt, D_MODEL), lambda bi, ti: (bi, ti, 0))
    out_specs = [ospec]
    out_shape = [jax.ShapeDtypeStruct((b, t, D_MODEL), BF16)]
    if want_v:
        out_specs.append(ospec)
        out_shape.append(jax.ShapeDtypeStruct((b, t, D_MODEL), F32))
    return pl.pallas_call(
        functools.partial(_sgu_kernel, lc=lc),
        grid=(b, t // tt),
        in_specs=[col(COL_U), col(COL_GV), full(ln), full(ws), full(bs_full)],
        out_specs=out_specs,
        out_shape=out_shape,
        compiler_params=_cparams(("arbitrary", "arbitrary")),
        name="sgu",
    )(cols3, cols3, ln, ws, bs_full)


def _xattn_kernel(q_ref, k_ref, v_ref, o_ref):
    outs = []
    for h in range(X_HEADS):
        hs = slice(h * X_HEAD_DIM, (h + 1) * X_HEAD_DIM)
        s = _dot_nt(q_ref[:, hs], k_ref[:, hs]) * (X_HEAD_DIM ** -0.5)
        s = s - jnp.max(s, axis=-1, keepdims=True)
        e = jnp.exp(s)
        p = e / jnp.sum(e, axis=-1, keepdims=True)
        outs.append(_dot(p.astype(BF16), v_ref[:, hs]))
    o_ref[...] = jnp.concatenate(outs, axis=1).astype(o_ref.dtype)


def _xattn(cols3, mem_k, mem_v, tq):
    b, t, _ = cols3.shape
    mspec = pl.BlockSpec((None, N_MEM, D_MODEL), lambda bi, ti: (bi, 0, 0))
    return pl.pallas_call(
        _xattn_kernel,
        grid=(b, t // tq),
        in_specs=[pl.BlockSpec((None, tq, D_MODEL), lambda bi, ti: (bi, ti, COL_Q)), mspec, mspec],
        out_specs=pl.BlockSpec((None, tq, D_MODEL), lambda bi, ti: (bi, ti, 0)),
        out_shape=jax.ShapeDtypeStruct((b, t, D_MODEL), BF16),
        compiler_params=_cparams(("arbitrary", "arbitrary")),
        name="xattn",
    )(cols3, mem_k, mem_v)


def _merge_kernel(x_ref, gate_ref, ya_ref, yb_ref, yc_ref, wb_ref, wo_ref, o_ref):
    gate = _sigmoid(gate_ref[...].astype(F32))
    merged = None
    for bi, y_ref in enumerate((ya_ref, yb_ref, yc_ref)):
        term = gate[:, bi * D_MODEL:(bi + 1) * D_MODEL] * _dot(y_ref[...], wb_ref[bi])
        merged = term if merged is None else merged + term
    o_ref[...] = x_ref[...] + _dot(merged.astype(BF16), wo_ref[...])


def _merge(x, cols, ya, yb, yc, wb, wo, tm):
    t, d = x.shape
    tok = lambda w: pl.BlockSpec((tm, w), lambda i: (i, 0))
    return pl.pallas_call(
        _merge_kernel,
        grid=(t // tm,),
        in_specs=[tok(d), pl.BlockSpec((tm, 3 * d), lambda i: (i, COL_GATE_BLOCK3)), tok(d), tok(d), tok(d),
                  pl.BlockSpec(wb.shape, lambda i: (0, 0, 0)), pl.BlockSpec(wo.shape, lambda i: (0, 0))],
        out_specs=tok(d),
        out_shape=jax.ShapeDtypeStruct((t, d), F32),
        compiler_params=_cparams(("arbitrary",)),
        name="merge",
    )(x, cols, ya, yb, yc, wb, wo)


def _ffn_kernel(x_ref, g_ref, wu_ref, wd_ref, o_ref, h_scr):
    f = pl.program_id(1)

    @pl.when(f == 0)
    def _init():
        x = x_ref[...]
        h_scr[...] = _rmsnorm(x, g_ref[...]).astype(BF16)
        o_ref[...] = x

    up = _dot(h_scr[...], wu_ref[...])
    act = jnp.square(jnp.maximum(up, 0.0)).astype(BF16)
    o_ref[...] += _dot(act, wd_ref[...])


def _ffn(x, g, wu, wd, tm, tf):
    t, d = x.shape
    return pl.pallas_call(
        _ffn_kernel,
        grid=(t // tm, D_FF // tf),
        in_specs=[pl.BlockSpec((tm, d), lambda i, f: (i, 0)),
                  pl.BlockSpec((1, d), lambda i, f: (0, 0)),
                  pl.BlockSpec((d, tf), lambda i, f: (0, f)),
                  pl.BlockSpec((tf, d), lambda i, f: (f, 0))],
        out_specs=pl.BlockSpec((tm, d), lambda i, f: (i, 0)),
        out_shape=jax.ShapeDtypeStruct((t, d), F32),
        scratch_shapes=[pltpu.VMEM((tm, d), BF16)],
        compiler_params=_cparams(("arbitrary", "arbitrary")),
        name="ffn",
    )(x, g, wu, wd)


def _norm_kernel(x_ref, g_ref, o_ref):
    o_ref[...] = _rmsnorm(x_ref[...], g_ref[...])


def _final_norm(x, g, tm):
    t, d = x.shape
    return pl.pallas_call(
        _norm_kernel,
        grid=(t // tm,),
        in_specs=[pl.BlockSpec((tm, d), lambda i: (i, 0)), pl.BlockSpec((1, d), lambda i: (0, 0))],
        out_specs=pl.BlockSpec((tm, d), lambda i: (i, 0)),
        out_shape=jax.ShapeDtypeStruct((t, d), F32),
        compiler_params=_cparams(("arbitrary",)),
        name="final_norm",
    )(x, g)


def _permute_cols(a):
    return jnp.concatenate([a[..., :3 * D_MODEL], a[..., RWKV_COLS:], a[..., 3 * D_MODEL:RWKV_COLS]], axis=-1)


def _unpermute_rwkv_row(row):
    return jnp.concatenate([row[..., :3 * D_MODEL], row[..., 9 * D_MODEL:]], axis=-1)


def _state_to_pairs(s):
    b = s.shape[0]
    s = s.reshape(b, N_PAIRS, 2, HEAD_DIM, HEAD_DIM)
    z = jnp.zeros_like(s[:, :, 0])
    top = jnp.concatenate([s[:, :, 0], z], axis=-1)
    bot = jnp.concatenate([z, s[:, :, 1]], axis=-1)
    return jnp.concatenate([top, bot], axis=-2)


def _pairs_to_state(s2):
    b = s2.shape[0]
    s0 = s2[:, :, :HEAD_DIM, :HEAD_DIM]
    s1 = s2[:, :, HEAD_DIM:, HEAD_DIM:]
    return jnp.stack([s0, s1], axis=2).reshape(b, N_HEADS, HEAD_DIM, HEAD_DIM)


def _pick(t, candidates):
    for c in candidates:
        if t % c == 0:
            return c
    return t


def _group_layer(x, b, t, mem_k, mem_v, prev_row, s0_pairs, lp):
    n_tok = b * t
    cols = _proj(x, lp["norm_mix_g"], lp["w_in"], BF16, _pick(n_tok, (512, 256)), IN_COLS // 2)
    cols3 = cols.reshape(b, t, IN_COLS)
    new_row = _unpermute_rwkv_row(cols3[:, t - 1:t, :]).astype(F32)

    tp = -(-t // CHUNK) * CHUNK
    cols3p = cols3 if tp == t else jnp.pad(cols3, ((0, 0), (0, tp - t), (0, 0)))
    prev = (prev_row[..., :D_MODEL], prev_row[..., D_MODEL:2 * D_MODEL], prev_row[..., 2 * D_MODEL:3 * D_MODEL],
            prev_row[..., 3 * D_MODEL:])
    y_a, s_new = _rwkv(cols3p, prev, s0_pairs, lp["rwkv_vec"], lp["rwkv_mu_lo"], lp["rwkv_lw"], lp["e256"], t)
    y_a = y_a[:, :t].reshape(n_tok, D_MODEL)

    lc = min(t, GMLP_CHUNK)
    n_chunks = _pick(t // lc, (4, 2, 1))
    ws = lp["sgu_w_s"][:, :lc, :lc]
    bs_full = jnp.tile(jnp.repeat(lp["sgu_b_s"][:, :lc].T, GMLP_CHUNK, axis=1), (n_chunks, 1))
    sgu_out = _sgu(cols3, lp["sgu_ln"], ws, bs_full, lc, n_chunks, want_v=t < GMLP_CHUNK)
    y_b = sgu_out[0].reshape(n_tok, D_MODEL)
    v_rows = sgu_out[1] if t < GMLP_CHUNK else None

    y_c = _xattn(cols3, mem_k, mem_v, _pick(t, (512, 256, 128))).reshape(n_tok, D_MODEL)

    x = _merge(x, cols, y_a, y_b, y_c, lp["w_branch"], lp["w_out"], _pick(n_tok, (512, 256)))
    x = _ffn(x, lp["norm_ffn_g"], lp["w_ffn_up"], lp["w_ffn_down"], _pick(n_tok, (1024, 512, 256)), 1024)
    return x, new_row, s_new, v_rows


def kernel(x_prompt, x_sample, cache_mem_k, cache_mem_v, state_wkv, state_shift, mem_prompt, norm_mix_g, norm_mem_g, norm_ffn_g, norm_final_g, w_in, w_mem_kv, rwkv_mu, rwkv_w0, rwkv_w2, rwkv_a0, rwkv_a2, rwkv_g2, rwkv_k_k, rwkv_k_a, rwkv_r_k, rwkv_lnx_g, rwkv_lnx_b, sgu_ln_g, sgu_ln_b, sgu_w_s, sgu_b_s, w_branch, w_out, w_ffn_up, w_ffn_down):
    depth = w_in.shape[0]
    bp, tp, _ = x_prompt.shape
    bs, ts, _ = x_sample.shape

    w_in_b = _permute_cols(w_in).astype(BF16)
    w_kv_b = w_mem_kv.astype(BF16)
    w_branch_b = w_branch.astype(BF16)
    w_out_b = w_out.astype(BF16)
    w_up_b = w_ffn_up.astype(BF16)
    w_down_b = w_ffn_down.astype(BF16)
    zeros = lambda n: jnp.zeros((depth, n, D_MODEL), F32)
    lw = jnp.stack([jnp.concatenate([rwkv_w2, zeros(192)], axis=1),
                    jnp.concatenate([zeros(64), rwkv_a2, zeros(128)], axis=1),
                    jnp.concatenate([zeros(128), rwkv_g2], axis=1)], axis=1).astype(BF16)
    rwkv_vec = jnp.stack([rwkv_mu[:, :D_MODEL], rwkv_mu[:, D_MODEL:2 * D_MODEL], rwkv_mu[:, 2 * D_MODEL:3 * D_MODEL],
                          rwkv_w0, rwkv_a0, rwkv_k_k, rwkv_k_a, rwkv_r_k.reshape(depth, D_MODEL),
                          rwkv_lnx_g, rwkv_lnx_b] + [jnp.zeros((depth, D_MODEL), F32)] * 6, axis=1)
    sgu_ln = jnp.stack([sgu_ln_g, sgu_ln_b] + [jnp.zeros((depth, D_MODEL), F32)] * 6, axis=1)
    hq = 4 * HEAD_DIM
    e256 = (jnp.arange(hq)[:, None] // HEAD_DIM == jnp.arange(hq)[None, :] // HEAD_DIM).astype(BF16)

    xp = x_prompt.reshape(bp * tp, D_MODEL)
    xs = x_sample.reshape(bs * ts, D_MODEL)
    mem_flat = mem_prompt.reshape(bp * N_MEM, D_MODEL)
    prompt_row0 = jnp.zeros((bp, 1, RWKV_COLS), F32)
    prompt_s0 = jnp.zeros((bp, N_PAIRS, PAIR, PAIR), F32)

    mk_p, mv_p, wkv_p, row_p, wkv_s, row_s, v_s = [], [], [], [], [], [], []
    for l in range(depth):
        lp = {
            "norm_mix_g": norm_mix_g[l][None], "w_in": w_in_b[l],
            "rwkv_vec": rwkv_vec[l], "rwkv_mu_lo": rwkv_mu[l][None, 3 * D_MODEL:], "rwkv_lw": lw[l], "e256": e256,
            "sgu_ln": sgu_ln[l], "sgu_w_s": sgu_w_s[l], "sgu_b_s": sgu_b_s[l],
            "w_branch": w_branch_b[l], "w_out": w_out_b[l],
            "norm_ffn_g": norm_ffn_g[l][None], "w_ffn_up": w_up_b[l], "w_ffn_down": w_down_b[l],
        }
        kv = _proj(mem_flat, norm_mem_g[l][None], w_kv_b[l], F32, _pick(bp * N_MEM, (512, 256)), 2 * D_MODEL)
        mem_k = kv[:, :D_MODEL].reshape(bp, N_MEM, D_MODEL)
        mem_v = kv[:, D_MODEL:].reshape(bp, N_MEM, D_MODEL)
        xp, r_p, s_p, _ = _group_layer(xp, bp, tp, mem_k.astype(BF16), mem_v.astype(BF16), prompt_row0, prompt_s0, lp)
        mk_p.append(mem_k.reshape(bp, N_MEM, X_HEADS, X_HEAD_DIM))
        mv_p.append(mem_v.reshape(bp, N_MEM, X_HEADS, X_HEAD_DIM))
        wkv_p.append(_pairs_to_state(s_p))
        row_p.append(r_p)

        ck = cache_mem_k[l].reshape(bs, N_MEM, D_MODEL).astype(BF16)
        cv = cache_mem_v[l].reshape(bs, N_MEM, D_MODEL).astype(BF16)
        xs, r_s, s_s, vr = _group_layer(xs, bs, ts, ck, cv, _permute_rwkv_prev(state_shift[l]),
                                        _state_to_pairs(state_wkv[l]), lp)
        wkv_s.append(_pairs_to_state(s_s))
        row_s.append(r_s)
        v_s.append(vr)

    g_fin = norm_final_g[None]
    y_prompt = _final_norm(xp, g_fin, _pick(bp * tp, (1024, 512, 256))).reshape(bp, tp, D_MODEL)
    y_sample = _final_norm(xs, g_fin, _pick(bs * ts, (1024, 512, 256))).reshape(bs, ts, D_MODEL)
    return (y_prompt, y_sample, jnp.stack(mk_p), jnp.stack(mv_p), jnp.stack(wkv_p), jnp.stack(row_p),
            jnp.stack(wkv_s), jnp.stack(row_s), jnp.stack(v_s))


def _permute_rwkv_prev(row):
    return row
```

```python
import functools
import math

import jax
import jax.numpy as jnp
from jax import lax
from jax.experimental import pallas as pl
from jax.experimental.pallas import tpu as pltpu

F32 = jnp.float32
BF16 = jnp.bfloat16

D_MODEL = 1024
HEAD_DIM = 64
N_HEADS = D_MODEL // HEAD_DIM
PAIR = 2 * HEAD_DIM
N_PAIRS = D_MODEL // PAIR
LORA_COLS = 256
RWKV_COLS = 3 * D_MODEL + LORA_COLS
GMLP_CHUNK = 128
GMLP_GROUPS = 8
N_MEM = 256
X_HEADS = 4
X_HEAD_DIM = D_MODEL // X_HEADS
D_FF = 4 * D_MODEL
IN_COLS = 9 * D_MODEL + LORA_COLS
RMS_EPS = 1e-6
LN_EPS = 1e-5
GN_EPS = 64e-5
CHUNK = 64
DECAY_SCALE = math.exp(-0.5)
VMEM_LIMIT = 56 * 1024 * 1024
PASSES_INVERSE = ((3, 3), (3, 3), (3, 3), (1, 1), (1, 1))
RWKV_SEQS_PER_STEP = 2

COL_R, COL_K, COL_V, COL_U, COL_GV, COL_Q = 0, 1, 2, 3, 4, 5
COL_GATE_BLOCK3 = 2
COL_LORA_BLOCK = 9 * D_MODEL // LORA_COLS


def _cparams(sem):
    return pltpu.CompilerParams(dimension_semantics=sem, vmem_limit_bytes=VMEM_LIMIT)


def _dot(a, b):
    return jnp.dot(a, b, preferred_element_type=F32)


def _dot_nt(a, b):
    return lax.dot_general(a, b, (((1,), (1,)), ((), ())), preferred_element_type=F32)


def _dot_tn(a, b):
    return lax.dot_general(a, b, (((0,), (0,)), ((), ())), preferred_element_type=F32)


def _split(x):
    hi = x.astype(BF16)
    lo = (x - hi.astype(F32)).astype(BF16)
    return hi, lo


def _rmsnorm(x, g):
    return x * lax.rsqrt(jnp.mean(x * x, axis=-1, keepdims=True) + RMS_EPS) * g


def _sigmoid(x):
    return 1.0 / (1.0 + jnp.exp(-x))


def _proj_kernel(x_ref, g_ref, w_ref, o_ref):
    h = _rmsnorm(x_ref[...], g_ref[...]).astype(BF16)
    o_ref[...] = _dot(h, w_ref[...]).astype(o_ref.dtype)


def _proj(x, g, w, out_dtype, tm, tn):
    t, d = x.shape
    nc = w.shape[1]
    return pl.pallas_call(
        _proj_kernel,
        grid=(nc // tn, t // tm),
        in_specs=[pl.BlockSpec((tm, d), lambda j, i: (i, 0)),
                  pl.BlockSpec((1, d), lambda j, i: (0, 0)),
                  pl.BlockSpec((d, tn), lambda j, i: (0, j))],
        out_specs=pl.BlockSpec((tm, tn), lambda j, i: (i, j)),
        out_shape=jax.ShapeDtypeStruct((t, nc), out_dtype),
        compiler_params=_cparams(("arbitrary", "arbitrary")),
        name="proj",
    )(x, g, w)


def _head_sum(x, e_ref):
    c = x.shape[0]
    q = 4 * HEAD_DIM
    xs = jnp.concatenate([x[:, i * q:(i + 1) * q] for i in range(D_MODEL // q)], axis=0)
    r = _dot(xs.astype(BF16), e_ref[...])
    return jnp.concatenate([r[i * c:(i + 1) * c] for i in range(D_MODEL // q)], axis=1)


def _rwkv_kernel(r_ref, k_ref, v_ref, lo_ref, pr_ref, pk_ref, pv_ref, plo_ref, s0_ref,
                 vec_ref, mulo_ref, lw_ref, e_ref,
                 y_ref, sout_ref,
                 s_scr, prr, prk, prv, prlo, *, t_valid, nb):
    c_idx = pl.program_id(1)
    C = CHUNK
    R = nb * C

    @pl.when(c_idx == 0)
    def _init():
        s_scr[...] = s0_ref[...]
        prr[...] = pr_ref[...]
        prk[...] = pk_ref[...]
        prv[...] = pv_ref[...]
        prlo[...] = plo_ref[...]

    def vec(i):
        return vec_ref[i:i + 1, :]

    row = lax.broadcasted_iota(jnp.int32, (R, 1), 0)
    tpos = row & (C - 1)
    seq_rows = [slice(b * C, (b + 1) * C) for b in range(nb)]

    def per_seq_rows(rows):
        out = rows[nb - 1]
        for b in reversed(range(nb - 1)):
            out = jnp.where(row < (b + 1) * C, rows[b], out)
        return out

    def mixed(ref, prev, mu):
        x = ref[...].astype(F32).reshape(R, ref.shape[-1])
        shifted = jnp.where(tpos == 0, per_seq_rows([prev[b] for b in range(nb)]), pltpu.roll(x, 1, 0))
        for b in range(nb):
            prev[b] = x[(b + 1) * C - 1:(b + 1) * C, :]
        return x + (shifted - x) * mu

    r = mixed(r_ref, prr, vec(0))
    k = mixed(k_ref, prk, vec(1))
    v = mixed(v_ref, prv, vec(2))
    lo = mixed(lo_ref, prlo, mulo_ref[...])

    lane_lo = lax.broadcasted_iota(jnp.int32, lo.shape, 1)
    act = jnp.where(lane_lo < 64, jnp.tanh(lo), jnp.where(lane_lo < 128, lo, _sigmoid(lo))).astype(BF16)
    z = vec(3) + _dot(act, lw_ref[0])
    a = _sigmoid(vec(4) + _dot(act, lw_ref[1]))
    g = _dot(act, lw_ref[2])
    logw = -DECAY_SCALE * _sigmoid(z)

    kk = k * vec(5)
    kk = kk * lax.rsqrt(jnp.maximum(_head_sum(kk * kk, e_ref), 1e-24))
    k = k * (1.0 + (a - 1.0) * vec(6))
    bonus = _head_sum(r * k * vec(7), e_ref)

    if t_valid < C:
        valid = tpos < t_valid
        logw = jnp.where(valid, logw, 0.0)
        kk = jnp.where(valid, kk, 0.0)
        k = jnp.where(valid, k, 0.0)
        v = jnp.where(valid, v, 0.0)

    ti = lax.broadcasted_iota(jnp.int32, (R, R), 0)
    tj = lax.broadcasted_iota(jnp.int32, (R, R), 1)
    tril = jnp.where((tj <= ti) & (tj >= ti - (ti & (C - 1))), 1.0, 0.0).astype(BF16)
    lw_hi, lw_lo = _split(logw)
    cum = _dot(tril, lw_hi) + _dot(tril, lw_lo)
    e_inc = jnp.exp(cum)
    e_dec = jnp.exp(-cum)
    a_t = -kk * jnp.exp(cum - logw)
    r_t = r * e_inc
    b_t = kk * a * e_dec
    k_t = k * e_dec
    w_end = [e_inc[(b + 1) * C - 1:(b + 1) * C, :] for b in range(nb)]
    w_end_rows = per_seq_rows(w_end)
    b_h = b_t * w_end_rows
    k_h = k_t * w_end_rows

    lane = lax.broadcasted_iota(jnp.int32, (C, PAIR), 1)
    first = lane < HEAD_DIM
    ri = lax.broadcasted_iota(jnp.int32, (2 * C, 2 * C), 0)
    ci = lax.broadcasted_iota(jnp.int32, (2 * C, 2 * C), 1)
    strict = ci < ri
    incl = ci <= ri

    def stack2(x):
        return jnp.concatenate([jnp.where(first, x, 0.0), jnp.where(first, 0.0, x)], axis=0)

    def mm3(x, y):
        xh, xl = _split(x)
        yh, yl = _split(y)
        w = jnp.concatenate([jnp.concatenate([yh, yl], axis=1),
                             jnp.concatenate([yh, jnp.zeros_like(yh)], axis=1)], axis=0)
        r = _dot(jnp.concatenate([xh, xl], axis=1), w)
        n = y.shape[1]
        return r[:, :n] + r[:, n:]

    def mm(x, y, passes):
        if passes == 1:
            return _dot(x.astype(BF16), y.astype(BF16))
        return mm3(x, y)

    units = [(b, p) for b in range(nb) for p in range(N_PAIRS)]
    U = range(len(units))
    cut = [(seq_rows[b], slice(p * PAIR, (p + 1) * PAIR)) for b, p in units]
    lhs = [jnp.concatenate([stack2(a_t[cut[u]]), stack2(r_t[cut[u]])], axis=0).astype(BF16) for u in U]
    rhs = [jnp.concatenate([stack2(b_t[cut[u]]), stack2(k_t[cut[u]])], axis=0).astype(BF16) for u in U]
    pm = [_dot_nt(lhs[u], rhs[u]) for u in U]
    m = [jnp.where(strict, pm[u][0:2 * C, 0:2 * C], 0.0) for u in U]
    akrk = [jnp.concatenate([jnp.where(strict, pm[u][0:2 * C, 2 * C:4 * C], 0.0),
                             jnp.where(incl, pm[u][2 * C:4 * C, 2 * C:4 * C], 0.0)], axis=0).astype(BF16) for u in U]
    rb = [jnp.where(incl, pm[u][2 * C:4 * C, 0:2 * C], 0.0).astype(BF16) for u in U]
    xr = m
    for sq_passes, up_passes in PASSES_INVERSE:
        m = [mm(m[u], m[u], sq_passes) for u in U]
        xr = [xr[u] + m[u] + mm(xr[u], m[u], up_passes) for u in U]
    s2 = [s_scr[b, p] for b, p in units]
    as_rs = [_dot_nt(lhs[u], s2[u].astype(BF16)) for u in U]
    v2 = [stack2(v[cut[u]]).astype(BF16) for u in U]
    pv = [_dot(akrk[u], v2[u]) for u in U]
    rhs0 = [as_rs[u][0:2 * C] + pv[u][0:2 * C] for u in U]
    sab = [(rhs0[u] + mm(xr[u], rhs0[u], 1)).astype(BF16) for u in U]
    y2 = [as_rs[u][2 * C:] + pv[u][2 * C:] + _dot(rb[u], sab[u]) for u in U]
    y_units = [y2[u][0:C] + y2[u][C:2 * C] for u in U]
    bhkh = [jnp.concatenate([stack2(b_h[cut[u]]), stack2(k_h[cut[u]])], axis=0).astype(BF16) for u in U]
    for u, (b, p) in enumerate(units):
        upd = _dot_tn(jnp.concatenate([sab[u], v2[u]], axis=0), bhkh[u])
        s_scr[b, p] = s2[u] * w_end[b][:, cut[u][1]] + upd

    y = jnp.concatenate([jnp.concatenate(y_units[b * N_PAIRS:(b + 1) * N_PAIRS], axis=1) for b in range(nb)], axis=0)
    mean = _head_sum(y, e_ref) * (1.0 / HEAD_DIM)
    yc = y - mean
    var = _head_sum(yc * yc, e_ref) * (1.0 / HEAD_DIM)
    yn = yc * lax.rsqrt(var + GN_EPS) * vec(8) + vec(9)
    y_ref[...] = ((yn + bonus * v) * g).astype(y_ref.dtype).reshape(nb, C, D_MODEL)

    @pl.when(c_idx == pl.num_programs(1) - 1)
    def _fin():
        sout_ref[...] = s_scr[...]


def _rwkv(cols3, prev, s0, vec, mulo, lw, e256, t_valid):
    b, tp, _ = cols3.shape
    C = CHUNK
    nb = _pick(b, (RWKV_SEQS_PER_STEP, 1))
    col = lambda j: pl.BlockSpec((nb, C, D_MODEL), lambda bi, ci: (bi, ci, j))
    rowspec = lambda w: pl.BlockSpec((nb, 1, w), lambda bi, ci: (bi, 0, 0))
    full = lambda a: pl.BlockSpec(a.shape, lambda bi, ci: (0,) * a.ndim)
    sspec = pl.BlockSpec((nb, N_PAIRS, PAIR, PAIR), lambda bi, ci: (bi, 0, 0, 0))
    return pl.pallas_call(
        functools.partial(_rwkv_kernel, t_valid=min(t_valid, C), nb=nb),
        grid=(b // nb, tp // C),
        in_specs=[col(COL_R), col(COL_K), col(COL_V),
                  pl.BlockSpec((nb, C, LORA_COLS), lambda bi, ci: (bi, ci, COL_LORA_BLOCK)),
                  rowspec(D_MODEL), rowspec(D_MODEL), rowspec(D_MODEL), rowspec(LORA_COLS),
                  sspec, full(vec), full(mulo), full(lw), full(e256)],
        out_specs=[pl.BlockSpec((nb, C, D_MODEL), lambda bi, ci: (bi, ci, 0)), sspec],
        out_shape=[jax.ShapeDtypeStruct((b, tp, D_MODEL), BF16),
                   jax.ShapeDtypeStruct(s0.shape, F32)],
        scratch_shapes=[pltpu.VMEM((nb, N_PAIRS, PAIR, PAIR), F32),
                        pltpu.VMEM((nb, 1, D_MODEL), F32), pltpu.VMEM((nb, 1, D_MODEL), F32),
                        pltpu.VMEM((nb, 1, D_MODEL), F32), pltpu.VMEM((nb, 1, LORA_COLS), F32)],
        compiler_params=_cparams(("arbitrary", "arbitrary")),
        name="rwkv7",
    )(cols3, cols3, cols3, cols3, *prev, s0, vec, mulo, lw, e256)


def _gelu(x):
    return 0.5 * x * (1.0 + jnp.tanh(math.sqrt(2.0 / math.pi) * (x + 0.044715 * (x * x * x))))


def _sgu_kernel(u_ref, v_ref, ln_ref, ws_ref, bs_ref, y_ref, *maybe_v_out, lc):
    u = _gelu(u_ref[...].astype(F32))
    v = _gelu(v_ref[...].astype(F32))
    mu = jnp.mean(v, axis=-1, keepdims=True)
    vc = v - mu
    var = jnp.mean(vc * vc, axis=-1, keepdims=True)
    v = vc * lax.rsqrt(var + LN_EPS) * ln_ref[0:1, :] + ln_ref[1:2, :]
    if maybe_v_out:
        maybe_v_out[0][...] = v
    vb = v.astype(BF16)
    n_chunks = v.shape[0] // lc
    ri = lax.broadcasted_iota(jnp.int32, (lc, lc), 0)
    ci = lax.broadcasted_iota(jnp.int32, (lc, lc), 1)
    bias = bs_ref[...]
    outs = []
    for gi in range(GMLP_GROUPS):
        gs = slice(gi * GMLP_CHUNK, (gi + 1) * GMLP_CHUNK)
        w = jnp.where(ci <= ri, ws_ref[gi], 0.0).astype(BF16)
        rhs = jnp.concatenate([vb[n * lc:(n + 1) * lc, gs] for n in range(n_chunks)], axis=1)
        sv = _dot(w, rhs)
        sv = jnp.concatenate([sv[:, n * GMLP_CHUNK:(n + 1) * GMLP_CHUNK] for n in range(n_chunks)], axis=0)
        outs.append(sv + bias[:, gs])
    sv = jnp.concatenate(outs, axis=1)
    y_ref[...] = (u * sv).astype(y_ref.dtype)


def _sgu(cols3, ln, ws, bs_full, lc, n_chunks, want_v):
    b, t, _ = cols3.shape
    tt = lc * n_chunks
    col = lambda j: pl.BlockSpec((None, tt, D_MODEL), lambda bi, ti: (bi, ti, j))
    full = lambda a: pl.BlockSpec(a.shape, lambda bi, ti: (0,) * a.ndim)
    ospec = pl.BlockSpec((None, tt, D_MODEL), lambda bi, ti: (bi, ti, 0))
    out_specs = [ospec]
    out_shape = [jax.ShapeDtypeStruct((b, t, D_MODEL), BF16)]
    if want_v:
        out_specs.append(ospec)
        out_shape.append(jax.ShapeDtypeStruct((b, t, D_MODEL), F32))
    return pl.pallas_call(
        functools.partial(_sgu_kernel, lc=lc),
        grid=(b, t // tt),
        in_specs=[col(COL_U), col(COL_GV), full(ln), full(ws), full(bs_full)],
        out_specs=out_specs,
        out_shape=out_shape,
        compiler_params=_cparams(("arbitrary", "arbitrary")),
        name="sgu",
    )(cols3, cols3, ln, ws, bs_full)


def _xattn_kernel(q_ref, k_ref, v_ref, o_ref):
    outs = []
    for h in range(X_HEADS):
        hs = slice(h * X_HEAD_DIM, (h + 1) * X_HEAD_DIM)
        s = _dot_nt(q_ref[:, hs], k_ref[:, hs]) * (X_HEAD_DIM ** -0.5)
        s = s - jnp.max(s, axis=-1, keepdims=True)
        e = jnp.exp(s)
        p = e / jnp.sum(e, axis=-1, keepdims=True)
        outs.append(_dot(p.astype(BF16), v_ref[:, hs]))
    o_ref[...] = jnp.concatenate(outs, axis=1).astype(o_ref.dtype)


def _xattn(cols3, mem_k, mem_v, tq):
    b, t, _ = cols3.shape
    mspec = pl.BlockSpec((None, N_MEM, D_MODEL), lambda bi, ti: (bi, 0, 0))
    return pl.pallas_call(
        _xattn_kernel,
        grid=(b, t // tq),
        in_specs=[pl.BlockSpec((None, tq, D_MODEL), lambda bi, ti: (bi, ti, COL_Q)), mspec, mspec],
        out_specs=pl.BlockSpec((None, tq, D_MODEL), lambda bi, ti: (bi, ti, 0)),
        out_shape=jax.ShapeDtypeStruct((b, t, D_MODEL), BF16),
        compiler_params=_cparams(("arbitrary", "arbitrary")),
        name="xattn",
    )(cols3, mem_k, mem_v)


def _merge_kernel(x_ref, gate_ref, ya_ref, yb_ref, yc_ref, wb_ref, wo_ref, o_ref):
    gate = _sigmoid(gate_ref[...].astype(F32))
    merged = None
    for bi, y_ref in enumerate((ya_ref, yb_ref, yc_ref)):
        term = gate[:, bi * D_MODEL:(bi + 1) * D_MODEL] * _dot(y_ref[...], wb_ref[bi])
        merged = term if merged is None else merged + term
    o_ref[...] = x_ref[...] + _dot(merged.astype(BF16), wo_ref[...])


def _merge(x, cols, ya, yb, yc, wb, wo, tm):
    t, d = x.shape
    tok = lambda w: pl.BlockSpec((tm, w), lambda i: (i, 0))
    return pl.pallas_call(
        _merge_kernel,
        grid=(t // tm,),
        in_specs=[tok(d), pl.BlockSpec((tm, 3 * d), lambda i: (i, COL_GATE_BLOCK3)), tok(d), tok(d), tok(d),
                  pl.BlockSpec(wb.shape, lambda i: (0, 0, 0)), pl.BlockSpec(wo.shape, lambda i: (0, 0))],
        out_specs=tok(d),
        out_shape=jax.ShapeDtypeStruct((t, d), F32),
        compiler_params=_cparams(("arbitrary",)),
        name="merge",
    )(x, cols, ya, yb, yc, wb, wo)


def _ffn_kernel(x_ref, g_ref, wu_ref, wd_ref, o_ref, h_scr):
    f = pl.program_id(1)

    @pl.when(f == 0)
    def _init():
        x = x_ref[...]
        h_scr[...] = _rmsnorm(x, g_ref[...]).astype(BF16)
        o_ref[...] = x

    up = _dot(h_scr[...], wu_ref[...])
    act = jnp.square(jnp.maximum(up, 0.0)).astype(BF16)
    o_ref[...] += _dot(act, wd_ref[...])


def _ffn(x, g, wu, wd, tm, tf):
    t, d = x.shape
    return pl.pallas_call(
        _ffn_kernel,
        grid=(t // tm, D_FF // tf),
        in_specs=[pl.BlockSpec((tm, d), lambda i, f: (i, 0)),
                  pl.BlockSpec((1, d), lambda i, f: (0, 0)),
                  pl.BlockSpec((d, tf), lambda i, f: (0, f)),
                  pl.BlockSpec((tf, d), lambda i, f: (f, 0))],
        out_specs=pl.BlockSpec((tm, d), lambda i, f: (i, 0)),
        out_shape=jax.ShapeDtypeStruct((t, d), F32),
        scratch_shapes=[pltpu.VMEM((tm, d), BF16)],
        compiler_params=_cparams(("arbitrary", "arbitrary")),
        name="ffn",
    )(x, g, wu, wd)


def _norm_kernel(x_ref, g_ref, o_ref):
    o_ref[...] = _rmsnorm(x_ref[...], g_ref[...])


def _final_norm(x, g, tm):
    t, d = x.shape
    return pl.pallas_call(
        _norm_kernel,
        grid=(t // tm,),
        in_specs=[pl.BlockSpec((tm, d), lambda i: (i, 0)), pl.BlockSpec((1, d), lambda i: (0, 0))],
        out_specs=pl.BlockSpec((tm, d), lambda i: (i, 0)),
        out_shape=jax.ShapeDtypeStruct((t, d), F32),
        compiler_params=_cparams(("arbitrary",)),
        name="final_norm",
    )(x, g)


def _permute_cols(a):
    return jnp.concatenate([a[..., :3 * D_MODEL], a[..., RWKV_COLS:], a[..., 3 * D_MODEL:RWKV_COLS]], axis=-1)


def _unpermute_rwkv_row(row):
    return jnp.concatenate([row[..., :3 * D_MODEL], row[..., 9 * D_MODEL:]], axis=-1)


def _state_to_pairs(s):
    b = s.shape[0]
    s = s.reshape(b, N_PAIRS, 2, HEAD_DIM, HEAD_DIM)
    z = jnp.zeros_like(s[:, :, 0])
    top = jnp.concatenate([s[:, :, 0], z], axis=-1)
    bot = jnp.concatenate([z, s[:, :, 1]], axis=-1)
    return jnp.concatenate([top, bot], axis=-2)


def _pairs_to_state(s2):
    b = s2.shape[0]
    s0 = s2[:, :, :HEAD_DIM, :HEAD_DIM]
    s1 = s2[:, :, HEAD_DIM:, HEAD_DIM:]
    return jnp.stack([s0, s1], axis=2).reshape(b, N_HEADS, HEAD_DIM, HEAD_DIM)


def _pick(t, candidates):
    for c in candidates:
        if t % c == 0:
            return c
    return t


def _group_layer(x, b, t, mem_k, mem_v, prev_row, s0_pairs, lp):
    n_tok = b * t
    cols = _proj(x, lp["norm_mix_g"], lp["w_in"], BF16, _pick(n_tok, (512, 256)), IN_COLS // 2)
    cols3 = cols.reshape(b, t, IN_COLS)
    new_row = _unpermute_rwkv_row(cols3[:, t - 1:t, :]).astype(F32)

    tp = -(-t // CHUNK) * CHUNK
    cols3p = cols3 if tp == t else jnp.pad(cols3, ((0, 0), (0, tp - t), (0, 0)))
    prev = (prev_row[..., :D_MODEL], prev_row[..., D_MODEL:2 * D_MODEL], prev_row[..., 2 * D_MODEL:3 * D_MODEL],
            prev_row[..., 3 * D_MODEL:])
    y_a, s_new = _rwkv(cols3p, prev, s0_pairs, lp["rwkv_vec"], lp["rwkv_mu_lo"], lp["rwkv_lw"], lp["e256"], t)
    y_a = y_a[:, :t].reshape(n_tok, D_MODEL)

    lc = min(t, GMLP_CHUNK)
    n_chunks = _pick(t // lc, (4, 2, 1))
    ws = lp["sgu_w_s"][:, :lc, :lc]
    bs_full = jnp.tile(jnp.repeat(lp["sgu_b_s"][:, :lc].T, GMLP_CHUNK, axis=1), (n_chunks, 1))
    sgu_out = _sgu(cols3, lp["sgu_ln"], ws, bs_full, lc, n_chunks, want_v=t < GMLP_CHUNK)
    y_b = sgu_out[0].reshape(n_tok, D_MODEL)
    v_rows = sgu_out[1] if t < GMLP_CHUNK else None

    y_c = _xattn(cols3, mem_k, mem_v, _pick(t, (512, 256, 128))).reshape(n_tok, D_MODEL)

    x = _merge(x, cols, y_a, y_b, y_c, lp["w_branch"], lp["w_out"], _pick(n_tok, (512, 256)))
    x = _ffn(x, lp["norm_ffn_g"], lp["w_ffn_up"], lp["w_ffn_down"], _pick(n_tok, (1024, 512, 256)), 1024)
    return x, new_row, s_new, v_rows


def kernel(x_prompt, x_sample, cache_mem_k, cache_mem_v, state_wkv, state_shift, mem_prompt, norm_mix_g, norm_mem_g, norm_ffn_g, norm_final_g, w_in, w_mem_kv, rwkv_mu, rwkv_w0, rwkv_w2, rwkv_a0, rwkv_a2, rwkv_g2, rwkv_k_k, rwkv_k_a, rwkv_r_k, rwkv_lnx_g, rwkv_lnx_b, sgu_ln_g, sgu_ln_b, sgu_w_s, sgu_b_s, w_branch, w_out, w_ffn_up, w_ffn_down):
    depth = w_in.shape[0]
    bp, tp, _ = x_prompt.shape
    bs, ts, _ = x_sample.shape

    w_in_b = _permute_cols(w_in).astype(BF16)
    w_kv_b = w_mem_kv.astype(BF16)
    w_branch_b = w_branch.astype(BF16)
    w_out_b = w_out.astype(BF16)
    w_up_b = w_ffn_up.astype(BF16)
    w_down_b = w_ffn_down.astype(BF16)
    zeros = lambda n: jnp.zeros((depth, n, D_MODEL), F32)
    lw = jnp.stack([jnp.concatenate([rwkv_w2, zeros(192)], axis=1),
                    jnp.concatenate([zeros(64), rwkv_a2, zeros(128)], axis=1),
                    jnp.concatenate([zeros(128), rwkv_g2], axis=1)], axis=1).astype(BF16)
    rwkv_vec = jnp.stack([rwkv_mu[:, :D_MODEL], rwkv_mu[:, D_MODEL:2 * D_MODEL], rwkv_mu[:, 2 * D_MODEL:3 * D_MODEL],
                          rwkv_w0, rwkv_a0, rwkv_k_k, rwkv_k_a, rwkv_r_k.reshape(depth, D_MODEL),
                          rwkv_lnx_g, rwkv_lnx_b] + [jnp.zeros((depth, D_MODEL), F32)] * 6, axis=1)
    sgu_ln = jnp.stack([sgu_ln_g, sgu_ln_b] + [jnp.zeros((depth, D_MODEL), F32)] * 6, axis=1)
    hq = 4 * HEAD_DIM
    e256 = (jnp.arange(hq)[:, None] // HEAD_DIM == jnp.arange(hq)[None, :] // HEAD_DIM).astype(BF16)

    xp = x_prompt.reshape(bp * tp, D_MODEL)
    xs = x_sample.reshape(bs * ts, D_MODEL)
    mem_flat = mem_prompt.reshape(bp * N_MEM, D_MODEL)
    prompt_row0 = jnp.zeros((bp, 1, RWKV_COLS), F32)
    prompt_s0 = jnp.zeros((bp, N_PAIRS, PAIR, PAIR), F32)

    mk_p, mv_p, wkv_p, row_p, wkv_s, row_s, v_s = [], [], [], [], [], [], []
    for l in range(depth):
        lp = {
            "norm_mix_g": norm_mix_g[l][None], "w_in": w_in_b[l],
            "rwkv_vec": rwkv_vec[l], "rwkv_mu_lo": rwkv_mu[l][None, 3 * D_MODEL:], "rwkv_lw": lw[l], "e256": e256,
            "sgu_ln": sgu_ln[l], "sgu_w_s": sgu_w_s[l], "sgu_b_s": sgu_b_s[l],
            "w_branch": w_branch_b[l], "w_out": w_out_b[l],
            "norm_ffn_g": norm_ffn_g[l][None], "w_ffn_up": w_up_b[l], "w_ffn_down": w_down_b[l],
        }
        kv = _proj(mem_flat, norm_mem_g[l][None], w_kv_b[l], F32, _pick(bp * N_MEM, (512, 256)), 2 * D_MODEL)
        mem_k = kv[:, :D_MODEL].reshape(bp, N_MEM, D_MODEL)
        mem_v = kv[:, D_MODEL:].reshape(bp, N_MEM, D_MODEL)
        xp, r_p, s_p, _ = _group_layer(xp, bp, tp, mem_k.astype(BF16), mem_v.astype(BF16), prompt_row0, prompt_s0, lp)
        mk_p.append(mem_k.reshape(bp, N_MEM, X_HEADS, X_HEAD_DIM))
        mv_p.append(mem_v.reshape(bp, N_MEM, X_HEADS, X_HEAD_DIM))
        wkv_p.append(_pairs_to_state(s_p))
        row_p.append(r_p)

        ck = cache_mem_k[l].reshape(bs, N_MEM, D_MODEL).astype(BF16)
        cv = cache_mem_v[l].reshape(bs, N_MEM, D_MODEL).astype(BF16)
        xs, r_s, s_s, vr = _group_layer(xs, bs, ts, ck, cv, _permute_rwkv_prev(state_shift[l]),
                                        _state_to_pairs(state_wkv[l]), lp)
        wkv_s.append(_pairs_to_state(s_s))
        row_s.append(r_s)
        v_s.append(vr)

    g_fin = norm_final_g[None]
    y_prompt = _final_norm(xp, g_fin, _pick(bp * tp, (1024, 512, 256))).reshape(bp, tp, D_MODEL)
    y_sample = _final_norm(xs, g_fin, _pick(bs * ts, (1024, 512, 256))).reshape(bs, ts, D_MODEL)
    return (y_prompt, y_sample, jnp.stack(mk_p), jnp.stack(mv_p), jnp.stack(wkv_p), jnp.stack(row_p),
            jnp.stack(wkv_s), jnp.stack(row_s), jnp.stack(v_s))


def _permute_rwkv_prev(row):
    return row
```

```python
import functools
import math

import jax
import jax.numpy as jnp
from jax import lax
from jax.experimental import pallas as pl
from jax.experimental.pallas import tpu as pltpu

F32 = jnp.float32
BF16 = jnp.bfloat16

D_MODEL = 1024
HEAD_DIM = 64
N_HEADS = D_MODEL // HEAD_DIM
PAIR = 2 * HEAD_DIM
N_PAIRS = D_MODEL // PAIR
LORA_COLS = 256
RWKV_COLS = 3 * D_MODEL + LORA_COLS
GMLP_CHUNK = 128
GMLP_GROUPS = 8
N_MEM = 256
X_HEADS = 4
X_HEAD_DIM = D_MODEL // X_HEADS
D_FF = 4 * D_MODEL
IN_COLS = 9 * D_MODEL + LORA_COLS
RMS_EPS = 1e-6
LN_EPS = 1e-5
GN_EPS = 64e-5
CHUNK = 64
DECAY_SCALE = math.exp(-0.5)
VMEM_LIMIT = 56 * 1024 * 1024
PASSES_INVERSE = ((3, 3), (3, 3), (3, 3), (1, 1), (1, 1))
RWKV_SEQS_PER_GROUP = 2
RWKV_GROUPS_PER_STEP = 2
RWKV_GROUP_LAG = 5

COL_R, COL_K, COL_V, COL_U, COL_GV, COL_Q = 0, 1, 2, 3, 4, 5
COL_GATE_BLOCK3 = 2
COL_LORA_BLOCK = 9 * D_MODEL // LORA_COLS


def _cparams(sem):
    return pltpu.CompilerParams(dimension_semantics=sem, vmem_limit_bytes=VMEM_LIMIT)


def _dot(a, b):
    return jnp.dot(a, b, preferred_element_type=F32)


def _dot_nt(a, b):
    return lax.dot_general(a, b, (((1,), (1,)), ((), ())), preferred_element_type=F32)


def _dot_tn(a, b):
    return lax.dot_general(a, b, (((0,), (0,)), ((), ())), preferred_element_type=F32)


def _split(x):
    hi = x.astype(BF16)
    lo = (x - hi.astype(F32)).astype(BF16)
    return hi, lo


def _rmsnorm(x, g):
    return x * lax.rsqrt(jnp.mean(x * x, axis=-1, keepdims=True) + RMS_EPS) * g


def _sigmoid(x):
    return 1.0 / (1.0 + jnp.exp(-x))


def _proj_kernel(x_ref, g_ref, w_ref, o_ref):
    h = _rmsnorm(x_ref[...], g_ref[...]).astype(BF16)
    o_ref[...] = _dot(h, w_ref[...]).astype(o_ref.dtype)


def _proj(x, g, w, out_dtype, tm, tn):
    t, d = x.shape
    nc = w.shape[1]
    return pl.pallas_call(
        _proj_kernel,
        grid=(nc // tn, t // tm),
        in_specs=[pl.BlockSpec((tm, d), lambda j, i: (i, 0)),
                  pl.BlockSpec((1, d), lambda j, i: (0, 0)),
                  pl.BlockSpec((d, tn), lambda j, i: (0, j))],
        out_specs=pl.BlockSpec((tm, tn), lambda j, i: (i, j)),
        out_shape=jax.ShapeDtypeStruct((t, nc), out_dtype),
        compiler_params=_cparams(("arbitrary", "arbitrary")),
        name="proj",
    )(x, g, w)


def _head_sum(x, e_ref):
    c = x.shape[0]
    q = 4 * HEAD_DIM
    xs = jnp.concatenate([x[:, i * q:(i + 1) * q] for i in range(D_MODEL // q)], axis=0)
    r = _dot(xs.astype(BF16), e_ref[...])
    return jnp.concatenate([r[i * c:(i + 1) * c] for i in range(D_MODEL // q)], axis=1)


def _rwkv_kernel(r_ref, k_ref, v_ref, lo_ref, pr_ref, pk_ref, pv_ref, plo_ref, s0_ref,
                 vec_ref, mulo_ref, lw_ref, e_ref,
                 y_ref, sout_ref,
                 s_scr, prr, prk, prv, prlo, *, t_valid, nb, ng):
    c_idx = pl.program_id(1)

    @pl.when(c_idx == 0)
    def _init():
        s_scr[...] = s0_ref[...]
        prr[...] = pr_ref[...]
        prk[...] = pk_ref[...]
        prv[...] = pv_ref[...]
        prlo[...] = plo_ref[...]

    groups = []
    for g in range(ng):
        sl = slice(g * nb, (g + 1) * nb)
        groups.append(_rwkv_group(r_ref.at[sl], k_ref.at[sl], v_ref.at[sl], lo_ref.at[sl], vec_ref, mulo_ref,
                                  lw_ref, e_ref, y_ref.at[sl], s_scr.at[sl], prr.at[sl], prk.at[sl], prv.at[sl],
                                  prlo.at[sl], t_valid=t_valid, nb=nb))
    live, step = list(range(ng)), 0
    while live:
        for g in list(live):
            if step >= g * RWKV_GROUP_LAG and next(groups[g], "done") == "done":
                live.remove(g)
        step += 1

    @pl.when(c_idx == pl.num_programs(1) - 1)
    def _fin():
        sout_ref[...] = s_scr[...]


def _rwkv_group(r_ref, k_ref, v_ref, lo_ref, vec_ref, mulo_ref, lw_ref, e_ref,
                y_ref, s_scr, prr, prk, prv, prlo, *, t_valid, nb):
    C = CHUNK
    R = nb * C

    def vec(i):
        return vec_ref[i:i + 1, :]

    row = lax.broadcasted_iota(jnp.int32, (R, 1), 0)
    tpos = row & (C - 1)
    seq_rows = [slice(b * C, (b + 1) * C) for b in range(nb)]

    def per_seq_rows(rows):
        out = rows[nb - 1]
        for b in reversed(range(nb - 1)):
            out = jnp.where(row < (b + 1) * C, rows[b], out)
        return out

    def mixed(ref, prev, mu):
        x = ref[...].astype(F32).reshape(R, ref.shape[-1])
        shifted = jnp.where(tpos == 0, per_seq_rows([prev[b] for b in range(nb)]), pltpu.roll(x, 1, 0))
        for b in range(nb):
            prev[b] = x[(b + 1) * C - 1:(b + 1) * C, :]
        return x + (shifted - x) * mu

    r = mixed(r_ref, prr, vec(0))
    k = mixed(k_ref, prk, vec(1))
    v = mixed(v_ref, prv, vec(2))
    lo = mixed(lo_ref, prlo, mulo_ref[...])
    yield

    lane_lo = lax.broadcasted_iota(jnp.int32, lo.shape, 1)
    act = jnp.where(lane_lo < 64, jnp.tanh(lo), jnp.where(lane_lo < 128, lo, _sigmoid(lo))).astype(BF16)
    z = vec(3) + _dot(act, lw_ref[0])
    a = _sigmoid(vec(4) + _dot(act, lw_ref[1]))
    g = _dot(act, lw_ref[2])
    logw = -DECAY_SCALE * _sigmoid(z)
    yield

    kk = k * vec(5)
    kk = kk * lax.rsqrt(jnp.maximum(_head_sum(kk * kk, e_ref), 1e-24))
    k = k * (1.0 + (a - 1.0) * vec(6))
    bonus = _head_sum(r * k * vec(7), e_ref)
    yield

    if t_valid < C:
        valid = tpos < t_valid
        logw = jnp.where(valid, logw, 0.0)
        kk = jnp.where(valid, kk, 0.0)
        k = jnp.where(valid, k, 0.0)
        v = jnp.where(valid, v, 0.0)

    ti = lax.broadcasted_iota(jnp.int32, (R, R), 0)
    tj = lax.broadcasted_iota(jnp.int32, (R, R), 1)
    tril = jnp.where((tj <= ti) & (tj >= ti - (ti & (C - 1))), 1.0, 0.0).astype(BF16)
    lw_hi, lw_lo = _split(logw)
    cum = _dot(jnp.concatenate([tril, tril], axis=1), jnp.concatenate([lw_hi, lw_lo], axis=0))
    e_inc = jnp.exp(cum)
    e_dec = jnp.exp(-cum)
    yield
    a_t = -kk * jnp.exp(cum - logw)
    r_t = r * e_inc
    b_t = kk * a * e_dec
    k_t = k * e_dec
    w_end = [e_inc[(b + 1) * C - 1:(b + 1) * C, :] for b in range(nb)]
    w_end_rows = per_seq_rows(w_end)
    b_h = b_t * w_end_rows
    k_h = k_t * w_end_rows

    lane = lax.broadcasted_iota(jnp.int32, (C, PAIR), 1)
    first = lane < HEAD_DIM
    t_row = lax.broadcasted_iota(jnp.int32, (C, PAIR), 0)
    s_lane = lane & (HEAD_DIM - 1)
    strict = s_lane < t_row
    incl = s_lane <= t_row
    bi = lax.broadcasted_iota(jnp.int32, (PAIR, PAIR), 0)
    bj = lax.broadcasted_iota(jnp.int32, (PAIR, PAIR), 1)
    same_head = (bi < HEAD_DIM) == (bj < HEAD_DIM)

    def blockdiag(x):
        z = jnp.zeros_like(x)
        return jnp.concatenate([jnp.where(first, x, z), jnp.where(first, z, x)], axis=0)

    def mm3(x, y):
        xh, xl = _split(x)
        yh, yl = _split(y)
        yh, yl = blockdiag(yh), blockdiag(yl)
        w = jnp.concatenate([jnp.concatenate([yh, yl], axis=1),
                             jnp.concatenate([yh, jnp.zeros_like(yh)], axis=1)], axis=0)
        res = _dot(jnp.concatenate([xh, xl], axis=1), w)
        return res[:, :PAIR] + res[:, PAIR:]

    def mm(x, y, passes):
        if passes == 1:
            return _dot(x.astype(BF16), blockdiag(y.astype(BF16)))
        return mm3(x, y)

    yield
    units = [(b, p) for b in range(nb) for p in range(N_PAIRS)]
    U = range(len(units))
    cut = [(seq_rows[b], slice(p * PAIR, (p + 1) * PAIR)) for b, p in units]
    lhs = [jnp.concatenate([a_t[cut[u]], r_t[cut[u]]], axis=0).astype(BF16) for u in U]
    rhs = [jnp.concatenate([blockdiag(b_t[cut[u]].astype(BF16)), blockdiag(k_t[cut[u]].astype(BF16))], axis=0)
           for u in U]
    pm = [_dot_nt(lhs[u], rhs[u]) for u in U]
    yield
    m = [jnp.where(strict, pm[u][0:C, 0:PAIR], 0.0) for u in U]
    akrk = [jnp.concatenate([jnp.where(strict, pm[u][0:C, PAIR:], 0.0),
                             jnp.where(incl, pm[u][C:, PAIR:], 0.0)], axis=0).astype(BF16) for u in U]
    rb = [jnp.where(incl, pm[u][C:, 0:PAIR], 0.0).astype(BF16) for u in U]
    yield
    xr = m
    for sq_passes, up_passes in PASSES_INVERSE:
        m = [mm(m[u], m[u], sq_passes) for u in U]
        xr = [xr[u] + m[u] + mm(xr[u], m[u], up_passes) for u in U]
        yield
    s2 = [s_scr[b, p] for b, p in units]
    as_rs = [_dot_nt(lhs[u], s2[u].astype(BF16)) for u in U]
    vb = [v[cut[u]].astype(BF16) for u in U]
    pv = [_dot(akrk[u], blockdiag(vb[u])) for u in U]
    yield
    rhs0 = [as_rs[u][0:C] + pv[u][0:C] for u in U]
    sab = [(rhs0[u] + mm(xr[u], rhs0[u], 1)).astype(BF16) for u in U]
    yield
    y_units = [as_rs[u][C:] + pv[u][C:] + _dot(rb[u], blockdiag(sab[u])) for u in U]
    bhkh = [jnp.concatenate([b_h[cut[u]], k_h[cut[u]]], axis=0).astype(BF16) for u in U]
    for u, (b, p) in enumerate(units):
        upd = _dot_tn(jnp.concatenate([sab[u], vb[u]], axis=0), bhkh[u])
        s_scr[b, p] = s2[u] * w_end[b][:, cut[u][1]] + jnp.where(same_head, upd, 0.0)

    yield
    y = jnp.concatenate([jnp.concatenate(y_units[b * N_PAIRS:(b + 1) * N_PAIRS], axis=1) for b in range(nb)], axis=0)
    mean = _head_sum(y, e_ref) * (1.0 / HEAD_DIM)
    yield
    yc = y - mean
    var = _head_sum(yc * yc, e_ref) * (1.0 / HEAD_DIM)
    yield
    yn = yc * lax.rsqrt(var + GN_EPS) * vec(8) + vec(9)
    y_ref[...] = ((yn + bonus * v) * g).astype(y_ref.dtype).reshape(nb, C, D_MODEL)


def _rwkv(cols3, prev, s0, vec, mulo, lw, e256, t_valid):
    b, tp, _ = cols3.shape
    C = CHUNK
    nb = _pick(b, (RWKV_SEQS_PER_GROUP, 1))
    ng = _pick(b // nb, (RWKV_GROUPS_PER_STEP, 1))
    nbt = nb * ng
    col = lambda j: pl.BlockSpec((nbt, C, D_MODEL), lambda bi, ci: (bi, ci, j))
    rowspec = lambda w: pl.BlockSpec((nbt, 1, w), lambda bi, ci: (bi, 0, 0))
    full = lambda a: pl.BlockSpec(a.shape, lambda bi, ci: (0,) * a.ndim)
    sspec = pl.BlockSpec((nbt, N_PAIRS, PAIR, PAIR), lambda bi, ci: (bi, 0, 0, 0))
    return pl.pallas_call(
        functools.partial(_rwkv_kernel, t_valid=min(t_valid, C), nb=nb, ng=ng),
        grid=(b // nbt, tp // C),
        in_specs=[col(COL_R), col(COL_K), col(COL_V),
                  pl.BlockSpec((nbt, C, LORA_COLS), lambda bi, ci: (bi, ci, COL_LORA_BLOCK)),
                  rowspec(D_MODEL), rowspec(D_MODEL), rowspec(D_MODEL), rowspec(LORA_COLS),
                  sspec, full(vec), full(mulo), full(lw), full(e256)],
        out_specs=[pl.BlockSpec((nbt, C, D_MODEL), lambda bi, ci: (bi, ci, 0)), sspec],
        out_shape=[jax.ShapeDtypeStruct((b, tp, D_MODEL), BF16),
                   jax.ShapeDtypeStruct(s0.shape, F32)],
        scratch_shapes=[pltpu.VMEM((nbt, N_PAIRS, PAIR, PAIR), F32),
                        pltpu.VMEM((nbt, 1, D_MODEL), F32), pltpu.VMEM((nbt, 1, D_MODEL), F32),
                        pltpu.VMEM((nbt, 1, D_MODEL), F32), pltpu.VMEM((nbt, 1, LORA_COLS), F32)],
        compiler_params=_cparams(("arbitrary", "arbitrary")),
        name="rwkv7",
    )(cols3, cols3, cols3, cols3, *prev, s0, vec, mulo, lw, e256)


def _gelu(x):
    return 0.5 * x * (1.0 + jnp.tanh(math.sqrt(2.0 / math.pi) * (x + 0.044715 * (x * x * x))))


def _sgu_kernel(u_ref, v_ref, ln_ref, ws_ref, bs_ref, y_ref, *maybe_v_out, lc):
    u = _gelu(u_ref[...].astype(F32))
    v = _gelu(v_ref[...].astype(F32))
    mu = jnp.mean(v, axis=-1, keepdims=True)
    vc = v - mu
    var = jnp.mean(vc * vc, axis=-1, keepdims=True)
    v = vc * lax.rsqrt(var + LN_EPS) * ln_ref[0:1, :] + ln_ref[1:2, :]
    if maybe_v_out:
        maybe_v_out[0][...] = v
    vb = v.astype(BF16)
    n_chunks = v.shape[0] // lc
    ri = lax.broadcasted_iota(jnp.int32, (lc, lc), 0)
    ci = lax.broadcasted_iota(jnp.int32, (lc, lc), 1)
    bias = bs_ref[...]
    outs = []
    for gi in range(GMLP_GROUPS):
        gs = slice(gi * GMLP_CHUNK, (gi + 1) * GMLP_CHUNK)
        w = jnp.where(ci <= ri, ws_ref[gi], 0.0).astype(BF16)
        rhs = jnp.concatenate([vb[n * lc:(n + 1) * lc, gs] for n in range(n_chunks)], axis=1)
        sv = _dot(w, rhs)
        sv = jnp.concatenate([sv[:, n * GMLP_CHUNK:(n + 1) * GMLP_CHUNK] for n in range(n_chunks)], axis=0)
        outs.append(sv + bias[:, gs])
    sv = jnp.concatenate(outs, axis=1)
    y_ref[...] = (u * sv).astype(y_ref.dtype)


def _sgu(cols3, ln, ws, bs_full, lc, n_chunks, want_v):
    b, t, _ = cols3.shape
    tt = lc * n_chunks
    col = lambda j: pl.BlockSpec((None, tt, D_MODEL), lambda bi, ti: (bi, ti, j))
    full = lambda a: pl.BlockSpec(a.shape, lambda bi, ti: (0,) * a.ndim)
    ospec = pl.BlockSpec((None, tt, D_MODEL), lambda bi, ti: (bi, ti, 0))
    out_specs = [ospec]
    out_shape = [jax.ShapeDtypeStruct((b, t, D_MODEL), BF16)]
    if want_v:
        out_specs.append(ospec)
        out_shape.append(jax.ShapeDtypeStruct((b, t, D_MODEL), F32))
    return pl.pallas_call(
        functools.partial(_sgu_kernel, lc=lc),
        grid=(b, t // tt),
        in_specs=[col(COL_U), col(COL_GV), full(ln), full(ws), full(bs_full)],
        out_specs=out_specs,
        out_shape=out_shape,
        compiler_params=_cparams(("arbitrary", "arbitrary")),
        name="sgu",
    )(cols3, cols3, ln, ws, bs_full)


def _xattn_kernel(q_ref, k_ref, v_ref, o_ref):
    outs = []
    for h in range(X_HEADS):
        hs = slice(h * X_HEAD_DIM, (h + 1) * X_HEAD_DIM)
        s = _dot_nt(q_ref[:, hs], k_ref[:, hs]) * (X_HEAD_DIM ** -0.5)
        s = s - jnp.max(s, axis=-1, keepdims=True)
        e = jnp.exp(s)
        p = e / jnp.sum(e, axis=-1, keepdims=True)
        outs.append(_dot(p.astype(BF16), v_ref[:, hs]))
    o_ref[...] = jnp.concatenate(outs, axis=1).astype(o_ref.dtype)


def _xattn(cols3, mem_k, mem_v, tq):
    b, t, _ = cols3.shape
    mspec = pl.BlockSpec((None, N_MEM, D_MODEL), lambda bi, ti: (bi, 0, 0))
    return pl.pallas_call(
        _xattn_kernel,
        grid=(b, t // tq),
        in_specs=[pl.BlockSpec((None, tq, D_MODEL), lambda bi, ti: (bi, ti, COL_Q)), mspec, mspec],
        out_specs=pl.BlockSpec((None, tq, D_MODEL), lambda bi, ti: (bi, ti, 0)),
        out_shape=jax.ShapeDtypeStruct((b, t, D_MODEL), BF16),
        compiler_params=_cparams(("arbitrary", "arbitrary")),
        name="xattn",
    )(cols3, mem_k, mem_v)


def _merge_kernel(x_ref, gate_ref, ya_ref, yb_ref, yc_ref, wb_ref, wo_ref, o_ref):
    gate = _sigmoid(gate_ref[...].astype(F32))
    merged = None
    for bi, y_ref in enumerate((ya_ref, yb_ref, yc_ref)):
        term = gate[:, bi * D_MODEL:(bi + 1) * D_MODEL] * _dot(y_ref[...], wb_ref[bi])
        merged = term if merged is None else merged + term
    o_ref[...] = x_ref[...] + _dot(merged.astype(BF16), wo_ref[...])


def _merge(x, cols, ya, yb, yc, wb, wo, tm):
    t, d = x.shape
    tok = lambda w: pl.BlockSpec((tm, w), lambda i: (i, 0))
    return pl.pallas_call(
        _merge_kernel,
        grid=(t // tm,),
        in_specs=[tok(d), pl.BlockSpec((tm, 3 * d), lambda i: (i, COL_GATE_BLOCK3)), tok(d), tok(d), tok(d),
                  pl.BlockSpec(wb.shape, lambda i: (0, 0, 0)), pl.BlockSpec(wo.shape, lambda i: (0, 0))],
        out_specs=tok(d),
        out_shape=jax.ShapeDtypeStruct((t, d), F32),
        compiler_params=_cparams(("arbitrary",)),
        name="merge",
    )(x, cols, ya, yb, yc, wb, wo)


def _ffn_kernel(x_ref, g_ref, wu_ref, wd_ref, gf_ref, o_ref, h_scr, *, final_norm):
    f = pl.program_id(1)

    @pl.when(f == 0)
    def _init():
        x = x_ref[...]
        h_scr[...] = _rmsnorm(x, g_ref[...]).astype(BF16)
        o_ref[...] = x

    up = _dot(h_scr[...], wu_ref[...])
    act = jnp.square(jnp.maximum(up, 0.0)).astype(BF16)
    o_ref[...] += _dot(act, wd_ref[...])

    if final_norm:
        @pl.when(f == pl.num_programs(1) - 1)
        def _final():
            o_ref[...] = _rmsnorm(o_ref[...], gf_ref[...])


def _ffn(x, g, wu, wd, g_final, final_norm, tm, tf):
    t, d = x.shape
    return pl.pallas_call(
        functools.partial(_ffn_kernel, final_norm=final_norm),
        grid=(t // tm, D_FF // tf),
        in_specs=[pl.BlockSpec((tm, d), lambda i, f: (i, 0)),
                  pl.BlockSpec((1, d), lambda i, f: (0, 0)),
                  pl.BlockSpec((d, tf), lambda i, f: (0, f)),
                  pl.BlockSpec((tf, d), lambda i, f: (f, 0)),
                  pl.BlockSpec((1, d), lambda i, f: (0, 0))],
        out_specs=pl.BlockSpec((tm, d), lambda i, f: (i, 0)),
        out_shape=jax.ShapeDtypeStruct((t, d), F32),
        scratch_shapes=[pltpu.VMEM((tm, d), BF16)],
        compiler_params=_cparams(("arbitrary", "arbitrary")),
        name="ffn",
    )(x, g, wu, wd, g_final)


def _permute_cols(a):
    return jnp.concatenate([a[..., :3 * D_MODEL], a[..., RWKV_COLS:], a[..., 3 * D_MODEL:RWKV_COLS]], axis=-1)


def _unpermute_rwkv_row(row):
    return jnp.concatenate([row[..., :3 * D_MODEL], row[..., 9 * D_MODEL:]], axis=-1)


def _state_to_pairs(s):
    b = s.shape[0]
    s = s.reshape(b, N_PAIRS, 2, HEAD_DIM, HEAD_DIM)
    z = jnp.zeros_like(s[:, :, 0])
    top = jnp.concatenate([s[:, :, 0], z], axis=-1)
    bot = jnp.concatenate([z, s[:, :, 1]], axis=-1)
    return jnp.concatenate([top, bot], axis=-2)


def _pairs_to_state(s2):
    b = s2.shape[0]
    s0 = s2[:, :, :HEAD_DIM, :HEAD_DIM]
    s1 = s2[:, :, HEAD_DIM:, HEAD_DIM:]
    return jnp.stack([s0, s1], axis=2).reshape(b, N_HEADS, HEAD_DIM, HEAD_DIM)


def _pick(t, candidates):
    for c in candidates:
        if t % c == 0:
            return c
    return t


def _group_layer(x, b, t, mem_k, mem_v, prev_row, s0_pairs, lp):
    n_tok = b * t
    cols = _proj(x, lp["norm_mix_g"], lp["w_in"], BF16, _pick(n_tok, (512, 256)), IN_COLS // 2)
    cols3 = cols.reshape(b, t, IN_COLS)
    new_row = _unpermute_rwkv_row(cols3[:, t - 1:t, :]).astype(F32)

    tp = -(-t // CHUNK) * CHUNK
    cols3p = cols3 if tp == t else jnp.pad(cols3, ((0, 0), (0, tp - t), (0, 0)))
    prev = (prev_row[..., :D_MODEL], prev_row[..., D_MODEL:2 * D_MODEL], prev_row[..., 2 * D_MODEL:3 * D_MODEL],
            prev_row[..., 3 * D_MODEL:])
    y_a, s_new = _rwkv(cols3p, prev, s0_pairs, lp["rwkv_vec"], lp["rwkv_mu_lo"], lp["rwkv_lw"], lp["e256"], t)
    y_a = y_a[:, :t].reshape(n_tok, D_MODEL)

    lc = min(t, GMLP_CHUNK)
    n_chunks = _pick(t // lc, (4, 2, 1))
    ws = lp["sgu_w_s"][:, :lc, :lc]
    bs_full = jnp.tile(jnp.repeat(lp["sgu_b_s"][:, :lc].T, GMLP_CHUNK, axis=1), (n_chunks, 1))
    sgu_out = _sgu(cols3, lp["sgu_ln"], ws, bs_full, lc, n_chunks, want_v=t < GMLP_CHUNK)
    y_b = sgu_out[0].reshape(n_tok, D_MODEL)
    v_rows = sgu_out[1] if t < GMLP_CHUNK else None

    y_c = _xattn(cols3, mem_k, mem_v, _pick(t, (512, 256, 128))).reshape(n_tok, D_MODEL)

    x = _merge(x, cols, y_a, y_b, y_c, lp["w_branch"], lp["w_out"], _pick(n_tok, (512, 256)))
    x = _ffn(x, lp["norm_ffn_g"], lp["w_ffn_up"], lp["w_ffn_down"], lp["norm_final_g"], lp["is_last"],
             _pick(n_tok, (1024, 512, 256)), 1024)
    return x, new_row, s_new, v_rows


def kernel(x_prompt, x_sample, cache_mem_k, cache_mem_v, state_wkv, state_shift, mem_prompt, norm_mix_g, norm_mem_g, norm_ffn_g, norm_final_g, w_in, w_mem_kv, rwkv_mu, rwkv_w0, rwkv_w2, rwkv_a0, rwkv_a2, rwkv_g2, rwkv_k_k, rwkv_k_a, rwkv_r_k, rwkv_lnx_g, rwkv_lnx_b, sgu_ln_g, sgu_ln_b, sgu_w_s, sgu_b_s, w_branch, w_out, w_ffn_up, w_ffn_down):
    depth = w_in.shape[0]
    bp, tp, _ = x_prompt.shape
    bs, ts, _ = x_sample.shape

    w_in_b = _permute_cols(w_in).astype(BF16)
    w_kv_b = w_mem_kv.astype(BF16)
    w_branch_b = w_branch.astype(BF16)
    w_out_b = w_out.astype(BF16)
    w_up_b = w_ffn_up.astype(BF16)
    w_down_b = w_ffn_down.astype(BF16)
    zeros = lambda n: jnp.zeros((depth, n, D_MODEL), F32)
    lw = jnp.stack([jnp.concatenate([rwkv_w2, zeros(192)], axis=1),
                    jnp.concatenate([zeros(64), rwkv_a2, zeros(128)], axis=1),
                    jnp.concatenate([zeros(128), rwkv_g2], axis=1)], axis=1).astype(BF16)
    rwkv_vec = jnp.stack([rwkv_mu[:, :D_MODEL], rwkv_mu[:, D_MODEL:2 * D_MODEL], rwkv_mu[:, 2 * D_MODEL:3 * D_MODEL],
                          rwkv_w0, rwkv_a0, rwkv_k_k, rwkv_k_a, rwkv_r_k.reshape(depth, D_MODEL),
                          rwkv_lnx_g, rwkv_lnx_b] + [jnp.zeros((depth, D_MODEL), F32)] * 6, axis=1)
    sgu_ln = jnp.stack([sgu_ln_g, sgu_ln_b] + [jnp.zeros((depth, D_MODEL), F32)] * 6, axis=1)
    hq = 4 * HEAD_DIM
    e256 = (jnp.arange(hq)[:, None] // HEAD_DIM == jnp.arange(hq)[None, :] // HEAD_DIM).astype(BF16)

    xp = x_prompt.reshape(bp * tp, D_MODEL)
    xs = x_sample.reshape(bs * ts, D_MODEL)
    mem_flat = mem_prompt.reshape(bp * N_MEM, D_MODEL)
    prompt_row0 = jnp.zeros((bp, 1, RWKV_COLS), F32)
    prompt_s0 = jnp.zeros((bp, N_PAIRS, PAIR, PAIR), F32)

    mk_p, mv_p, wkv_p, row_p, wkv_s, row_s, v_s = [], [], [], [], [], [], []
    for l in range(depth):
        lp = {
            "norm_mix_g": norm_mix_g[l][None], "w_in": w_in_b[l],
            "rwkv_vec": rwkv_vec[l], "rwkv_mu_lo": rwkv_mu[l][None, 3 * D_MODEL:], "rwkv_lw": lw[l], "e256": e256,
            "sgu_ln": sgu_ln[l], "sgu_w_s": sgu_w_s[l], "sgu_b_s": sgu_b_s[l],
            "w_branch": w_branch_b[l], "w_out": w_out_b[l],
            "norm_ffn_g": norm_ffn_g[l][None], "w_ffn_up": w_up_b[l], "w_ffn_down": w_down_b[l],
            "norm_final_g": norm_final_g[None], "is_last": l == depth - 1,
        }
        kv = _proj(mem_flat, norm_mem_g[l][None], w_kv_b[l], F32, _pick(bp * N_MEM, (512, 256)), 2 * D_MODEL)
        mem_k = kv[:, :D_MODEL].reshape(bp, N_MEM, D_MODEL)
        mem_v = kv[:, D_MODEL:].reshape(bp, N_MEM, D_MODEL)
        xp, r_p, s_p, _ = _group_layer(xp, bp, tp, mem_k.astype(BF16), mem_v.astype(BF16), prompt_row0, prompt_s0, lp)
        mk_p.append(mem_k.reshape(bp, N_MEM, X_HEADS, X_HEAD_DIM))
        mv_p.append(mem_v.reshape(bp, N_MEM, X_HEADS, X_HEAD_DIM))
        wkv_p.append(_pairs_to_state(s_p))
        row_p.append(r_p)

        ck = cache_mem_k[l].reshape(bs, N_MEM, D_MODEL).astype(BF16)
        cv = cache_mem_v[l].reshape(bs, N_MEM, D_MODEL).astype(BF16)
        xs, r_s, s_s, vr = _group_layer(xs, bs, ts, ck, cv, _permute_rwkv_prev(state_shift[l]),
                                        _state_to_pairs(state_wkv[l]), lp)
        wkv_s.append(_pairs_to_state(s_s))
        row_s.append(r_s)
        v_s.append(vr)

    y_prompt = xp.reshape(bp, tp, D_MODEL)
    y_sample = xs.reshape(bs, ts, D_MODEL)
    return (y_prompt, y_sample, jnp.stack(mk_p), jnp.stack(mv_p), jnp.stack(wkv_p), jnp.stack(row_p),
            jnp.stack(wkv_s), jnp.stack(row_s), jnp.stack(v_s))


def _permute_rwkv_prev(row):
    return row
```

```python
import functools
import math

import jax
import jax.numpy as jnp
from jax import lax
from jax.experimental import pallas as pl
from jax.experimental.pallas import tpu as pltpu

F32 = jnp.float32
BF16 = jnp.bfloat16

D_MODEL = 1024
HEAD_DIM = 64
N_HEADS = D_MODEL // HEAD_DIM
PAIR = 2 * HEAD_DIM
N_PAIRS = D_MODEL // PAIR
LORA_COLS = 256
RWKV_COLS = 3 * D_MODEL + LORA_COLS
GMLP_CHUNK = 128
GMLP_GROUPS = 8
N_MEM = 256
X_HEADS = 4
X_HEAD_DIM = D_MODEL // X_HEADS
D_FF = 4 * D_MODEL
COL_Q_START = RWKV_COLS + 2 * D_MODEL
GQ_Q_BLOCK = 3
RMS_EPS = 1e-6
LN_EPS = 1e-5
GN_EPS = 64e-5
CHUNK = 64
DECAY_SCALE = math.exp(-0.5)
VMEM_LIMIT = 56 * 1024 * 1024
PASSES_INVERSE = ((3, 3), (3, 3), (3, 3), (1, 1), (1, 1))
RWKV_SEQS_PER_GROUP = 2
RWKV_GROUPS_PER_STEP = 2
RWKV_GROUP_LAG = 5


def _cparams(sem):
    return pltpu.CompilerParams(dimension_semantics=sem, vmem_limit_bytes=VMEM_LIMIT)


def _dot(a, b):
    return jnp.dot(a, b, preferred_element_type=F32)


def _dot_nt(a, b):
    return lax.dot_general(a, b, (((1,), (1,)), ((), ())), preferred_element_type=F32)


def _dot_tn(a, b):
    return lax.dot_general(a, b, (((0,), (0,)), ((), ())), preferred_element_type=F32)


def _split(x):
    hi = x.astype(BF16)
    lo = (x - hi.astype(F32)).astype(BF16)
    return hi, lo


def _rmsnorm(x, g):
    return x * lax.rsqrt(jnp.mean(x * x, axis=-1, keepdims=True) + RMS_EPS) * g


def _sigmoid(x):
    return 1.0 / (1.0 + jnp.exp(-x))


def _gelu(x):
    return 0.5 * x * (1.0 + jnp.tanh(math.sqrt(2.0 / math.pi) * (x + 0.044715 * (x * x * x))))


def _proj_kernel(x_ref, g_ref, w_ref, o_ref, *, gelu):
    h = _rmsnorm(x_ref[...], g_ref[...]).astype(BF16)
    y = _dot(h, w_ref[...])
    o_ref[...] = (_gelu(y) if gelu else y).astype(o_ref.dtype)


def _proj(x, g, w, out_dtype, tm, tn, gelu=False):
    t, d = x.shape
    nc = w.shape[1]
    return pl.pallas_call(
        functools.partial(_proj_kernel, gelu=gelu),
        grid=(nc // tn, t // tm),
        in_specs=[pl.BlockSpec((tm, d), lambda j, i: (i, 0)),
                  pl.BlockSpec((1, d), lambda j, i: (0, 0)),
                  pl.BlockSpec((d, tn), lambda j, i: (0, j))],
        out_specs=pl.BlockSpec((tm, tn), lambda j, i: (i, j)),
        out_shape=jax.ShapeDtypeStruct((t, nc), out_dtype),
        compiler_params=_cparams(("arbitrary", "arbitrary")),
        name="proj",
    )(x, g, w)


def _head_sum(x, e_ref):
    c = x.shape[0]
    q = 4 * HEAD_DIM
    xs = jnp.concatenate([x[:, i * q:(i + 1) * q] for i in range(D_MODEL // q)], axis=0)
    r = _dot(xs.astype(BF16), e_ref[...])
    return jnp.concatenate([r[i * c:(i + 1) * c] for i in range(D_MODEL // q)], axis=1)


def _rwkv_kernel(cols_ref, pr_ref, pk_ref, pv_ref, plo_ref, s0_ref,
                 vec_ref, mulo_ref, lw_ref, e_ref,
                 y_ref, sout_ref,
                 s_scr, prr, prk, prv, prlo, *, t_valid, nb, ng):
    c_idx = pl.program_id(1)

    @pl.when(c_idx == 0)
    def _init():
        s_scr[...] = s0_ref[...]
        prr[...] = pr_ref[...]
        prk[...] = pk_ref[...]
        prv[...] = pv_ref[...]
        prlo[...] = plo_ref[...]

    groups = []
    for g in range(ng):
        sl = slice(g * nb, (g + 1) * nb)
        r_ref, k_ref, v_ref = (cols_ref.at[sl, :, i * D_MODEL:(i + 1) * D_MODEL] for i in range(3))
        lo_ref = cols_ref.at[sl, :, 3 * D_MODEL:RWKV_COLS]
        groups.append(_rwkv_group(r_ref, k_ref, v_ref, lo_ref, vec_ref, mulo_ref,
                                  lw_ref, e_ref, y_ref.at[sl], s_scr.at[sl], prr.at[sl], prk.at[sl], prv.at[sl],
                                  prlo.at[sl], t_valid=t_valid, nb=nb))
    live, step = list(range(ng)), 0
    while live:
        for g in list(live):
            if step >= g * RWKV_GROUP_LAG and next(groups[g], "done") == "done":
                live.remove(g)
        step += 1

    @pl.when(c_idx == pl.num_programs(1) - 1)
    def _fin():
        sout_ref[...] = s_scr[...]


def _rwkv_group(r_ref, k_ref, v_ref, lo_ref, vec_ref, mulo_ref, lw_ref, e_ref,
                y_ref, s_scr, prr, prk, prv, prlo, *, t_valid, nb):
    C = CHUNK
    R = nb * C

    def vec(i):
        return vec_ref[i:i + 1, :]

    row = lax.broadcasted_iota(jnp.int32, (R, 1), 0)
    tpos = row & (C - 1)
    seq_rows = [slice(b * C, (b + 1) * C) for b in range(nb)]

    def per_seq_rows(rows):
        out = rows[nb - 1]
        for b in reversed(range(nb - 1)):
            out = jnp.where(row < (b + 1) * C, rows[b], out)
        return out

    def mixed(ref, prev, mu):
        x = ref[...].astype(F32).reshape(R, ref.shape[-1])
        shifted = jnp.where(tpos == 0, per_seq_rows([prev[b] for b in range(nb)]), pltpu.roll(x, 1, 0))
        for b in range(nb):
            prev[b] = x[(b + 1) * C - 1:(b + 1) * C, :]
        return x + (shifted - x) * mu

    r = mixed(r_ref, prr, vec(0))
    k = mixed(k_ref, prk, vec(1))
    v = mixed(v_ref, prv, vec(2))
    lo = mixed(lo_ref, prlo, mulo_ref[...])
    yield

    lane_lo = lax.broadcasted_iota(jnp.int32, lo.shape, 1)
    act = jnp.where(lane_lo < 64, jnp.tanh(lo), jnp.where(lane_lo < 128, lo, _sigmoid(lo))).astype(BF16)
    z = vec(3) + _dot(act, lw_ref[0])
    a = _sigmoid(vec(4) + _dot(act, lw_ref[1]))
    g = _dot(act, lw_ref[2])
    logw = -DECAY_SCALE * _sigmoid(z)
    yield

    kk = k * vec(5)
    kk = kk * lax.rsqrt(jnp.maximum(_head_sum(kk * kk, e_ref), 1e-24))
    k = k * (1.0 + (a - 1.0) * vec(6))
    bonus = _head_sum(r * k * vec(7), e_ref)
    yield

    if t_valid < C:
        valid = tpos < t_valid
        logw = jnp.where(valid, logw, 0.0)
        kk = jnp.where(valid, kk, 0.0)
        k = jnp.where(valid, k, 0.0)
        v = jnp.where(valid, v, 0.0)

    ti = lax.broadcasted_iota(jnp.int32, (R, R), 0)
    tj = lax.broadcasted_iota(jnp.int32, (R, R), 1)
    tril = jnp.where((tj <= ti) & (tj >= ti - (ti & (C - 1))), 1.0, 0.0).astype(BF16)
    lw_hi, lw_lo = _split(logw)
    cum = _dot(jnp.concatenate([tril, tril], axis=1), jnp.concatenate([lw_hi, lw_lo], axis=0))
    e_inc = jnp.exp(cum)
    e_dec = jnp.exp(-cum)
    yield
    a_t = -kk * jnp.exp(cum - logw)
    r_t = r * e_inc
    b_t = kk * a * e_dec
    k_t = k * e_dec
    w_end = [e_inc[(b + 1) * C - 1:(b + 1) * C, :] for b in range(nb)]
    w_end_rows = per_seq_rows(w_end)
    b_h = b_t * w_end_rows
    k_h = k_t * w_end_rows

    lane = lax.broadcasted_iota(jnp.int32, (C, PAIR), 1)
    first = lane < HEAD_DIM
    t_row = lax.broadcasted_iota(jnp.int32, (C, PAIR), 0)
    s_lane = lane & (HEAD_DIM - 1)
    strict = s_lane < t_row
    incl = s_lane <= t_row
    bi = lax.broadcasted_iota(jnp.int32, (PAIR, PAIR), 0)
    bj = lax.broadcasted_iota(jnp.int32, (PAIR, PAIR), 1)
    same_head = (bi < HEAD_DIM) == (bj < HEAD_DIM)

    def blockdiag(x):
        z = jnp.zeros_like(x)
        return jnp.concatenate([jnp.where(first, x, z), jnp.where(first, z, x)], axis=0)

    def mm3(x, y):
        xh, xl = _split(x)
        yh, yl = _split(y)
        yh, yl = blockdiag(yh), blockdiag(yl)
        w = jnp.concatenate([jnp.concatenate([yh, yl], axis=1),
                             jnp.concatenate([yh, jnp.zeros_like(yh)], axis=1)], axis=0)
        res = _dot(jnp.concatenate([xh, xl], axis=1), w)
        return res[:, :PAIR] + res[:, PAIR:]

    def mm(x, y, passes):
        if passes == 1:
            return _dot(x.astype(BF16), blockdiag(y.astype(BF16)))
        return mm3(x, y)

    yield
    units = [(b, p) for b in range(nb) for p in range(N_PAIRS)]
    U = range(len(units))
    cut = [(seq_rows[b], slice(p * PAIR, (p + 1) * PAIR)) for b, p in units]
    lhs = [jnp.concatenate([a_t[cut[u]], r_t[cut[u]]], axis=0).astype(BF16) for u in U]
    rhs = [jnp.concatenate([blockdiag(b_t[cut[u]].astype(BF16)), blockdiag(k_t[cut[u]].astype(BF16))], axis=0)
           for u in U]
    pm = [_dot_nt(lhs[u], rhs[u]) for u in U]
    yield
    m = [jnp.where(strict, pm[u][0:C, 0:PAIR], 0.0) for u in U]
    akrk = [jnp.concatenate([jnp.where(strict, pm[u][0:C, PAIR:], 0.0),
                             jnp.where(incl, pm[u][C:, PAIR:], 0.0)], axis=0).astype(BF16) for u in U]
    rb = [jnp.where(incl, pm[u][C:, 0:PAIR], 0.0).astype(BF16) for u in U]
    yield
    xr = m
    for sq_passes, up_passes in PASSES_INVERSE:
        m = [mm(m[u], m[u], sq_passes) for u in U]
        xr = [xr[u] + m[u] + mm(xr[u], m[u], up_passes) for u in U]
        yield
    s2 = [s_scr[b, p] for b, p in units]
    as_rs = [_dot_nt(lhs[u], s2[u].astype(BF16)) for u in U]
    vb = [v[cut[u]].astype(BF16) for u in U]
    pv = [_dot(akrk[u], blockdiag(vb[u])) for u in U]
    yield
    rhs0 = [as_rs[u][0:C] + pv[u][0:C] for u in U]
    sab = [(rhs0[u] + mm(xr[u], rhs0[u], 1)).astype(BF16) for u in U]
    yield
    y_units = [as_rs[u][C:] + pv[u][C:] + _dot(rb[u], blockdiag(sab[u])) for u in U]
    bhkh = [jnp.concatenate([b_h[cut[u]], k_h[cut[u]]], axis=0).astype(BF16) for u in U]
    for u, (b, p) in enumerate(units):
        upd = _dot_tn(jnp.concatenate([sab[u], vb[u]], axis=0), bhkh[u])
        s_scr[b, p] = s2[u] * w_end[b][:, cut[u][1]] + jnp.where(same_head, upd, 0.0)

    yield
    y = jnp.concatenate([jnp.concatenate(y_units[b * N_PAIRS:(b + 1) * N_PAIRS], axis=1) for b in range(nb)], axis=0)
    mean = _head_sum(y, e_ref) * (1.0 / HEAD_DIM)
    yield
    yc = y - mean
    var = _head_sum(yc * yc, e_ref) * (1.0 / HEAD_DIM)
    yield
    yn = yc * lax.rsqrt(var + GN_EPS) * vec(8) + vec(9)
    y_ref[...] = ((yn + bonus * v) * g).astype(y_ref.dtype).reshape(nb, C, D_MODEL)


def _rwkv(cols3, prev, s0, vec, mulo, lw, e256, t_valid):
    b, tp, _ = cols3.shape
    C = CHUNK
    nb = _pick(b, (RWKV_SEQS_PER_GROUP, 1))
    ng = _pick(b // nb, (RWKV_GROUPS_PER_STEP, 1))
    nbt = nb * ng
    rowspec = lambda w: pl.BlockSpec((nbt, 1, w), lambda bi, ci: (bi, 0, 0))
    full = lambda a: pl.BlockSpec(a.shape, lambda bi, ci: (0,) * a.ndim)
    sspec = pl.BlockSpec((nbt, N_PAIRS, PAIR, PAIR), lambda bi, ci: (bi, 0, 0, 0))
    return pl.pallas_call(
        functools.partial(_rwkv_kernel, t_valid=min(t_valid, C), nb=nb, ng=ng),
        grid=(b // nbt, tp // C),
        in_specs=[pl.BlockSpec((nbt, C, RWKV_COLS), lambda bi, ci: (bi, ci, 0)),
                  rowspec(D_MODEL), rowspec(D_MODEL), rowspec(D_MODEL), rowspec(LORA_COLS),
                  sspec, full(vec), full(mulo), full(lw), full(e256)],
        out_specs=[pl.BlockSpec((nbt, C, D_MODEL), lambda bi, ci: (bi, ci, 0)), sspec],
        out_shape=[jax.ShapeDtypeStruct((b, tp, D_MODEL), BF16),
                   jax.ShapeDtypeStruct(s0.shape, F32)],
        scratch_shapes=[pltpu.VMEM((nbt, N_PAIRS, PAIR, PAIR), F32),
                        pltpu.VMEM((nbt, 1, D_MODEL), F32), pltpu.VMEM((nbt, 1, D_MODEL), F32),
                        pltpu.VMEM((nbt, 1, D_MODEL), F32), pltpu.VMEM((nbt, 1, LORA_COLS), F32)],
        compiler_params=_cparams(("arbitrary", "arbitrary")),
        name="rwkv7",
    )(cols3, *prev, s0, vec, mulo, lw, e256)


def _sgu_kernel(u_ref, v_ref, ln_ref, ws_ref, bs_ref, y_ref, *maybe_v_out, lc):
    u = u_ref[...].astype(F32)
    v = v_ref[...].astype(F32)
    mu = jnp.mean(v, axis=-1, keepdims=True)
    vc = v - mu
    var = jnp.mean(vc * vc, axis=-1, keepdims=True)
    v = vc * lax.rsqrt(var + LN_EPS) * ln_ref[0:1, :] + ln_ref[1:2, :]
    if maybe_v_out:
        maybe_v_out[0][...] = v
    vb = v.astype(BF16)
    n_chunks = v.shape[0] // lc
    ri = lax.broadcasted_iota(jnp.int32, (lc, lc), 0)
    ci = lax.broadcasted_iota(jnp.int32, (lc, lc), 1)
    bias = bs_ref[...]
    outs = []
    for gi in range(GMLP_GROUPS):
        gs = slice(gi * GMLP_CHUNK, (gi + 1) * GMLP_CHUNK)
        w = jnp.where(ci <= ri, ws_ref[gi], 0.0).astype(BF16)
        rhs = jnp.concatenate([vb[n * lc:(n + 1) * lc, gs] for n in range(n_chunks)], axis=1)
        sv = _dot(w, rhs)
        sv = jnp.concatenate([sv[:, n * GMLP_CHUNK:(n + 1) * GMLP_CHUNK] for n in range(n_chunks)], axis=0)
        outs.append(sv + bias[:, gs])
    sv = jnp.concatenate(outs, axis=1)
    y_ref[...] = (u * sv).astype(y_ref.dtype)


def _sgu(uv3, ln, ws, bs_full, lc, n_chunks, want_v):
    b, t, _ = uv3.shape
    tt = lc * n_chunks
    col = lambda j: pl.BlockSpec((None, tt, D_MODEL), lambda bi, ti: (bi, ti, j))
    full = lambda a: pl.BlockSpec(a.shape, lambda bi, ti: (0,) * a.ndim)
    ospec = pl.BlockSpec((None, tt, D_MODEL), lambda bi, ti: (bi, ti, 0))
    out_specs = [ospec]
    out_shape = [jax.ShapeDtypeStruct((b, t, D_MODEL), BF16)]
    if want_v:
        out_specs.append(ospec)
        out_shape.append(jax.ShapeDtypeStruct((b, t, D_MODEL), F32))
    return pl.pallas_call(
        functools.partial(_sgu_kernel, lc=lc),
        grid=(b, t // tt),
        in_specs=[col(0), col(1), full(ln), full(ws), full(bs_full)],
        out_specs=out_specs,
        out_shape=out_shape,
        compiler_params=_cparams(("arbitrary", "arbitrary")),
        name="sgu",
    )(uv3, uv3, ln, ws, bs_full)


def _xattn_kernel(q_ref, k_ref, v_ref, o_ref):
    outs = []
    for h in range(X_HEADS):
        hs = slice(h * X_HEAD_DIM, (h + 1) * X_HEAD_DIM)
        s = _dot_nt(q_ref[:, hs], k_ref[:, hs]) * (X_HEAD_DIM ** -0.5)
        s = s - jnp.max(s, axis=-1, keepdims=True)
        e = jnp.exp(s)
        p = e / jnp.sum(e, axis=-1, keepdims=True)
        outs.append(_dot(p.astype(BF16), v_ref[:, hs]))
    o_ref[...] = jnp.concatenate(outs, axis=1).astype(o_ref.dtype)


def _xattn(gq3, mem_k, mem_v, tq):
    b, t, _ = gq3.shape
    mspec = pl.BlockSpec((None, N_MEM, D_MODEL), lambda bi, ti: (bi, 0, 0))
    return pl.pallas_call(
        _xattn_kernel,
        grid=(b, t // tq),
        in_specs=[pl.BlockSpec((None, tq, D_MODEL), lambda bi, ti: (bi, ti, GQ_Q_BLOCK)), mspec, mspec],
        out_specs=pl.BlockSpec((None, tq, D_MODEL), lambda bi, ti: (bi, ti, 0)),
        out_shape=jax.ShapeDtypeStruct((b, t, D_MODEL), BF16),
        compiler_params=_cparams(("arbitrary", "arbitrary")),
        name="xattn",
    )(gq3, mem_k, mem_v)


def _merge_kernel(x_ref, gate_ref, ya_ref, yb_ref, yc_ref, wb_ref, wo_ref, o_ref):
    gate = _sigmoid(gate_ref[...].astype(F32))
    merged = None
    for bi, y_ref in enumerate((ya_ref, yb_ref, yc_ref)):
        term = gate[:, bi * D_MODEL:(bi + 1) * D_MODEL] * _dot(y_ref[...], wb_ref[bi])
        merged = term if merged is None else merged + term
    o_ref[...] = x_ref[...] + _dot(merged.astype(BF16), wo_ref[...])


def _merge(x, gq, ya, yb, yc, wb, wo, tm):
    t, d = x.shape
    tok = lambda w: pl.BlockSpec((tm, w), lambda i: (i, 0))
    return pl.pallas_call(
        _merge_kernel,
        grid=(t // tm,),
        in_specs=[tok(d), tok(3 * d), tok(d), tok(d), tok(d),
                  pl.BlockSpec(wb.shape, lambda i: (0, 0, 0)), pl.BlockSpec(wo.shape, lambda i: (0, 0))],
        out_specs=tok(d),
        out_shape=jax.ShapeDtypeStruct((t, d), F32),
        compiler_params=_cparams(("arbitrary",)),
        name="merge",
    )(x, gq, ya, yb, yc, wb, wo)


def _ffn_kernel(x_ref, g_ref, wu_ref, wd_ref, gf_ref, o_ref, h_scr, *, final_norm):
    f = pl.program_id(1)

    @pl.when(f == 0)
    def _init():
        x = x_ref[...]
        h_scr[...] = _rmsnorm(x, g_ref[...]).astype(BF16)
        o_ref[...] = x

    up = _dot(h_scr[...], wu_ref[...])
    act = jnp.square(jnp.maximum(up, 0.0)).astype(BF16)
    o_ref[...] += _dot(act, wd_ref[...])

    if final_norm:
        @pl.when(f == pl.num_programs(1) - 1)
        def _final():
            o_ref[...] = _rmsnorm(o_ref[...], gf_ref[...])


def _ffn(x, g, wu, wd, g_final, final_norm, tm, tf):
    t, d = x.shape
    return pl.pallas_call(
        functools.partial(_ffn_kernel, final_norm=final_norm),
        grid=(t // tm, D_FF // tf),
        in_specs=[pl.BlockSpec((tm, d), lambda i, f: (i, 0)),
                  pl.BlockSpec((1, d), lambda i, f: (0, 0)),
                  pl.BlockSpec((d, tf), lambda i, f: (0, f)),
                  pl.BlockSpec((tf, d), lambda i, f: (f, 0)),
                  pl.BlockSpec((1, d), lambda i, f: (0, 0))],
        out_specs=pl.BlockSpec((tm, d), lambda i, f: (i, 0)),
        out_shape=jax.ShapeDtypeStruct((t, d), F32),
        scratch_shapes=[pltpu.VMEM((tm, d), BF16)],
        compiler_params=_cparams(("arbitrary", "arbitrary")),
        name="ffn",
    )(x, g, wu, wd, g_final)


def _state_to_pairs(s):
    b = s.shape[0]
    s = s.reshape(b, N_PAIRS, 2, HEAD_DIM, HEAD_DIM)
    z = jnp.zeros_like(s[:, :, 0])
    top = jnp.concatenate([s[:, :, 0], z], axis=-1)
    bot = jnp.concatenate([z, s[:, :, 1]], axis=-1)
    return jnp.concatenate([top, bot], axis=-2)


def _pairs_to_state(s2):
    b = s2.shape[0]
    s0 = s2[:, :, :HEAD_DIM, :HEAD_DIM]
    s1 = s2[:, :, HEAD_DIM:, HEAD_DIM:]
    return jnp.stack([s0, s1], axis=2).reshape(b, N_HEADS, HEAD_DIM, HEAD_DIM)


def _pick(t, candidates):
    for c in candidates:
        if t % c == 0:
            return c
    return t


def _group_layer(x, b, t, mem_k, mem_v, prev_row, s0_pairs, lp):
    n_tok = b * t
    tm = _pick(n_tok, (512, 256))
    g_mix = lp["norm_mix_g"]
    cols3 = _proj(x, g_mix, lp["w_in_rwkv"], BF16, tm, RWKV_COLS).reshape(b, t, RWKV_COLS)
    uv3 = _proj(x, g_mix, lp["w_in_gmlp"], BF16, tm, 2 * D_MODEL, gelu=True).reshape(b, t, 2 * D_MODEL)
    gq = _proj(x, g_mix, lp["w_in_gq"], BF16, tm, 4 * D_MODEL)
    gq3 = gq.reshape(b, t, 4 * D_MODEL)
    new_row = cols3[:, t - 1:t, :].astype(F32)

    tp = -(-t // CHUNK) * CHUNK
    cols3p = cols3 if tp == t else jnp.pad(cols3, ((0, 0), (0, tp - t), (0, 0)))
    prev = (prev_row[..., :D_MODEL], prev_row[..., D_MODEL:2 * D_MODEL], prev_row[..., 2 * D_MODEL:3 * D_MODEL],
            prev_row[..., 3 * D_MODEL:])
    y_a, s_new = _rwkv(cols3p, prev, s0_pairs, lp["rwkv_vec"], lp["rwkv_mu_lo"], lp["rwkv_lw"], lp["e256"], t)
    y_a = y_a[:, :t].reshape(n_tok, D_MODEL)

    lc = min(t, GMLP_CHUNK)
    n_chunks = _pick(t // lc, (4, 2, 1))
    ws = lp["sgu_w_s"][:, :lc, :lc]
    bs_full = jnp.tile(jnp.repeat(lp["sgu_b_s"][:, :lc].T, GMLP_CHUNK, axis=1), (n_chunks, 1))
    sgu_out = _sgu(uv3, lp["sgu_ln"], ws, bs_full, lc, n_chunks, want_v=t < GMLP_CHUNK)
    y_b = sgu_out[0].reshape(n_tok, D_MODEL)
    v_rows = sgu_out[1] if t < GMLP_CHUNK else None

    y_c = _xattn(gq3, mem_k, mem_v, _pick(t, (512, 256, 128))).reshape(n_tok, D_MODEL)

    x = _merge(x, gq, y_a, y_b, y_c, lp["w_branch"], lp["w_out"], _pick(n_tok, (512, 256)))
    x = _ffn(x, lp["norm_ffn_g"], lp["w_ffn_up"], lp["w_ffn_down"], lp["norm_final_g"], lp["is_last"],
             _pick(n_tok, (1024, 512, 256)), 1024)
    return x, new_row, s_new, v_rows


def kernel(x_prompt, x_sample, cache_mem_k, cache_mem_v, state_wkv, state_shift, mem_prompt, norm_mix_g, norm_mem_g, norm_ffn_g, norm_final_g, w_in, w_mem_kv, rwkv_mu, rwkv_w0, rwkv_w2, rwkv_a0, rwkv_a2, rwkv_g2, rwkv_k_k, rwkv_k_a, rwkv_r_k, rwkv_lnx_g, rwkv_lnx_b, sgu_ln_g, sgu_ln_b, sgu_w_s, sgu_b_s, w_branch, w_out, w_ffn_up, w_ffn_down):
    depth = w_in.shape[0]
    bp, tp, _ = x_prompt.shape
    bs, ts, _ = x_sample.shape

    w_in_rwkv = w_in[..., :RWKV_COLS].astype(BF16)
    w_in_gmlp = w_in[..., RWKV_COLS:COL_Q_START].astype(BF16)
    w_in_gq = jnp.concatenate([w_in[..., COL_Q_START + D_MODEL:], w_in[..., COL_Q_START:COL_Q_START + D_MODEL]],
                              axis=-1).astype(BF16)
    w_kv_b = w_mem_kv.astype(BF16)
    w_branch_b = w_branch.astype(BF16)
    w_out_b = w_out.astype(BF16)
    w_up_b = w_ffn_up.astype(BF16)
    w_down_b = w_ffn_down.astype(BF16)
    zeros = lambda n: jnp.zeros((depth, n, D_MODEL), F32)
    lw = jnp.stack([jnp.concatenate([rwkv_w2, zeros(192)], axis=1),
                    jnp.concatenate([zeros(64), rwkv_a2, zeros(128)], axis=1),
                    jnp.concatenate([zeros(128), rwkv_g2], axis=1)], axis=1).astype(BF16)
    rwkv_vec = jnp.stack([rwkv_mu[:, :D_MODEL], rwkv_mu[:, D_MODEL:2 * D_MODEL], rwkv_mu[:, 2 * D_MODEL:3 * D_MODEL],
                          rwkv_w0, rwkv_a0, rwkv_k_k, rwkv_k_a, rwkv_r_k.reshape(depth, D_MODEL),
                          rwkv_lnx_g, rwkv_lnx_b] + [jnp.zeros((depth, D_MODEL), F32)] * 6, axis=1)
    sgu_ln = jnp.stack([sgu_ln_g, sgu_ln_b] + [jnp.zeros((depth, D_MODEL), F32)] * 6, axis=1)
    hq = 4 * HEAD_DIM
    e256 = (jnp.arange(hq)[:, None] // HEAD_DIM == jnp.arange(hq)[None, :] // HEAD_DIM).astype(BF16)

    xp = x_prompt.reshape(bp * tp, D_MODEL)
    xs = x_sample.reshape(bs * ts, D_MODEL)
    mem_flat = mem_prompt.reshape(bp * N_MEM, D_MODEL)
    prompt_row0 = jnp.zeros((bp, 1, RWKV_COLS), F32)
    prompt_s0 = jnp.zeros((bp, N_PAIRS, PAIR, PAIR), F32)

    mk_p, mv_p, wkv_p, row_p, wkv_s, row_s, v_s = [], [], [], [], [], [], []
    for l in range(depth):
        lp = {
            "norm_mix_g": norm_mix_g[l][None],
            "w_in_rwkv": w_in_rwkv[l], "w_in_gmlp": w_in_gmlp[l], "w_in_gq": w_in_gq[l],
            "rwkv_vec": rwkv_vec[l], "rwkv_mu_lo": rwkv_mu[l][None, 3 * D_MODEL:], "rwkv_lw": lw[l], "e256": e256,
            "sgu_ln": sgu_ln[l], "sgu_w_s": sgu_w_s[l], "sgu_b_s": sgu_b_s[l],
            "w_branch": w_branch_b[l], "w_out": w_out_b[l],
            "norm_ffn_g": norm_ffn_g[l][None], "w_ffn_up": w_up_b[l], "w_ffn_down": w_down_b[l],
            "norm_final_g": norm_final_g[None], "is_last": l == depth - 1,
        }
        kv = _proj(mem_flat, norm_mem_g[l][None], w_kv_b[l], F32, _pick(bp * N_MEM, (512, 256)), 2 * D_MODEL)
        mem_k = kv[:, :D_MODEL].reshape(bp, N_MEM, D_MODEL)
        mem_v = kv[:, D_MODEL:].reshape(bp, N_MEM, D_MODEL)
        xp, r_p, s_p, _ = _group_layer(xp, bp, tp, mem_k.astype(BF16), mem_v.astype(BF16), prompt_row0, prompt_s0, lp)
        mk_p.append(mem_k.reshape(bp, N_MEM, X_HEADS, X_HEAD_DIM))
        mv_p.append(mem_v.reshape(bp, N_MEM, X_HEADS, X_HEAD_DIM))
        wkv_p.append(_pairs_to_state(s_p))
        row_p.append(r_p)

        ck = cache_mem_k[l].reshape(bs, N_MEM, D_MODEL).astype(BF16)
        cv = cache_mem_v[l].reshape(bs, N_MEM, D_MODEL).astype(BF16)
        xs, r_s, s_s, vr = _group_layer(xs, bs, ts, ck, cv, _permute_rwkv_prev(state_shift[l]),
                                        _state_to_pairs(state_wkv[l]), lp)
        wkv_s.append(_pairs_to_state(s_s))
        row_s.append(r_s)
        v_s.append(vr)

    y_prompt = xp.reshape(bp, tp, D_MODEL)
    y_sample = xs.reshape(bs, ts, D_MODEL)
    return (y_prompt, y_sample, jnp.stack(mk_p), jnp.stack(mv_p), jnp.stack(wkv_p), jnp.stack(row_p),
            jnp.stack(wkv_s), jnp.stack(row_s), jnp.stack(v_s))


def _permute_rwkv_prev(row):
    return row
```

```python
import functools
import math

import jax
import jax.numpy as jnp
from jax import lax
from jax.experimental import pallas as pl
from jax.experimental.pallas import tpu as pltpu

F32 = jnp.float32
BF16 = jnp.bfloat16

D_MODEL = 1024
HEAD_DIM = 64
N_HEADS = D_MODEL // HEAD_DIM
PAIR = 2 * HEAD_DIM
N_PAIRS = D_MODEL // PAIR
LORA_COLS = 256
RWKV_COLS = 3 * D_MODEL + LORA_COLS
GMLP_CHUNK = 128
GMLP_GROUPS = 8
N_MEM = 256
X_HEADS = 4
X_HEAD_DIM = D_MODEL // X_HEADS
D_FF = 4 * D_MODEL
COL_Q_START = RWKV_COLS + 2 * D_MODEL
GQ_Q_BLOCK = 3
RMS_EPS = 1e-6
LN_EPS = 1e-5
GN_EPS = 64e-5
CHUNK = 64
DECAY_SCALE = math.exp(-0.5)
VMEM_LIMIT = 56 * 1024 * 1024
PASSES_INVERSE = ((3, 3), (3, 3), (3, 3), (1, 1), (1, 1))
RWKV_SEQS_PER_STEP = 2


def _cparams(sem):
    return pltpu.CompilerParams(dimension_semantics=sem, vmem_limit_bytes=VMEM_LIMIT)


def _dot(a, b):
    return jnp.dot(a, b, preferred_element_type=F32)


def _dot_nt(a, b):
    return lax.dot_general(a, b, (((1,), (1,)), ((), ())), preferred_element_type=F32)


def _dot_tn(a, b):
    return lax.dot_general(a, b, (((0,), (0,)), ((), ())), preferred_element_type=F32)


def _split(x):
    hi = x.astype(BF16)
    lo = (x - hi.astype(F32)).astype(BF16)
    return hi, lo


def _rmsnorm(x, g):
    return x * lax.rsqrt(jnp.mean(x * x, axis=-1, keepdims=True) + RMS_EPS) * g


def _sigmoid(x):
    return 1.0 / (1.0 + jnp.exp(-x))


def _gelu(x):
    return 0.5 * x * (1.0 + jnp.tanh(math.sqrt(2.0 / math.pi) * (x + 0.044715 * (x * x * x))))


def _proj_kernel(x_ref, g_ref, w_ref, o_ref):
    h = _rmsnorm(x_ref[...], g_ref[...]).astype(BF16)
    o_ref[...] = _dot(h, w_ref[...]).astype(o_ref.dtype)


def _proj(x, g, w, out_dtype, tm, tn):
    t, d = x.shape
    nc = w.shape[1]
    return pl.pallas_call(
        _proj_kernel,
        grid=(nc // tn, t // tm),
        in_specs=[pl.BlockSpec((tm, d), lambda j, i: (i, 0)),
                  pl.BlockSpec((1, d), lambda j, i: (0, 0)),
                  pl.BlockSpec((d, tn), lambda j, i: (0, j))],
        out_specs=pl.BlockSpec((tm, tn), lambda j, i: (i, j)),
        out_shape=jax.ShapeDtypeStruct((t, nc), out_dtype),
        compiler_params=_cparams(("arbitrary", "arbitrary")),
        name="proj",
    )(x, g, w)


def _head_sum(x, e_ref):
    c = x.shape[0]
    q = 4 * HEAD_DIM
    xs = jnp.concatenate([x[:, i * q:(i + 1) * q] for i in range(D_MODEL // q)], axis=0)
    r = _dot(xs.astype(BF16), e_ref[...])
    return jnp.concatenate([r[i * c:(i + 1) * c] for i in range(D_MODEL // q)], axis=1)


def _rwkv_kernel(cols_ref, next_ref, pr_ref, pk_ref, pv_ref, plo_ref, s0_ref,
                 vec_ref, mulo_ref, lw_ref, e_ref,
                 y_ref, sout_ref,
                 s_scr, prr, prk, prv, prlo, ops_x, aux_x, wend_x, ops_y, aux_y, wend_y, *, t_valid, nb):
    i = pl.program_id(1)
    C = CHUNK
    prev = (prr, prk, prv, prlo)
    buf_x, buf_y = (ops_x, aux_x, wend_x), (ops_y, aux_y, wend_y)

    def col_views(ref, rows):
        return ([ref.at[:, rows, j * D_MODEL:(j + 1) * D_MODEL] for j in range(3)]
                + [ref.at[:, rows, 3 * D_MODEL:RWKV_COLS]])

    def prep(ref, rows, chunk, buf):
        return _rwkv_prep(col_views(ref, rows), prev, vec_ref, mulo_ref, lw_ref, e_ref, buf,
                          t_valid=t_valid, chunk=chunk, nb=nb)

    def mix(buf, rows):
        return _rwkv_mix(buf, s_scr, y_ref.at[:, rows, :], vec_ref, e_ref, nb=nb)

    def emit(main, side):
        for _ in main:
            next(side, None)
        for _ in side:
            pass

    @pl.when(i == 0)
    def _init():
        s_scr[...] = s0_ref[...]
        prr[...] = pr_ref[...]
        prk[...] = pk_ref[...]
        prv[...] = pv_ref[...]
        prlo[...] = plo_ref[...]
        emit(prep(cols_ref, slice(0, C), 0, buf_x), iter(()))

    emit(mix(buf_x, slice(0, C)), prep(cols_ref, slice(C, 2 * C), 2 * i + 1, buf_y))
    emit(mix(buf_y, slice(C, 2 * C)), prep(next_ref, slice(0, C), 2 * i + 2, buf_x))

    @pl.when(i == pl.num_programs(1) - 1)
    def _fin():
        sout_ref[...] = s_scr[...]


def _rwkv_prep(col_refs, prev, vec_ref, mulo_ref, lw_ref, e_ref, buf, *, t_valid, chunk, nb):
    r_ref, k_ref, v_ref, lo_ref = col_refs
    prr, prk, prv, prlo = prev
    ops, aux, wend = buf
    C = CHUNK
    R = nb * C

    def vec(i):
        return vec_ref[i:i + 1, :]

    row = lax.broadcasted_iota(jnp.int32, (R, 1), 0)
    tpos = row & (C - 1)
    seq_rows = [slice(b * C, (b + 1) * C) for b in range(nb)]

    def per_seq_rows(rows):
        out = rows[nb - 1]
        for b in reversed(range(nb - 1)):
            out = jnp.where(row < (b + 1) * C, rows[b], out)
        return out

    def mixed(ref, prev, mu):
        x = ref[...].astype(F32).reshape(R, ref.shape[-1])
        shifted = jnp.where(tpos == 0, per_seq_rows([prev[b] for b in range(nb)]), pltpu.roll(x, 1, 0))
        for b in range(nb):
            prev[b] = x[(b + 1) * C - 1:(b + 1) * C, :]
        return x + (shifted - x) * mu

    r = mixed(r_ref, prr, vec(0))
    k = mixed(k_ref, prk, vec(1))
    v = mixed(v_ref, prv, vec(2))
    lo = mixed(lo_ref, prlo, mulo_ref[...])
    yield

    lane_lo = lax.broadcasted_iota(jnp.int32, lo.shape, 1)
    act = jnp.where(lane_lo < 64, jnp.tanh(lo), jnp.where(lane_lo < 128, lo, _sigmoid(lo))).astype(BF16)
    z = vec(3) + _dot(act, lw_ref[0])
    a = _sigmoid(vec(4) + _dot(act, lw_ref[1]))
    g = _dot(act, lw_ref[2])
    logw = -DECAY_SCALE * _sigmoid(z)
    yield

    kk = k * vec(5)
    kk = kk * lax.rsqrt(jnp.maximum(_head_sum(kk * kk, e_ref), 1e-24))
    k = k * (1.0 + (a - 1.0) * vec(6))
    bonus = _head_sum(r * k * vec(7), e_ref)
    yield

    if t_valid is not None:
        valid = chunk * C + tpos < t_valid
        logw = jnp.where(valid, logw, 0.0)
        kk = jnp.where(valid, kk, 0.0)
        k = jnp.where(valid, k, 0.0)
        v = jnp.where(valid, v, 0.0)

    ti = lax.broadcasted_iota(jnp.int32, (R, R), 0)
    tj = lax.broadcasted_iota(jnp.int32, (R, R), 1)
    tril = jnp.where((tj <= ti) & (tj >= ti - (ti & (C - 1))), 1.0, 0.0).astype(BF16)
    lw_hi, lw_lo = _split(logw)
    cum = _dot(jnp.concatenate([tril, tril], axis=1), jnp.concatenate([lw_hi, lw_lo], axis=0))
    e_inc = jnp.exp(cum)
    e_dec = jnp.exp(-cum)
    yield
    a_t = -kk * jnp.exp(cum - logw)
    r_t = r * e_inc
    b_t = kk * a * e_dec
    k_t = k * e_dec
    w_end = [e_inc[(b + 1) * C - 1:(b + 1) * C, :] for b in range(nb)]
    w_end_rows = per_seq_rows(w_end)
    for j, op in enumerate((a_t, r_t, b_t, k_t, b_t * w_end_rows, k_t * w_end_rows, v)):
        ops[j] = op.astype(BF16)
    aux[0] = bonus * v
    aux[1] = g
    for b in range(nb):
        wend[b] = w_end[b]


N_OPS = 7


def _rwkv_mix(buf, s_scr, y_ref, vec_ref, e_ref, *, nb):
    ops, aux, wend = buf
    C = CHUNK
    seq_rows = [slice(b * C, (b + 1) * C) for b in range(nb)]

    def vec(i):
        return vec_ref[i:i + 1, :]

    lane = lax.broadcasted_iota(jnp.int32, (C, PAIR), 1)
    first = lane < HEAD_DIM
    t_row = lax.broadcasted_iota(jnp.int32, (C, PAIR), 0)
    s_lane = lane & (HEAD_DIM - 1)
    strict = s_lane < t_row
    incl = s_lane <= t_row
    bi = lax.broadcasted_iota(jnp.int32, (PAIR, PAIR), 0)
    bj = lax.broadcasted_iota(jnp.int32, (PAIR, PAIR), 1)
    same_head = (bi < HEAD_DIM) == (bj < HEAD_DIM)

    def blockdiag(x):
        z = jnp.zeros_like(x)
        return jnp.concatenate([jnp.where(first, x, z), jnp.where(first, z, x)], axis=0)

    def mm3(x, y):
        xh, xl = _split(x)
        yh, yl = _split(y)
        yh, yl = blockdiag(yh), blockdiag(yl)
        w = jnp.concatenate([jnp.concatenate([yh, yl], axis=1),
                             jnp.concatenate([yh, jnp.zeros_like(yh)], axis=1)], axis=0)
        res = _dot(jnp.concatenate([xh, xl], axis=1), w)
        return res[:, :PAIR] + res[:, PAIR:]

    def mm(x, y, passes):
        if passes == 1:
            return _dot(x.astype(BF16), blockdiag(y.astype(BF16)))
        return mm3(x, y)

    yield
    units = [(b, p) for b in range(nb) for p in range(N_PAIRS)]
    U = range(len(units))
    cut = [(seq_rows[b], slice(p * PAIR, (p + 1) * PAIR)) for b, p in units]
    op = lambda j, u: ops[(j,) + cut[u]]
    lhs = [jnp.concatenate([op(0, u), op(1, u)], axis=0) for u in U]
    rhs = [jnp.concatenate([blockdiag(op(2, u)), blockdiag(op(3, u))], axis=0) for u in U]
    pm = [_dot_nt(lhs[u], rhs[u]) for u in U]
    yield
    m = [jnp.where(strict, pm[u][0:C, 0:PAIR], 0.0) for u in U]
    akrk = [jnp.concatenate([jnp.where(strict, pm[u][0:C, PAIR:], 0.0),
                             jnp.where(incl, pm[u][C:, PAIR:], 0.0)], axis=0).astype(BF16) for u in U]
    rb = [jnp.where(incl, pm[u][C:, 0:PAIR], 0.0).astype(BF16) for u in U]
    yield
    xr = m
    for sq_passes, up_passes in PASSES_INVERSE:
        m = [mm(m[u], m[u], sq_passes) for u in U]
        xr = [xr[u] + m[u] + mm(xr[u], m[u], up_passes) for u in U]
        yield
    s2 = [s_scr[b, p] for b, p in units]
    as_rs = [_dot_nt(lhs[u], s2[u].astype(BF16)) for u in U]
    vb = [op(6, u) for u in U]
    pv = [_dot(akrk[u], blockdiag(vb[u])) for u in U]
    yield
    rhs0 = [as_rs[u][0:C] + pv[u][0:C] for u in U]
    sab = [(rhs0[u] + mm(xr[u], rhs0[u], 1)).astype(BF16) for u in U]
    yield
    y_units = [as_rs[u][C:] + pv[u][C:] + _dot(rb[u], blockdiag(sab[u])) for u in U]
    bhkh = [jnp.concatenate([op(4, u), op(5, u)], axis=0) for u in U]
    for u, (b, p) in enumerate(units):
        upd = _dot_tn(jnp.concatenate([sab[u], vb[u]], axis=0), bhkh[u])
        s_scr[b, p] = s2[u] * wend[b][:, cut[u][1]] + jnp.where(same_head, upd, 0.0)

    yield
    y = jnp.concatenate([jnp.concatenate(y_units[b * N_PAIRS:(b + 1) * N_PAIRS], axis=1) for b in range(nb)], axis=0)
    mean = _head_sum(y, e_ref) * (1.0 / HEAD_DIM)
    yield
    yc = y - mean
    var = _head_sum(yc * yc, e_ref) * (1.0 / HEAD_DIM)
    yield
    yn = yc * lax.rsqrt(var + GN_EPS) * vec(8) + vec(9)
    y_ref[...] = ((yn + aux[0]) * aux[1]).astype(y_ref.dtype).reshape(nb, C, D_MODEL)


def _rwkv(cols3, prev, s0, vec, mulo, lw, e256, t_valid):
    b, tp, _ = cols3.shape
    C = CHUNK
    nb = _pick(b, (RWKV_SEQS_PER_STEP, 1))
    n_chunks = tp // C
    rowspec = lambda w: pl.BlockSpec((nb, 1, w), lambda bi, ci: (bi, 0, 0))
    full = lambda a: pl.BlockSpec(a.shape, lambda bi, ci: (0,) * a.ndim)
    sspec = pl.BlockSpec((nb, N_PAIRS, PAIR, PAIR), lambda bi, ci: (bi, 0, 0, 0))
    operand_bufs = [pltpu.VMEM((N_OPS, nb * C, D_MODEL), BF16), pltpu.VMEM((2, nb * C, D_MODEL), F32),
                    pltpu.VMEM((nb, 1, D_MODEL), F32)]
    return pl.pallas_call(
        functools.partial(_rwkv_kernel, t_valid=t_valid if t_valid < tp else None, nb=nb),
        grid=(b // nb, n_chunks // 2),
        in_specs=[pl.BlockSpec((nb, 2 * C, RWKV_COLS), lambda bi, ci: (bi, ci, 0)),
                  pl.BlockSpec((nb, C, RWKV_COLS), lambda bi, ci: (bi, jnp.minimum(2 * ci + 2, n_chunks - 1), 0)),
                  rowspec(D_MODEL), rowspec(D_MODEL), rowspec(D_MODEL), rowspec(LORA_COLS),
                  sspec, full(vec), full(mulo), full(lw), full(e256)],
        out_specs=[pl.BlockSpec((nb, 2 * C, D_MODEL), lambda bi, ci: (bi, ci, 0)), sspec],
        out_shape=[jax.ShapeDtypeStruct((b, tp, D_MODEL), BF16),
                   jax.ShapeDtypeStruct(s0.shape, F32)],
        scratch_shapes=[pltpu.VMEM((nb, N_PAIRS, PAIR, PAIR), F32),
                        pltpu.VMEM((nb, 1, D_MODEL), F32), pltpu.VMEM((nb, 1, D_MODEL), F32),
                        pltpu.VMEM((nb, 1, D_MODEL), F32), pltpu.VMEM((nb, 1, LORA_COLS), F32)]
                       + operand_bufs + operand_bufs,
        compiler_params=_cparams(("arbitrary", "arbitrary")),
        name="rwkv7",
    )(cols3, cols3, *prev, s0, vec, mulo, lw, e256)


def _sgu_kernel(u_ref, v_ref, ln_ref, ws_ref, bs_ref, y_ref, *maybe_v_out, lc):
    u = _gelu(u_ref[...].astype(F32))
    v = _gelu(v_ref[...].astype(F32))
    mu = jnp.mean(v, axis=-1, keepdims=True)
    vc = v - mu
    var = jnp.mean(vc * vc, axis=-1, keepdims=True)
    v = vc * lax.rsqrt(var + LN_EPS) * ln_ref[0:1, :] + ln_ref[1:2, :]
    if maybe_v_out:
        maybe_v_out[0][...] = v
    vb = v.astype(BF16)
    n_chunks = v.shape[0] // lc
    ri = lax.broadcasted_iota(jnp.int32, (lc, lc), 0)
    ci = lax.broadcasted_iota(jnp.int32, (lc, lc), 1)
    bias = bs_ref[...]
    outs = []
    for gi in range(GMLP_GROUPS):
        gs = slice(gi * GMLP_CHUNK, (gi + 1) * GMLP_CHUNK)
        w = jnp.where(ci <= ri, ws_ref[gi], 0.0).astype(BF16)
        rhs = jnp.concatenate([vb[n * lc:(n + 1) * lc, gs] for n in range(n_chunks)], axis=1)
        sv = _dot(w, rhs)
        sv = jnp.concatenate([sv[:, n * GMLP_CHUNK:(n + 1) * GMLP_CHUNK] for n in range(n_chunks)], axis=0)
        outs.append(sv + bias[:, gs])
    sv = jnp.concatenate(outs, axis=1)
    y_ref[...] = (u * sv).astype(y_ref.dtype)


def _sgu(uv3, ln, ws, bs_full, lc, n_chunks, want_v):
    b, t, _ = uv3.shape
    tt = lc * n_chunks
    col = lambda j: pl.BlockSpec((None, tt, D_MODEL), lambda bi, ti: (bi, ti, j))
    full = lambda a: pl.BlockSpec(a.shape, lambda bi, ti: (0,) * a.ndim)
    ospec = pl.BlockSpec((None, tt, D_MODEL), lambda bi, ti: (bi, ti, 0))
    out_specs = [ospec]
    out_shape = [jax.ShapeDtypeStruct((b, t, D_MODEL), BF16)]
    if want_v:
        out_specs.append(ospec)
        out_shape.append(jax.ShapeDtypeStruct((b, t, D_MODEL), F32))
    return pl.pallas_call(
        functools.partial(_sgu_kernel, lc=lc),
        grid=(b, t // tt),
        in_specs=[col(0), col(1), full(ln), full(ws), full(bs_full)],
        out_specs=out_specs,
        out_shape=out_shape,
        compiler_params=_cparams(("arbitrary", "arbitrary")),
        name="sgu",
    )(uv3, uv3, ln, ws, bs_full)


def _xattn_kernel(q_ref, k_ref, v_ref, o_ref):
    outs = []
    for h in range(X_HEADS):
        hs = slice(h * X_HEAD_DIM, (h + 1) * X_HEAD_DIM)
        s = _dot_nt(q_ref[:, hs], k_ref[:, hs]) * (X_HEAD_DIM ** -0.5)
        s = s - jnp.max(s, axis=-1, keepdims=True)
        e = jnp.exp(s)
        p = e / jnp.sum(e, axis=-1, keepdims=True)
        outs.append(_dot(p.astype(BF16), v_ref[:, hs]))
    o_ref[...] = jnp.concatenate(outs, axis=1).astype(o_ref.dtype)


def _xattn(gq3, mem_k, mem_v, tq):
    b, t, _ = gq3.shape
    mspec = pl.BlockSpec((None, N_MEM, D_MODEL), lambda bi, ti: (bi, 0, 0))
    return pl.pallas_call(
        _xattn_kernel,
        grid=(b, t // tq),
        in_specs=[pl.BlockSpec((None, tq, D_MODEL), lambda bi, ti: (bi, ti, GQ_Q_BLOCK)), mspec, mspec],
        out_specs=pl.BlockSpec((None, tq, D_MODEL), lambda bi, ti: (bi, ti, 0)),
        out_shape=jax.ShapeDtypeStruct((b, t, D_MODEL), BF16),
        compiler_params=_cparams(("arbitrary", "arbitrary")),
        name="xattn",
    )(gq3, mem_k, mem_v)


def _merge_kernel(x_ref, gate_ref, ya_ref, yb_ref, yc_ref, wb_ref, wo_ref, o_ref):
    gate = _sigmoid(gate_ref[...].astype(F32))
    merged = None
    for bi, y_ref in enumerate((ya_ref, yb_ref, yc_ref)):
        term = gate[:, bi * D_MODEL:(bi + 1) * D_MODEL] * _dot(y_ref[...], wb_ref[bi])
        merged = term if merged is None else merged + term
    o_ref[...] = x_ref[...] + _dot(merged.astype(BF16), wo_ref[...])


def _merge(x, gq, ya, yb, yc, wb, wo, tm):
    t, d = x.shape
    tok = lambda w: pl.BlockSpec((tm, w), lambda i: (i, 0))
    return pl.pallas_call(
        _merge_kernel,
        grid=(t // tm,),
        in_specs=[tok(d), tok(3 * d), tok(d), tok(d), tok(d),
                  pl.BlockSpec(wb.shape, lambda i: (0, 0, 0)), pl.BlockSpec(wo.shape, lambda i: (0, 0))],
        out_specs=tok(d),
        out_shape=jax.ShapeDtypeStruct((t, d), F32),
        compiler_params=_cparams(("arbitrary",)),
        name="merge",
    )(x, gq, ya, yb, yc, wb, wo)


def _ffn_kernel(x_ref, g_ref, wu_ref, wd_ref, gf_ref, o_ref, h_scr, *, final_norm):
    f = pl.program_id(1)

    @pl.when(f == 0)
    def _init():
        x = x_ref[...]
        h_scr[...] = _rmsnorm(x, g_ref[...]).astype(BF16)
        o_ref[...] = x

    up = _dot(h_scr[...], wu_ref[...])
    act = jnp.square(jnp.maximum(up, 0.0)).astype(BF16)
    o_ref[...] += _dot(act, wd_ref[...])

    if final_norm:
        @pl.when(f == pl.num_programs(1) - 1)
        def _final():
            o_ref[...] = _rmsnorm(o_ref[...], gf_ref[...])


def _ffn(x, g, wu, wd, g_final, final_norm, tm, tf):
    t, d = x.shape
    return pl.pallas_call(
        functools.partial(_ffn_kernel, final_norm=final_norm),
        grid=(t // tm, D_FF // tf),
        in_specs=[pl.BlockSpec((tm, d), lambda i, f: (i, 0)),
                  pl.BlockSpec((1, d), lambda i, f: (0, 0)),
                  pl.BlockSpec((d, tf), lambda i, f: (0, f)),
                  pl.BlockSpec((tf, d), lambda i, f: (f, 0)),
                  pl.BlockSpec((1, d), lambda i, f: (0, 0))],
        out_specs=pl.BlockSpec((tm, d), lambda i, f: (i, 0)),
        out_shape=jax.ShapeDtypeStruct((t, d), F32),
        scratch_shapes=[pltpu.VMEM((tm, d), BF16)],
        compiler_params=_cparams(("arbitrary", "arbitrary")),
        name="ffn",
    )(x, g, wu, wd, g_final)


def _state_to_pairs(s):
    b = s.shape[0]
    s = s.reshape(b, N_PAIRS, 2, HEAD_DIM, HEAD_DIM)
    z = jnp.zeros_like(s[:, :, 0])
    top = jnp.concatenate([s[:, :, 0], z], axis=-1)
    bot = jnp.concatenate([z, s[:, :, 1]], axis=-1)
    return jnp.concatenate([top, bot], axis=-2)


def _pairs_to_state(s2):
    b = s2.shape[0]
    s0 = s2[:, :, :HEAD_DIM, :HEAD_DIM]
    s1 = s2[:, :, HEAD_DIM:, HEAD_DIM:]
    return jnp.stack([s0, s1], axis=2).reshape(b, N_HEADS, HEAD_DIM, HEAD_DIM)


def _pick(t, candidates):
    for c in candidates:
        if t % c == 0:
            return c
    return t


def _group_layer(x, b, t, mem_k, mem_v, prev_row, s0_pairs, lp):
    n_tok = b * t
    tm = _pick(n_tok, (512, 256))
    g_mix = lp["norm_mix_g"]
    cols3 = _proj(x, g_mix, lp["w_in_rwkv"], BF16, tm, RWKV_COLS).reshape(b, t, RWKV_COLS)
    uv3 = _proj(x, g_mix, lp["w_in_gmlp"], BF16, tm, 2 * D_MODEL).reshape(b, t, 2 * D_MODEL)
    gq = _proj(x, g_mix, lp["w_in_gq"], BF16, tm, 4 * D_MODEL)
    gq3 = gq.reshape(b, t, 4 * D_MODEL)
    new_row = cols3[:, t - 1:t, :].astype(F32)

    tp = -(-t // (2 * CHUNK)) * 2 * CHUNK
    cols3p = cols3 if tp == t else jnp.pad(cols3, ((0, 0), (0, tp - t), (0, 0)))
    prev = (prev_row[..., :D_MODEL], prev_row[..., D_MODEL:2 * D_MODEL], prev_row[..., 2 * D_MODEL:3 * D_MODEL],
            prev_row[..., 3 * D_MODEL:])
    y_a, s_new = _rwkv(cols3p, prev, s0_pairs, lp["rwkv_vec"], lp["rwkv_mu_lo"], lp["rwkv_lw"], lp["e256"], t)
    y_a = y_a[:, :t].reshape(n_tok, D_MODEL)

    lc = min(t, GMLP_CHUNK)
    n_chunks = _pick(t // lc, (4, 2, 1))
    ws = lp["sgu_w_s"][:, :lc, :lc]
    bs_full = jnp.tile(jnp.repeat(lp["sgu_b_s"][:, :lc].T, GMLP_CHUNK, axis=1), (n_chunks, 1))
    sgu_out = _sgu(uv3, lp["sgu_ln"], ws, bs_full, lc, n_chunks, want_v=t < GMLP_CHUNK)
    y_b = sgu_out[0].reshape(n_tok, D_MODEL)
    v_rows = sgu_out[1] if t < GMLP_CHUNK else None

    y_c = _xattn(gq3, mem_k, mem_v, _pick(t, (512, 256, 128))).reshape(n_tok, D_MODEL)

    x = _merge(x, gq, y_a, y_b, y_c, lp["w_branch"], lp["w_out"], _pick(n_tok, (512, 256)))
    x = _ffn(x, lp["norm_ffn_g"], lp["w_ffn_up"], lp["w_ffn_down"], lp["norm_final_g"], lp["is_last"],
             _pick(n_tok, (1024, 512, 256)), 1024)
    return x, new_row, s_new, v_rows


def kernel(x_prompt, x_sample, cache_mem_k, cache_mem_v, state_wkv, state_shift, mem_prompt, norm_mix_g, norm_mem_g, norm_ffn_g, norm_final_g, w_in, w_mem_kv, rwkv_mu, rwkv_w0, rwkv_w2, rwkv_a0, rwkv_a2, rwkv_g2, rwkv_k_k, rwkv_k_a, rwkv_r_k, rwkv_lnx_g, rwkv_lnx_b, sgu_ln_g, sgu_ln_b, sgu_w_s, sgu_b_s, w_branch, w_out, w_ffn_up, w_ffn_down):
    depth = w_in.shape[0]
    bp, tp, _ = x_prompt.shape
    bs, ts, _ = x_sample.shape

    w_in_rwkv = w_in[..., :RWKV_COLS].astype(BF16)
    w_in_gmlp = w_in[..., RWKV_COLS:COL_Q_START].astype(BF16)
    w_in_gq = jnp.concatenate([w_in[..., COL_Q_START + D_MODEL:], w_in[..., COL_Q_START:COL_Q_START + D_MODEL]],
                              axis=-1).astype(BF16)
    w_kv_b = w_mem_kv.astype(BF16)
    w_branch_b = w_branch.astype(BF16)
    w_out_b = w_out.astype(BF16)
    w_up_b = w_ffn_up.astype(BF16)
    w_down_b = w_ffn_down.astype(BF16)
    zeros = lambda n: jnp.zeros((depth, n, D_MODEL), F32)
    lw = jnp.stack([jnp.concatenate([rwkv_w2, zeros(192)], axis=1),
                    jnp.concatenate([zeros(64), rwkv_a2, zeros(128)], axis=1),
                    jnp.concatenate([zeros(128), rwkv_g2], axis=1)], axis=1).astype(BF16)
    rwkv_vec = jnp.stack([rwkv_mu[:, :D_MODEL], rwkv_mu[:, D_MODEL:2 * D_MODEL], rwkv_mu[:, 2 * D_MODEL:3 * D_MODEL],
                          rwkv_w0, rwkv_a0, rwkv_k_k, rwkv_k_a, rwkv_r_k.reshape(depth, D_MODEL),
                          rwkv_lnx_g, rwkv_lnx_b] + [jnp.zeros((depth, D_MODEL), F32)] * 6, axis=1)
    sgu_ln = jnp.stack([sgu_ln_g, sgu_ln_b] + [jnp.zeros((depth, D_MODEL), F32)] * 6, axis=1)
    hq = 4 * HEAD_DIM
    e256 = (jnp.arange(hq)[:, None] // HEAD_DIM == jnp.arange(hq)[None, :] // HEAD_DIM).astype(BF16)

    xp = x_prompt.reshape(bp * tp, D_MODEL)
    xs = x_sample.reshape(bs * ts, D_MODEL)
    mem_flat = mem_prompt.reshape(bp * N_MEM, D_MODEL)
    prompt_row0 = jnp.zeros((bp, 1, RWKV_COLS), F32)
    prompt_s0 = jnp.zeros((bp, N_PAIRS, PAIR, PAIR), F32)

    mk_p, mv_p, wkv_p, row_p, wkv_s, row_s, v_s = [], [], [], [], [], [], []
    for l in range(depth):
        lp = {
            "norm_mix_g": norm_mix_g[l][None],
            "w_in_rwkv": w_in_rwkv[l], "w_in_gmlp": w_in_gmlp[l], "w_in_gq": w_in_gq[l],
            "rwkv_vec": rwkv_vec[l], "rwkv_mu_lo": rwkv_mu[l][None, 3 * D_MODEL:], "rwkv_lw": lw[l], "e256": e256,
            "sgu_ln": sgu_ln[l], "sgu_w_s": sgu_w_s[l], "sgu_b_s": sgu_b_s[l],
            "w_branch": w_branch_b[l], "w_out": w_out_b[l],
            "norm_ffn_g": norm_ffn_g[l][None], "w_ffn_up": w_up_b[l], "w_ffn_down": w_down_b[l],
            "norm_final_g": norm_final_g[None], "is_last": l == depth - 1,
        }
        kv = _proj(mem_flat, norm_mem_g[l][None], w_kv_b[l], F32, _pick(bp * N_MEM, (512, 256)), 2 * D_MODEL)
        mem_k = kv[:, :D_MODEL].reshape(bp, N_MEM, D_MODEL)
        mem_v = kv[:, D_MODEL:].reshape(bp, N_MEM, D_MODEL)
        xp, r_p, s_p, _ = _group_layer(xp, bp, tp, mem_k.astype(BF16), mem_v.astype(BF16), prompt_row0, prompt_s0, lp)
        mk_p.append(mem_k.reshape(bp, N_MEM, X_HEADS, X_HEAD_DIM))
        mv_p.append(mem_v.reshape(bp, N_MEM, X_HEADS, X_HEAD_DIM))
        wkv_p.append(_pairs_to_state(s_p))
        row_p.append(r_p)

        ck = cache_mem_k[l].reshape(bs, N_MEM, D_MODEL).astype(BF16)
        cv = cache_mem_v[l].reshape(bs, N_MEM, D_MODEL).astype(BF16)
        xs, r_s, s_s, vr = _group_layer(xs, bs, ts, ck, cv, _permute_rwkv_prev(state_shift[l]),
                                        _state_to_pairs(state_wkv[l]), lp)
        wkv_s.append(_pairs_to_state(s_s))
        row_s.append(r_s)
        v_s.append(vr)

    y_prompt = xp.reshape(bp, tp, D_MODEL)
    y_sample = xs.reshape(bs, ts, D_MODEL)
    return (y_prompt, y_sample, jnp.stack(mk_p), jnp.stack(mv_p), jnp.stack(wkv_p), jnp.stack(row_p),
            jnp.stack(wkv_s), jnp.stack(row_s), jnp.stack(v_s))


def _permute_rwkv_prev(row):
    return row
```

```python
import functools
import math

import jax
import jax.numpy as jnp
from jax import lax
from jax.experimental import pallas as pl
from jax.experimental.pallas import tpu as pltpu

F32 = jnp.float32
BF16 = jnp.bfloat16

D_MODEL = 1024
HEAD_DIM = 64
N_HEADS = D_MODEL // HEAD_DIM
PAIR = 2 * HEAD_DIM
N_PAIRS = D_MODEL // PAIR
LORA_COLS = 256
RWKV_COLS = 3 * D_MODEL + LORA_COLS
GMLP_CHUNK = 128
GMLP_GROUPS = 8
N_MEM = 256
X_HEADS = 4
X_HEAD_DIM = D_MODEL // X_HEADS
D_FF = 4 * D_MODEL
COL_Q_START = RWKV_COLS + 2 * D_MODEL
GQ_Q_BLOCK = 3
RMS_EPS = 1e-6
LN_EPS = 1e-5
GN_EPS = 64e-5
CHUNK = 64
DECAY_SCALE = math.exp(-0.5)
VMEM_LIMIT = 56 * 1024 * 1024
PASSES_INVERSE = ((3, 3), (3, 3), (3, 3), (1, 1), (1, 1))
RWKV_SEQS_PER_STEP = 2


def _cparams(sem):
    return pltpu.CompilerParams(dimension_semantics=sem, vmem_limit_bytes=VMEM_LIMIT)


def _dot(a, b):
    return jnp.dot(a, b, preferred_element_type=F32)


def _dot_nt(a, b):
    return lax.dot_general(a, b, (((1,), (1,)), ((), ())), preferred_element_type=F32)


def _dot_tn(a, b):
    return lax.dot_general(a, b, (((0,), (0,)), ((), ())), preferred_element_type=F32)


def _split(x):
    hi = x.astype(BF16)
    lo = (x - hi.astype(F32)).astype(BF16)
    return hi, lo


def _rmsnorm(x, g):
    return x * lax.rsqrt(jnp.mean(x * x, axis=-1, keepdims=True) + RMS_EPS) * g


def _sigmoid(x):
    return 1.0 / (1.0 + jnp.exp(-x))


def _gelu(x):
    return 0.5 * x * (1.0 + jnp.tanh(math.sqrt(2.0 / math.pi) * (x + 0.044715 * (x * x * x))))


def _proj_kernel(x_ref, g_ref, w_ref, o_ref):
    h = _rmsnorm(x_ref[...], g_ref[...]).astype(BF16)
    o_ref[...] = _dot(h, w_ref[...]).astype(o_ref.dtype)


def _proj(x, g, w, out_dtype, tm, tn):
    t, d = x.shape
    nc = w.shape[1]
    return pl.pallas_call(
        _proj_kernel,
        grid=(nc // tn, t // tm),
        in_specs=[pl.BlockSpec((tm, d), lambda j, i: (i, 0)),
                  pl.BlockSpec((1, d), lambda j, i: (0, 0)),
                  pl.BlockSpec((d, tn), lambda j, i: (0, j))],
        out_specs=pl.BlockSpec((tm, tn), lambda j, i: (i, j)),
        out_shape=jax.ShapeDtypeStruct((t, nc), out_dtype),
        compiler_params=_cparams(("arbitrary", "arbitrary")),
        name="proj",
    )(x, g, w)


def _head_sum(x, e_ref):
    c = x.shape[0]
    q = 4 * HEAD_DIM
    xs = jnp.concatenate([x[:, i * q:(i + 1) * q] for i in range(D_MODEL // q)], axis=0)
    r = _dot(xs.astype(BF16), e_ref[...])
    return jnp.concatenate([r[i * c:(i + 1) * c] for i in range(D_MODEL // q)], axis=1)


def _rwkv_kernel(cols_ref, next_ref, pr_ref, pk_ref, pv_ref, plo_ref, s0_ref,
                 vec_ref, mulo_ref, lw_ref, e_ref,
                 y_ref, sout_ref,
                 s_scr, prr, prk, prv, prlo, ops_x, aux_x, wend_x, ops_y, aux_y, wend_y, *, t_valid, nb):
    i = pl.program_id(1)
    C = CHUNK
    prev = (prr, prk, prv, prlo)
    buf_x, buf_y = (ops_x, aux_x, wend_x), (ops_y, aux_y, wend_y)

    def col_views(ref, rows):
        return ([ref.at[:, rows, j * D_MODEL:(j + 1) * D_MODEL] for j in range(3)]
                + [ref.at[:, rows, 3 * D_MODEL:RWKV_COLS]])

    def prep(ref, rows, chunk, buf):
        return _rwkv_prep(col_views(ref, rows), prev, vec_ref, mulo_ref, lw_ref, e_ref, buf,
                          t_valid=t_valid, chunk=chunk, nb=nb)

    def mix(buf, rows):
        return _rwkv_mix(buf, s_scr, y_ref.at[:, rows, :], vec_ref, e_ref, nb=nb)

    def emit(main, side):
        for _ in main:
            next(side, None)
        for _ in side:
            pass

    @pl.when(i == 0)
    def _init():
        s_scr[...] = s0_ref[...]
        prr[...] = pr_ref[...]
        prk[...] = pk_ref[...]
        prv[...] = pv_ref[...]
        prlo[...] = plo_ref[...]
        emit(prep(cols_ref, slice(0, C), 0, buf_x), iter(()))

    emit(mix(buf_x, slice(0, C)), prep(cols_ref, slice(C, 2 * C), 2 * i + 1, buf_y))
    emit(mix(buf_y, slice(C, 2 * C)), prep(next_ref, slice(0, C), 2 * i + 2, buf_x))

    @pl.when(i == pl.num_programs(1) - 1)
    def _fin():
        sout_ref[...] = s_scr[...]


def _rwkv_prep(col_refs, prev, vec_ref, mulo_ref, lw_ref, e_ref, buf, *, t_valid, chunk, nb):
    r_ref, k_ref, v_ref, lo_ref = col_refs
    prr, prk, prv, prlo = prev
    ops, aux, wend = buf
    C = CHUNK
    R = nb * C

    def vec(i):
        return vec_ref[i:i + 1, :]

    row = lax.broadcasted_iota(jnp.int32, (R, 1), 0)
    tpos = row & (C - 1)
    seq_rows = [slice(b * C, (b + 1) * C) for b in range(nb)]

    def per_seq_rows(rows):
        out = rows[nb - 1]
        for b in reversed(range(nb - 1)):
            out = jnp.where(row < (b + 1) * C, rows[b], out)
        return out

    def mixed(ref, prev, mu):
        x = ref[...].astype(F32).reshape(R, ref.shape[-1])
        shifted = jnp.where(tpos == 0, per_seq_rows([prev[b] for b in range(nb)]), pltpu.roll(x, 1, 0))
        for b in range(nb):
            prev[b] = x[(b + 1) * C - 1:(b + 1) * C, :]
        return x + (shifted - x) * mu

    r = mixed(r_ref, prr, vec(0))
    k = mixed(k_ref, prk, vec(1))
    v = mixed(v_ref, prv, vec(2))
    lo = mixed(lo_ref, prlo, mulo_ref[...])
    yield

    lane_lo = lax.broadcasted_iota(jnp.int32, lo.shape, 1)
    act = jnp.where(lane_lo < 64, jnp.tanh(lo), jnp.where(lane_lo < 128, lo, _sigmoid(lo))).astype(BF16)
    z = vec(3) + _dot(act, lw_ref[0])
    a = _sigmoid(vec(4) + _dot(act, lw_ref[1]))
    g = _dot(act, lw_ref[2])
    logw = -DECAY_SCALE * _sigmoid(z)
    yield

    kk = k * vec(5)
    kk = kk * lax.rsqrt(jnp.maximum(_head_sum(kk * kk, e_ref), 1e-24))
    k = k * (1.0 + (a - 1.0) * vec(6))
    bonus = _head_sum(r * k * vec(7), e_ref)
    yield

    if t_valid is not None:
        valid = chunk * C + tpos < t_valid
        logw = jnp.where(valid, logw, 0.0)
        kk = jnp.where(valid, kk, 0.0)
        k = jnp.where(valid, k, 0.0)
        v = jnp.where(valid, v, 0.0)

    ti = lax.broadcasted_iota(jnp.int32, (R, R), 0)
    tj = lax.broadcasted_iota(jnp.int32, (R, R), 1)
    tril = jnp.where((tj <= ti) & (tj >= ti - (ti & (C - 1))), 1.0, 0.0).astype(BF16)
    lw_hi, lw_lo = _split(logw)
    cum = _dot(jnp.concatenate([tril, tril], axis=1), jnp.concatenate([lw_hi, lw_lo], axis=0))
    e_inc = jnp.exp(cum)
    e_dec = jnp.exp(-cum)
    yield
    a_t = -kk * jnp.exp(cum - logw)
    r_t = r * e_inc
    b_t = kk * a * e_dec
    k_t = k * e_dec
    w_end = [e_inc[(b + 1) * C - 1:(b + 1) * C, :] for b in range(nb)]
    w_end_rows = per_seq_rows(w_end)
    for j, op in enumerate((a_t, r_t, b_t, k_t, b_t * w_end_rows, k_t * w_end_rows, v)):
        ops[j] = op.astype(BF16)
    aux[0] = bonus * v
    aux[1] = g
    for b in range(nb):
        wend[b] = w_end[b]


N_OPS = 7


def _rwkv_mix(buf, s_scr, y_ref, vec_ref, e_ref, *, nb):
    ops, aux, wend = buf
    C = CHUNK
    seq_rows = [slice(b * C, (b + 1) * C) for b in range(nb)]

    def vec(i):
        return vec_ref[i:i + 1, :]

    lane = lax.broadcasted_iota(jnp.int32, (C, PAIR), 1)
    first = lane < HEAD_DIM
    t_row = lax.broadcasted_iota(jnp.int32, (C, PAIR), 0)
    s_lane = lane & (HEAD_DIM - 1)
    strict = s_lane < t_row
    incl = s_lane <= t_row
    bi = lax.broadcasted_iota(jnp.int32, (PAIR, PAIR), 0)
    bj = lax.broadcasted_iota(jnp.int32, (PAIR, PAIR), 1)
    same_head = (bi < HEAD_DIM) == (bj < HEAD_DIM)

    def blockdiag(x):
        z = jnp.zeros_like(x)
        return jnp.concatenate([jnp.where(first, x, z), jnp.where(first, z, x)], axis=0)

    def mm3(x, y):
        xh, xl = _split(x)
        yh, yl = _split(y)
        yh, yl = blockdiag(yh), blockdiag(yl)
        w = jnp.concatenate([jnp.concatenate([yh, yl], axis=1),
                             jnp.concatenate([yh, jnp.zeros_like(yh)], axis=1)], axis=0)
        res = _dot(jnp.concatenate([xh, xl], axis=1), w)
        return res[:, :PAIR] + res[:, PAIR:]

    def mm(x, y, passes):
        if passes == 1:
            return _dot(x.astype(BF16), blockdiag(y.astype(BF16)))
        return mm3(x, y)

    yield
    units = [(b, p) for b in range(nb) for p in range(N_PAIRS)]
    U = range(len(units))
    cut = [(seq_rows[b], slice(p * PAIR, (p + 1) * PAIR)) for b, p in units]
    op = lambda j, u: ops[(j,) + cut[u]]
    lhs = [jnp.concatenate([op(0, u), op(1, u)], axis=0) for u in U]
    rhs = [jnp.concatenate([blockdiag(op(2, u)), blockdiag(op(3, u))], axis=0) for u in U]
    pm = [_dot_nt(lhs[u], rhs[u]) for u in U]
    yield
    m = [jnp.where(strict, pm[u][0:C, 0:PAIR], 0.0) for u in U]
    akrk = [jnp.concatenate([jnp.where(strict, pm[u][0:C, PAIR:], 0.0),
                             jnp.where(incl, pm[u][C:, PAIR:], 0.0)], axis=0).astype(BF16) for u in U]
    rb = [jnp.where(incl, pm[u][C:, 0:PAIR], 0.0).astype(BF16) for u in U]
    yield
    xr = m
    for sq_passes, up_passes in PASSES_INVERSE:
        m = [mm(m[u], m[u], sq_passes) for u in U]
        xr = [xr[u] + m[u] + mm(xr[u], m[u], up_passes) for u in U]
        yield
    s2 = [s_scr[b, p] for b, p in units]
    as_rs = [_dot_nt(lhs[u], s2[u].astype(BF16)) for u in U]
    vb = [op(6, u) for u in U]
    pv = [_dot(akrk[u], blockdiag(vb[u])) for u in U]
    yield
    rhs0 = [as_rs[u][0:C] + pv[u][0:C] for u in U]
    sab = [(rhs0[u] + mm(xr[u], rhs0[u], 1)).astype(BF16) for u in U]
    yield
    y_units = [as_rs[u][C:] + pv[u][C:] + _dot(rb[u], blockdiag(sab[u])) for u in U]
    bhkh = [jnp.concatenate([op(4, u), op(5, u)], axis=0) for u in U]
    for u, (b, p) in enumerate(units):
        upd = _dot_tn(jnp.concatenate([sab[u], vb[u]], axis=0), bhkh[u])
        s_scr[b, p] = s2[u] * wend[b][:, cut[u][1]] + jnp.where(same_head, upd, 0.0)

    yield
    y = jnp.concatenate([jnp.concatenate(y_units[b * N_PAIRS:(b + 1) * N_PAIRS], axis=1) for b in range(nb)], axis=0)
    mean = _head_sum(y, e_ref) * (1.0 / HEAD_DIM)
    yield
    yc = y - mean
    var = _head_sum(yc * yc, e_ref) * (1.0 / HEAD_DIM)
    yield
    yn = yc * lax.rsqrt(var + GN_EPS) * vec(8) + vec(9)
    y_ref[...] = ((yn + aux[0]) * aux[1]).astype(y_ref.dtype).reshape(nb, C, D_MODEL)


def _rwkv(cols3, prev, s0, vec, mulo, lw, e256, t_valid):
    b, tp, _ = cols3.shape
    C = CHUNK
    nb = _pick(b, (RWKV_SEQS_PER_STEP, 1))
    n_chunks = tp // C
    rowspec = lambda w: pl.BlockSpec((nb, 1, w), lambda bi, ci: (bi, 0, 0))
    full = lambda a: pl.BlockSpec(a.shape, lambda bi, ci: (0,) * a.ndim)
    sspec = pl.BlockSpec((nb, N_PAIRS, PAIR, PAIR), lambda bi, ci: (bi, 0, 0, 0))
    operand_bufs = [pltpu.VMEM((N_OPS, nb * C, D_MODEL), BF16), pltpu.VMEM((2, nb * C, D_MODEL), F32),
                    pltpu.VMEM((nb, 1, D_MODEL), F32)]
    return pl.pallas_call(
        functools.partial(_rwkv_kernel, t_valid=t_valid if t_valid < tp else None, nb=nb),
        grid=(b // nb, n_chunks // 2),
        in_specs=[pl.BlockSpec((nb, 2 * C, RWKV_COLS), lambda bi, ci: (bi, ci, 0)),
                  pl.BlockSpec((nb, C, RWKV_COLS), lambda bi, ci: (bi, jnp.minimum(2 * ci + 2, n_chunks - 1), 0)),
                  rowspec(D_MODEL), rowspec(D_MODEL), rowspec(D_MODEL), rowspec(LORA_COLS),
                  sspec, full(vec), full(mulo), full(lw), full(e256)],
        out_specs=[pl.BlockSpec((nb, 2 * C, D_MODEL), lambda bi, ci: (bi, ci, 0)), sspec],
        out_shape=[jax.ShapeDtypeStruct((b, tp, D_MODEL), BF16),
                   jax.ShapeDtypeStruct(s0.shape, F32)],
        scratch_shapes=[pltpu.VMEM((nb, N_PAIRS, PAIR, PAIR), F32),
                        pltpu.VMEM((nb, 1, D_MODEL), F32), pltpu.VMEM((nb, 1, D_MODEL), F32),
                        pltpu.VMEM((nb, 1, D_MODEL), F32), pltpu.VMEM((nb, 1, LORA_COLS), F32)]
                       + operand_bufs + operand_bufs,
        compiler_params=_cparams(("arbitrary", "arbitrary")),
        name="rwkv7",
    )(cols3, cols3, *prev, s0, vec, mulo, lw, e256)


def _sgu_kernel(u_ref, v_ref, ln_ref, ws_ref, bs_ref, y_ref, *maybe_v_out, lc):
    u = _gelu(u_ref[...]).astype(F32)
    v = _gelu(v_ref[...]).astype(F32)
    mu = jnp.mean(v, axis=-1, keepdims=True)
    vc = v - mu
    var = jnp.mean(vc * vc, axis=-1, keepdims=True)
    v = vc * lax.rsqrt(var + LN_EPS) * ln_ref[0:1, :] + ln_ref[1:2, :]
    if maybe_v_out:
        maybe_v_out[0][...] = v
    vb = v.astype(BF16)
    n_chunks = v.shape[0] // lc
    ri = lax.broadcasted_iota(jnp.int32, (lc, lc), 0)
    ci = lax.broadcasted_iota(jnp.int32, (lc, lc), 1)
    bias = bs_ref[...]
    outs = []
    for gi in range(GMLP_GROUPS):
        gs = slice(gi * GMLP_CHUNK, (gi + 1) * GMLP_CHUNK)
        w = jnp.where(ci <= ri, ws_ref[gi], 0.0).astype(BF16)
        rhs = jnp.concatenate([vb[n * lc:(n + 1) * lc, gs] for n in range(n_chunks)], axis=1)
        sv = _dot(w, rhs)
        sv = jnp.concatenate([sv[:, n * GMLP_CHUNK:(n + 1) * GMLP_CHUNK] for n in range(n_chunks)], axis=0)
        outs.append(sv + bias[:, gs])
    sv = jnp.concatenate(outs, axis=1)
    y_ref[...] = (u * sv).astype(y_ref.dtype)


def _sgu(uv3, ln, ws, bs_full, lc, n_chunks, want_v):
    b, t, _ = uv3.shape
    tt = lc * n_chunks
    col = lambda j: pl.BlockSpec((None, tt, D_MODEL), lambda bi, ti: (bi, ti, j))
    full = lambda a: pl.BlockSpec(a.shape, lambda bi, ti: (0,) * a.ndim)
    ospec = pl.BlockSpec((None, tt, D_MODEL), lambda bi, ti: (bi, ti, 0))
    out_specs = [ospec]
    out_shape = [jax.ShapeDtypeStruct((b, t, D_MODEL), BF16)]
    if want_v:
        out_specs.append(ospec)
        out_shape.append(jax.ShapeDtypeStruct((b, t, D_MODEL), F32))
    return pl.pallas_call(
        functools.partial(_sgu_kernel, lc=lc),
        grid=(b, t // tt),
        in_specs=[col(0), col(1), full(ln), full(ws), full(bs_full)],
        out_specs=out_specs,
        out_shape=out_shape,
        compiler_params=_cparams(("arbitrary", "arbitrary")),
        name="sgu",
    )(uv3, uv3, ln, ws, bs_full)


def _xattn_kernel(q_ref, k_ref, v_ref, o_ref):
    outs = []
    for h in range(X_HEADS):
        hs = slice(h * X_HEAD_DIM, (h + 1) * X_HEAD_DIM)
        s = _dot_nt(q_ref[:, hs], k_ref[:, hs]) * (X_HEAD_DIM ** -0.5)
        s = s - jnp.max(s, axis=-1, keepdims=True)
        e = jnp.exp(s)
        p = e * (1.0 / jnp.sum(e, axis=-1, keepdims=True))
        outs.append(_dot(p.astype(BF16), v_ref[:, hs]))
    o_ref[...] = jnp.concatenate(outs, axis=1).astype(o_ref.dtype)


def _xattn(gq3, mem_k, mem_v, tq):
    b, t, _ = gq3.shape
    mspec = pl.BlockSpec((None, N_MEM, D_MODEL), lambda bi, ti: (bi, 0, 0))
    return pl.pallas_call(
        _xattn_kernel,
        grid=(b, t // tq),
        in_specs=[pl.BlockSpec((None, tq, D_MODEL), lambda bi, ti: (bi, ti, GQ_Q_BLOCK)), mspec, mspec],
        out_specs=pl.BlockSpec((None, tq, D_MODEL), lambda bi, ti: (bi, ti, 0)),
        out_shape=jax.ShapeDtypeStruct((b, t, D_MODEL), BF16),
        compiler_params=_cparams(("arbitrary", "arbitrary")),
        name="xattn",
    )(gq3, mem_k, mem_v)


def _merge_kernel(x_ref, gate_ref, ya_ref, yb_ref, yc_ref, wb_ref, wo_ref, o_ref):
    gate = _sigmoid(gate_ref[...].astype(F32))
    merged = None
    for bi, y_ref in enumerate((ya_ref, yb_ref, yc_ref)):
        term = gate[:, bi * D_MODEL:(bi + 1) * D_MODEL] * _dot(y_ref[...], wb_ref[bi])
        merged = term if merged is None else merged + term
    o_ref[...] = x_ref[...] + _dot(merged.astype(BF16), wo_ref[...])


def _merge(x, gq, ya, yb, yc, wb, wo, tm):
    t, d = x.shape
    tok = lambda w: pl.BlockSpec((tm, w), lambda i: (i, 0))
    return pl.pallas_call(
        _merge_kernel,
        grid=(t // tm,),
        in_specs=[tok(d), tok(3 * d), tok(d), tok(d), tok(d),
                  pl.BlockSpec(wb.shape, lambda i: (0, 0, 0)), pl.BlockSpec(wo.shape, lambda i: (0, 0))],
        out_specs=tok(d),
        out_shape=jax.ShapeDtypeStruct((t, d), F32),
        compiler_params=_cparams(("arbitrary",)),
        name="merge",
    )(x, gq, ya, yb, yc, wb, wo)


def _ffn_kernel(x_ref, g_ref, wu_ref, wd_ref, gf_ref, o_ref, h_scr, *, final_norm):
    f = pl.program_id(1)

    @pl.when(f == 0)
    def _init():
        x = x_ref[...]
        h_scr[...] = _rmsnorm(x, g_ref[...]).astype(BF16)
        o_ref[...] = x

    up = _dot(h_scr[...], wu_ref[...])
    act = jnp.square(jnp.maximum(up, 0.0)).astype(BF16)
    o_ref[...] += _dot(act, wd_ref[...])

    if final_norm:
        @pl.when(f == pl.num_programs(1) - 1)
        def _final():
            o_ref[...] = _rmsnorm(o_ref[...], gf_ref[...])


def _ffn(x, g, wu, wd, g_final, final_norm, tm, tf):
    t, d = x.shape
    return pl.pallas_call(
        functools.partial(_ffn_kernel, final_norm=final_norm),
        grid=(t // tm, D_FF // tf),
        in_specs=[pl.BlockSpec((tm, d), lambda i, f: (i, 0)),
                  pl.BlockSpec((1, d), lambda i, f: (0, 0)),
                  pl.BlockSpec((d, tf), lambda i, f: (0, f)),
                  pl.BlockSpec((tf, d), lambda i, f: (f, 0)),
                  pl.BlockSpec((1, d), lambda i, f: (0, 0))],
        out_specs=pl.BlockSpec((tm, d), lambda i, f: (i, 0)),
        out_shape=jax.ShapeDtypeStruct((t, d), F32),
        scratch_shapes=[pltpu.VMEM((tm, d), BF16)],
        compiler_params=_cparams(("arbitrary", "arbitrary")),
        name="ffn",
    )(x, g, wu, wd, g_final)


def _state_to_pairs(s):
    b = s.shape[0]
    s = s.reshape(b, N_PAIRS, 2, HEAD_DIM, HEAD_DIM)
    z = jnp.zeros_like(s[:, :, 0])
    top = jnp.concatenate([s[:, :, 0], z], axis=-1)
    bot = jnp.concatenate([z, s[:, :, 1]], axis=-1)
    return jnp.concatenate([top, bot], axis=-2)


def _pairs_to_state(s2):
    b = s2.shape[0]
    s0 = s2[:, :, :HEAD_DIM, :HEAD_DIM]
    s1 = s2[:, :, HEAD_DIM:, HEAD_DIM:]
    return jnp.stack([s0, s1], axis=2).reshape(b, N_HEADS, HEAD_DIM, HEAD_DIM)


def _pick(t, candidates):
    for c in candidates:
        if t % c == 0:
            return c
    return t


def _group_layer(x, b, t, mem_k, mem_v, prev_row, s0_pairs, lp):
    n_tok = b * t
    tm = _pick(n_tok, (512, 256))
    g_mix = lp["norm_mix_g"]
    cols3 = _proj(x, g_mix, lp["w_in_rwkv"], BF16, tm, RWKV_COLS).reshape(b, t, RWKV_COLS)
    uv3 = _proj(x, g_mix, lp["w_in_gmlp"], BF16, tm, 2 * D_MODEL).reshape(b, t, 2 * D_MODEL)
    gq = _proj(x, g_mix, lp["w_in_gq"], BF16, tm, 4 * D_MODEL)
    gq3 = gq.reshape(b, t, 4 * D_MODEL)
    new_row = cols3[:, t - 1:t, :].astype(F32)

    tp = -(-t // (2 * CHUNK)) * 2 * CHUNK
    cols3p = cols3 if tp == t else jnp.pad(cols3, ((0, 0), (0, tp - t), (0, 0)))
    prev = (prev_row[..., :D_MODEL], prev_row[..., D_MODEL:2 * D_MODEL], prev_row[..., 2 * D_MODEL:3 * D_MODEL],
            prev_row[..., 3 * D_MODEL:])
    y_a, s_new = _rwkv(cols3p, prev, s0_pairs, lp["rwkv_vec"], lp["rwkv_mu_lo"], lp["rwkv_lw"], lp["e256"], t)
    y_a = y_a[:, :t].reshape(n_tok, D_MODEL)

    lc = min(t, GMLP_CHUNK)
    n_chunks = _pick(t // lc, (8, 4, 2, 1))
    ws = lp["sgu_w_s"][:, :lc, :lc]
    bs_full = jnp.tile(jnp.repeat(lp["sgu_b_s"][:, :lc].T, GMLP_CHUNK, axis=1), (n_chunks, 1))
    sgu_out = _sgu(uv3, lp["sgu_ln"], ws, bs_full, lc, n_chunks, want_v=t < GMLP_CHUNK)
    y_b = sgu_out[0].reshape(n_tok, D_MODEL)
    v_rows = sgu_out[1] if t < GMLP_CHUNK else None

    y_c = _xattn(gq3, mem_k, mem_v, _pick(t, (1024, 512, 256, 128))).reshape(n_tok, D_MODEL)

    x = _merge(x, gq, y_a, y_b, y_c, lp["w_branch"], lp["w_out"], _pick(n_tok, (512, 256)))
    x = _ffn(x, lp["norm_ffn_g"], lp["w_ffn_up"], lp["w_ffn_down"], lp["norm_final_g"], lp["is_last"],
             _pick(n_tok, (1024, 512, 256)), 1024)
    return x, new_row, s_new, v_rows


def kernel(x_prompt, x_sample, cache_mem_k, cache_mem_v, state_wkv, state_shift, mem_prompt, norm_mix_g, norm_mem_g, norm_ffn_g, norm_final_g, w_in, w_mem_kv, rwkv_mu, rwkv_w0, rwkv_w2, rwkv_a0, rwkv_a2, rwkv_g2, rwkv_k_k, rwkv_k_a, rwkv_r_k, rwkv_lnx_g, rwkv_lnx_b, sgu_ln_g, sgu_ln_b, sgu_w_s, sgu_b_s, w_branch, w_out, w_ffn_up, w_ffn_down):
    depth = w_in.shape[0]
    bp, tp, _ = x_prompt.shape
    bs, ts, _ = x_sample.shape

    w_in_rwkv = w_in[..., :RWKV_COLS].astype(BF16)
    w_in_gmlp = w_in[..., RWKV_COLS:COL_Q_START].astype(BF16)
    w_in_gq = jnp.concatenate([w_in[..., COL_Q_START + D_MODEL:], w_in[..., COL_Q_START:COL_Q_START + D_MODEL]],
                              axis=-1).astype(BF16)
    w_kv_b = w_mem_kv.astype(BF16)
    w_branch_b = w_branch.astype(BF16)
    w_out_b = w_out.astype(BF16)
    w_up_b = w_ffn_up.astype(BF16)
    w_down_b = w_ffn_down.astype(BF16)
    zeros = lambda n: jnp.zeros((depth, n, D_MODEL), F32)
    lw = jnp.stack([jnp.concatenate([rwkv_w2, zeros(192)], axis=1),
                    jnp.concatenate([zeros(64), rwkv_a2, zeros(128)], axis=1),
                    jnp.concatenate([zeros(128), rwkv_g2], axis=1)], axis=1).astype(BF16)
    rwkv_vec = jnp.stack([rwkv_mu[:, :D_MODEL], rwkv_mu[:, D_MODEL:2 * D_MODEL], rwkv_mu[:, 2 * D_MODEL:3 * D_MODEL],
                          rwkv_w0, rwkv_a0, rwkv_k_k, rwkv_k_a, rwkv_r_k.reshape(depth, D_MODEL),
                          rwkv_lnx_g, rwkv_lnx_b] + [jnp.zeros((depth, D_MODEL), F32)] * 6, axis=1)
    sgu_ln = jnp.stack([sgu_ln_g, sgu_ln_b] + [jnp.zeros((depth, D_MODEL), F32)] * 6, axis=1)
    hq = 4 * HEAD_DIM
    e256 = (jnp.arange(hq)[:, None] // HEAD_DIM == jnp.arange(hq)[None, :] // HEAD_DIM).astype(BF16)

    xp = x_prompt.reshape(bp * tp, D_MODEL)
    xs = x_sample.reshape(bs * ts, D_MODEL)
    mem_flat = mem_prompt.reshape(bp * N_MEM, D_MODEL)
    prompt_row0 = jnp.zeros((bp, 1, RWKV_COLS), F32)
    prompt_s0 = jnp.zeros((bp, N_PAIRS, PAIR, PAIR), F32)

    mk_p, mv_p, wkv_p, row_p, wkv_s, row_s, v_s = [], [], [], [], [], [], []
    for l in range(depth):
        lp = {
            "norm_mix_g": norm_mix_g[l][None],
            "w_in_rwkv": w_in_rwkv[l], "w_in_gmlp": w_in_gmlp[l], "w_in_gq": w_in_gq[l],
            "rwkv_vec": rwkv_vec[l], "rwkv_mu_lo": rwkv_mu[l][None, 3 * D_MODEL:], "rwkv_lw": lw[l], "e256": e256,
            "sgu_ln": sgu_ln[l], "sgu_w_s": sgu_w_s[l], "sgu_b_s": sgu_b_s[l],
            "w_branch": w_branch_b[l], "w_out": w_out_b[l],
            "norm_ffn_g": norm_ffn_g[l][None], "w_ffn_up": w_up_b[l], "w_ffn_down": w_down_b[l],
            "norm_final_g": norm_final_g[None], "is_last": l == depth - 1,
        }
        kv = _proj(mem_flat, norm_mem_g[l][None], w_kv_b[l], F32, _pick(bp * N_MEM, (512, 256)), 2 * D_MODEL)
        mem_k = kv[:, :D_MODEL].reshape(bp, N_MEM, D_MODEL)
        mem_v = kv[:, D_MODEL:].reshape(bp, N_MEM, D_MODEL)
        xp, r_p, s_p, _ = _group_layer(xp, bp, tp, mem_k.astype(BF16), mem_v.astype(BF16), prompt_row0, prompt_s0, lp)
        mk_p.append(mem_k.reshape(bp, N_MEM, X_HEADS, X_HEAD_DIM))
        mv_p.append(mem_v.reshape(bp, N_MEM, X_HEADS, X_HEAD_DIM))
        wkv_p.append(_pairs_to_state(s_p))
        row_p.append(r_p)

        ck = cache_mem_k[l].reshape(bs, N_MEM, D_MODEL).astype(BF16)
        cv = cache_mem_v[l].reshape(bs, N_MEM, D_MODEL).astype(BF16)
        xs, r_s, s_s, vr = _group_layer(xs, bs, ts, ck, cv, _permute_rwkv_prev(state_shift[l]),
                                        _state_to_pairs(state_wkv[l]), lp)
        wkv_s.append(_pairs_to_state(s_s))
        row_s.append(r_s)
        v_s.append(vr)

    y_prompt = xp.reshape(bp, tp, D_MODEL)
    y_sample = xs.reshape(bs, ts, D_MODEL)
    return (y_prompt, y_sample, jnp.stack(mk_p), jnp.stack(mv_p), jnp.stack(wkv_p), jnp.stack(row_p),
            jnp.stack(wkv_s), jnp.stack(row_s), jnp.stack(v_s))


def _permute_rwkv_prev(row):
    return row
```

```python
import functools
import math

import jax
import jax.numpy as jnp
from jax import lax
from jax.experimental import pallas as pl
from jax.experimental.pallas import tpu as pltpu

F32 = jnp.float32
BF16 = jnp.bfloat16

D_MODEL = 1024
HEAD_DIM = 64
N_HEADS = D_MODEL // HEAD_DIM
PAIR = 2 * HEAD_DIM
N_PAIRS = D_MODEL // PAIR
LORA_COLS = 256
RWKV_COLS = 3 * D_MODEL + LORA_COLS
GMLP_CHUNK = 128
GMLP_GROUPS = 8
N_MEM = 256
X_HEADS = 4
X_HEAD_DIM = D_MODEL // X_HEADS
D_FF = 4 * D_MODEL
COL_Q_START = RWKV_COLS + 2 * D_MODEL
GQ_Q_BLOCK = 3
RMS_EPS = 1e-6
LN_EPS = 1e-5
GN_EPS = 64e-5
CHUNK = 64
DECAY_SCALE = math.exp(-0.5)
VMEM_LIMIT = 56 * 1024 * 1024
PASSES_INVERSE = ((3, 3), (3, 3), (3, 3), (1, 1), (1, 1))
RWKV_SEQS_PER_STEP = 2


def _cparams(sem):
    return pltpu.CompilerParams(dimension_semantics=sem, vmem_limit_bytes=VMEM_LIMIT)


def _dot(a, b):
    return jnp.dot(a, b, preferred_element_type=F32)


def _dot_nt(a, b):
    return lax.dot_general(a, b, (((1,), (1,)), ((), ())), preferred_element_type=F32)


def _dot_tn(a, b):
    return lax.dot_general(a, b, (((0,), (0,)), ((), ())), preferred_element_type=F32)


def _split(x):
    hi = x.astype(BF16)
    lo = (x - hi.astype(F32)).astype(BF16)
    return hi, lo


def _rmsnorm(x, g):
    return x * lax.rsqrt(jnp.mean(x * x, axis=-1, keepdims=True) + RMS_EPS) * g


def _sigmoid(x):
    return 1.0 / (1.0 + jnp.exp(-x))


def _gelu(x):
    return 0.5 * x * (1.0 + jnp.tanh(math.sqrt(2.0 / math.pi) * (x + 0.044715 * (x * x * x))))


def _proj_kernel(x_ref, g_ref, w_ref, o_ref):
    h = _rmsnorm(x_ref[...], g_ref[...]).astype(BF16)
    o_ref[...] = _dot(h, w_ref[...]).astype(o_ref.dtype)


def _proj(x, g, w, out_dtype, tm, tn):
    t, d = x.shape
    nc = w.shape[1]
    return pl.pallas_call(
        _proj_kernel,
        grid=(nc // tn, t // tm),
        in_specs=[pl.BlockSpec((tm, d), lambda j, i: (i, 0)),
                  pl.BlockSpec((1, d), lambda j, i: (0, 0)),
                  pl.BlockSpec((d, tn), lambda j, i: (0, j))],
        out_specs=pl.BlockSpec((tm, tn), lambda j, i: (i, j)),
        out_shape=jax.ShapeDtypeStruct((t, nc), out_dtype),
        compiler_params=_cparams(("arbitrary", "arbitrary")),
        name="proj",
    )(x, g, w)


def _head_sum(x, e_ref):
    c = x.shape[0]
    q = 4 * HEAD_DIM
    xs = jnp.concatenate([x[:, i * q:(i + 1) * q] for i in range(D_MODEL // q)], axis=0)
    r = _dot(xs.astype(BF16), e_ref[...])
    return jnp.concatenate([r[i * c:(i + 1) * c] for i in range(D_MODEL // q)], axis=1)


def _rwkv_kernel(cols_ref, next_ref, pr_ref, pk_ref, pv_ref, plo_ref, s0_ref,
                 vec_ref, mulo_ref, lw_ref, e_ref,
                 y_ref, sout_ref,
                 s_scr, prr, prk, prv, prlo, ops_x, aux_x, wend_x, ops_y, aux_y, wend_y, *, t_valid, nb):
    i = pl.program_id(1)
    C = CHUNK
    prev = (prr, prk, prv, prlo)
    buf_x, buf_y = (ops_x, aux_x, wend_x), (ops_y, aux_y, wend_y)

    def col_views(ref, rows):
        return ([ref.at[:, rows, j * D_MODEL:(j + 1) * D_MODEL] for j in range(3)]
                + [ref.at[:, rows, 3 * D_MODEL:RWKV_COLS]])

    def prep(ref, rows, chunk, buf):
        return _rwkv_prep(col_views(ref, rows), prev, vec_ref, mulo_ref, lw_ref, e_ref, buf,
                          t_valid=t_valid, chunk=chunk, nb=nb)

    def mix(buf, rows):
        return _rwkv_mix(buf, s_scr, y_ref.at[:, rows, :], vec_ref, e_ref, nb=nb)

    def emit(main, side):
        for _ in main:
            next(side, None)
        for _ in side:
            pass

    @pl.when(i == 0)
    def _init():
        s_scr[...] = s0_ref[...]
        prr[...] = pr_ref[...]
        prk[...] = pk_ref[...]
        prv[...] = pv_ref[...]
        prlo[...] = plo_ref[...]
        emit(prep(cols_ref, slice(0, C), 0, buf_x), iter(()))

    emit(mix(buf_x, slice(0, C)), prep(cols_ref, slice(C, 2 * C), 2 * i + 1, buf_y))
    emit(mix(buf_y, slice(C, 2 * C)), prep(next_ref, slice(0, C), 2 * i + 2, buf_x))

    @pl.when(i == pl.num_programs(1) - 1)
    def _fin():
        sout_ref[...] = s_scr[...]


def _rwkv_prep(col_refs, prev, vec_ref, mulo_ref, lw_ref, e_ref, buf, *, t_valid, chunk, nb):
    r_ref, k_ref, v_ref, lo_ref = col_refs
    prr, prk, prv, prlo = prev
    ops, aux, wend = buf
    C = CHUNK
    R = nb * C

    def vec(i):
        return vec_ref[i:i + 1, :]

    row = lax.broadcasted_iota(jnp.int32, (R, 1), 0)
    tpos = row & (C - 1)
    seq_rows = [slice(b * C, (b + 1) * C) for b in range(nb)]

    def per_seq_rows(rows):
        out = rows[nb - 1]
        for b in reversed(range(nb - 1)):
            out = jnp.where(row < (b + 1) * C, rows[b], out)
        return out

    def mixed(ref, prev, mu):
        x = ref[...].astype(F32).reshape(R, ref.shape[-1])
        shifted = jnp.where(tpos == 0, per_seq_rows([prev[b] for b in range(nb)]), pltpu.roll(x, 1, 0))
        for b in range(nb):
            prev[b] = x[(b + 1) * C - 1:(b + 1) * C, :]
        return x + (shifted - x) * mu

    r = mixed(r_ref, prr, vec(0))
    k = mixed(k_ref, prk, vec(1))
    v = mixed(v_ref, prv, vec(2))
    lo = mixed(lo_ref, prlo, mulo_ref[...])
    yield

    lane_lo = lax.broadcasted_iota(jnp.int32, lo.shape, 1)
    act = jnp.where(lane_lo < 64, jnp.tanh(lo), jnp.where(lane_lo < 128, lo, _sigmoid(lo))).astype(BF16)
    z = vec(3) + _dot(act, lw_ref[0])
    a = _sigmoid(vec(4) + _dot(act, lw_ref[1]))
    g = _dot(act, lw_ref[2])
    logw = -DECAY_SCALE * _sigmoid(z)
    yield

    kk = k * vec(5)
    kk = kk * lax.rsqrt(jnp.maximum(_head_sum(kk * kk, e_ref), 1e-24))
    k = k * (1.0 + (a - 1.0) * vec(6))
    bonus = _head_sum(r * k * vec(7), e_ref)
    yield

    if t_valid is not None:
        valid = chunk * C + tpos < t_valid
        logw = jnp.where(valid, logw, 0.0)
        kk = jnp.where(valid, kk, 0.0)
        k = jnp.where(valid, k, 0.0)
        v = jnp.where(valid, v, 0.0)

    ti = lax.broadcasted_iota(jnp.int32, (R, R), 0)
    tj = lax.broadcasted_iota(jnp.int32, (R, R), 1)
    tril = jnp.where((tj <= ti) & (tj >= ti - (ti & (C - 1))), 1.0, 0.0).astype(BF16)
    lw_hi, lw_lo = _split(logw)
    cum = _dot(jnp.concatenate([tril, tril], axis=1), jnp.concatenate([lw_hi, lw_lo], axis=0))
    e_inc = jnp.exp(cum)
    e_dec = jnp.exp(-cum)
    yield
    a_t = -kk * jnp.exp(cum - logw)
    r_t = r * e_inc
    b_t = kk * a * e_dec
    k_t = k * e_dec
    w_end = [e_inc[(b + 1) * C - 1:(b + 1) * C, :] for b in range(nb)]
    w_end_rows = per_seq_rows(w_end)
    for j, op in enumerate((a_t, r_t, b_t, k_t, b_t * w_end_rows, k_t * w_end_rows, v)):
        ops[j] = op.astype(BF16)
    aux[0] = bonus * v
    aux[1] = g
    for b in range(nb):
        wend[b] = w_end[b]


N_OPS = 7


def _rwkv_mix(buf, s_scr, y_ref, vec_ref, e_ref, *, nb):
    ops, aux, wend = buf
    C = CHUNK
    seq_rows = [slice(b * C, (b + 1) * C) for b in range(nb)]

    def vec(i):
        return vec_ref[i:i + 1, :]

    lane = lax.broadcasted_iota(jnp.int32, (C, PAIR), 1)
    first = lane < HEAD_DIM
    t_row = lax.broadcasted_iota(jnp.int32, (C, PAIR), 0)
    s_lane = lane & (HEAD_DIM - 1)
    strict = s_lane < t_row
    incl = s_lane <= t_row
    bi = lax.broadcasted_iota(jnp.int32, (PAIR, PAIR), 0)
    bj = lax.broadcasted_iota(jnp.int32, (PAIR, PAIR), 1)
    same_head = (bi < HEAD_DIM) == (bj < HEAD_DIM)

    def blockdiag(x):
        z = jnp.zeros_like(x)
        return jnp.concatenate([jnp.where(first, x, z), jnp.where(first, z, x)], axis=0)

    def mm3(x, y):
        xh, xl = _split(x)
        yh, yl = _split(y)
        yh, yl = blockdiag(yh), blockdiag(yl)
        w = jnp.concatenate([jnp.concatenate([yh, yl], axis=1),
                             jnp.concatenate([yh, jnp.zeros_like(yh)], axis=1)], axis=0)
        res = _dot(jnp.concatenate([xh, xl], axis=1), w)
        return res[:, :PAIR] + res[:, PAIR:]

    def mm(x, y, passes):
        if passes == 1:
            return _dot(x.astype(BF16), blockdiag(y.astype(BF16)))
        return mm3(x, y)

    yield
    units = [(b, p) for b in range(nb) for p in range(N_PAIRS)]
    U = range(len(units))
    cut = [(seq_rows[b], slice(p * PAIR, (p + 1) * PAIR)) for b, p in units]
    op = lambda j, u: ops[(j,) + cut[u]]
    lhs = [jnp.concatenate([op(0, u), op(1, u)], axis=0) for u in U]
    rhs = [jnp.concatenate([blockdiag(op(2, u)), blockdiag(op(3, u))], axis=0) for u in U]
    pm = [_dot_nt(lhs[u], rhs[u]) for u in U]
    yield
    m = [jnp.where(strict, pm[u][0:C, 0:PAIR], 0.0) for u in U]
    akrk = [jnp.concatenate([jnp.where(strict, pm[u][0:C, PAIR:], 0.0),
                             jnp.where(incl, pm[u][C:, PAIR:], 0.0)], axis=0).astype(BF16) for u in U]
    rb = [jnp.where(incl, pm[u][C:, 0:PAIR], 0.0).astype(BF16) for u in U]
    yield
    xr = m
    for sq_passes, up_passes in PASSES_INVERSE:
        m = [mm(m[u], m[u], sq_passes) for u in U]
        xr = [xr[u] + m[u] + mm(xr[u], m[u], up_passes) for u in U]
        yield
    s2 = [s_scr[b, p] for b, p in units]
    as_rs = [_dot_nt(lhs[u], s2[u].astype(BF16)) for u in U]
    vb = [op(6, u) for u in U]
    pv = [_dot(akrk[u], blockdiag(vb[u])) for u in U]
    yield
    rhs0 = [as_rs[u][0:C] + pv[u][0:C] for u in U]
    sab = [(rhs0[u] + mm(xr[u], rhs0[u], 1)).astype(BF16) for u in U]
    yield
    y_units = [as_rs[u][C:] + pv[u][C:] + _dot(rb[u], blockdiag(sab[u])) for u in U]
    bhkh = [jnp.concatenate([op(4, u), op(5, u)], axis=0) for u in U]
    for u, (b, p) in enumerate(units):
        upd = _dot_tn(jnp.concatenate([sab[u], vb[u]], axis=0), bhkh[u])
        s_scr[b, p] = s2[u] * wend[b][:, cut[u][1]] + jnp.where(same_head, upd, 0.0)

    yield
    y = jnp.concatenate([jnp.concatenate(y_units[b * N_PAIRS:(b + 1) * N_PAIRS], axis=1) for b in range(nb)], axis=0)
    mean = _head_sum(y, e_ref) * (1.0 / HEAD_DIM)
    yield
    yc = y - mean
    var = _head_sum(yc * yc, e_ref) * (1.0 / HEAD_DIM)
    yield
    yn = yc * lax.rsqrt(var + GN_EPS) * vec(8) + vec(9)
    y_ref[...] = ((yn + aux[0]) * aux[1]).astype(y_ref.dtype).reshape(nb, C, D_MODEL)


def _rwkv(cols3, prev, s0, vec, mulo, lw, e256, t_valid):
    b, tp, _ = cols3.shape
    C = CHUNK
    nb = _pick(b, (RWKV_SEQS_PER_STEP, 1))
    n_chunks = tp // C
    rowspec = lambda w: pl.BlockSpec((nb, 1, w), lambda bi, ci: (bi, 0, 0))
    full = lambda a: pl.BlockSpec(a.shape, lambda bi, ci: (0,) * a.ndim)
    sspec = pl.BlockSpec((nb, N_PAIRS, PAIR, PAIR), lambda bi, ci: (bi, 0, 0, 0))
    operand_bufs = [pltpu.VMEM((N_OPS, nb * C, D_MODEL), BF16), pltpu.VMEM((2, nb * C, D_MODEL), F32),
                    pltpu.VMEM((nb, 1, D_MODEL), F32)]
    return pl.pallas_call(
        functools.partial(_rwkv_kernel, t_valid=t_valid if t_valid < tp else None, nb=nb),
        grid=(b // nb, n_chunks // 2),
        in_specs=[pl.BlockSpec((nb, 2 * C, RWKV_COLS), lambda bi, ci: (bi, ci, 0)),
                  pl.BlockSpec((nb, C, RWKV_COLS), lambda bi, ci: (bi, jnp.minimum(2 * ci + 2, n_chunks - 1), 0)),
                  rowspec(D_MODEL), rowspec(D_MODEL), rowspec(D_MODEL), rowspec(LORA_COLS),
                  sspec, full(vec), full(mulo), full(lw), full(e256)],
        out_specs=[pl.BlockSpec((nb, 2 * C, D_MODEL), lambda bi, ci: (bi, ci, 0)), sspec],
        out_shape=[jax.ShapeDtypeStruct((b, tp, D_MODEL), BF16),
                   jax.ShapeDtypeStruct(s0.shape, F32)],
        scratch_shapes=[pltpu.VMEM((nb, N_PAIRS, PAIR, PAIR), F32),
                        pltpu.VMEM((nb, 1, D_MODEL), F32), pltpu.VMEM((nb, 1, D_MODEL), F32),
                        pltpu.VMEM((nb, 1, D_MODEL), F32), pltpu.VMEM((nb, 1, LORA_COLS), F32)]
                       + operand_bufs + operand_bufs,
        compiler_params=_cparams(("arbitrary", "arbitrary")),
        name="rwkv7",
    )(cols3, cols3, *prev, s0, vec, mulo, lw, e256)


def _sgu_kernel(u_ref, v_ref, ln_ref, ws_ref, bs_ref, y_ref, *maybe_v_out, lc):
    u = _gelu(u_ref[...]).astype(F32)
    v = _gelu(v_ref[...]).astype(F32)
    mu = jnp.mean(v, axis=-1, keepdims=True)
    vc = v - mu
    var = jnp.mean(vc * vc, axis=-1, keepdims=True)
    v = vc * lax.rsqrt(var + LN_EPS) * ln_ref[0:1, :] + ln_ref[1:2, :]
    if maybe_v_out:
        maybe_v_out[0][...] = v
    vb = v.astype(BF16)
    n_chunks = v.shape[0] // lc
    ri = lax.broadcasted_iota(jnp.int32, (lc, lc), 0)
    ci = lax.broadcasted_iota(jnp.int32, (lc, lc), 1)
    bias = bs_ref[...]
    outs = []
    for gi in range(GMLP_GROUPS):
        gs = slice(gi * GMLP_CHUNK, (gi + 1) * GMLP_CHUNK)
        w = jnp.where(ci <= ri, ws_ref[gi], 0.0).astype(BF16)
        rhs = jnp.concatenate([vb[n * lc:(n + 1) * lc, gs] for n in range(n_chunks)], axis=1)
        sv = _dot(w, rhs)
        sv = jnp.concatenate([sv[:, n * GMLP_CHUNK:(n + 1) * GMLP_CHUNK] for n in range(n_chunks)], axis=0)
        outs.append(sv + bias[:, gs])
    sv = jnp.concatenate(outs, axis=1)
    y_ref[...] = (u * sv).astype(y_ref.dtype)


def _sgu(uv3, ln, ws, bs_full, lc, n_chunks, want_v):
    b, t, _ = uv3.shape
    tt = lc * n_chunks
    col = lambda j: pl.BlockSpec((None, tt, D_MODEL), lambda bi, ti: (bi, ti, j))
    full = lambda a: pl.BlockSpec(a.shape, lambda bi, ti: (0,) * a.ndim)
    ospec = pl.BlockSpec((None, tt, D_MODEL), lambda bi, ti: (bi, ti, 0))
    out_specs = [ospec]
    out_shape = [jax.ShapeDtypeStruct((b, t, D_MODEL), BF16)]
    if want_v:
        out_specs.append(ospec)
        out_shape.append(jax.ShapeDtypeStruct((b, t, D_MODEL), F32))
    return pl.pallas_call(
        functools.partial(_sgu_kernel, lc=lc),
        grid=(b, t // tt),
        in_specs=[col(0), col(1), full(ln), full(ws), full(bs_full)],
        out_specs=out_specs,
        out_shape=out_shape,
        compiler_params=_cparams(("arbitrary", "arbitrary")),
        name="sgu",
    )(uv3, uv3, ln, ws, bs_full)


def _xattn_kernel(q_ref, k_ref, v_ref, o_ref):
    outs = []
    for h in range(X_HEADS):
        hs = slice(h * X_HEAD_DIM, (h + 1) * X_HEAD_DIM)
        s = _dot_nt(q_ref[:, hs], k_ref[:, hs]) * (X_HEAD_DIM ** -0.5)
        s = s - jnp.max(s, axis=-1, keepdims=True)
        e = jnp.exp(s)
        p = e * (1.0 / jnp.sum(e, axis=-1, keepdims=True))
        outs.append(_dot(p.astype(BF16), v_ref[:, hs]))
    o_ref[...] = jnp.concatenate(outs, axis=1).astype(o_ref.dtype)


def _xattn(gq3, mem_k, mem_v, tq):
    b, t, _ = gq3.shape
    mspec = pl.BlockSpec((None, N_MEM, D_MODEL), lambda bi, ti: (bi, 0, 0))
    return pl.pallas_call(
        _xattn_kernel,
        grid=(b, t // tq),
        in_specs=[pl.BlockSpec((None, tq, D_MODEL), lambda bi, ti: (bi, ti, GQ_Q_BLOCK)), mspec, mspec],
        out_specs=pl.BlockSpec((None, tq, D_MODEL), lambda bi, ti: (bi, ti, 0)),
        out_shape=jax.ShapeDtypeStruct((b, t, D_MODEL), BF16),
        compiler_params=_cparams(("arbitrary", "arbitrary")),
        name="xattn",
    )(gq3, mem_k, mem_v)


def _merge_kernel(x_ref, gate_ref, ya_ref, yb_ref, yc_ref, wb_ref, wo_ref, o_ref):
    gate = _sigmoid(gate_ref[...].astype(F32))
    merged = None
    for bi, y_ref in enumerate((ya_ref, yb_ref, yc_ref)):
        term = gate[:, bi * D_MODEL:(bi + 1) * D_MODEL] * _dot(y_ref[...], wb_ref[bi])
        merged = term if merged is None else merged + term
    o_ref[...] = x_ref[...] + _dot(merged.astype(BF16), wo_ref[...])


def _merge(x, gq, ya, yb, yc, wb, wo, tm):
    t, d = x.shape
    tok = lambda w: pl.BlockSpec((tm, w), lambda i: (i, 0))
    return pl.pallas_call(
        _merge_kernel,
        grid=(t // tm,),
        in_specs=[tok(d), tok(3 * d), tok(d), tok(d), tok(d),
                  pl.BlockSpec(wb.shape, lambda i: (0, 0, 0)), pl.BlockSpec(wo.shape, lambda i: (0, 0))],
        out_specs=tok(d),
        out_shape=jax.ShapeDtypeStruct((t, d), F32),
        compiler_params=_cparams(("arbitrary",)),
        name="merge",
    )(x, gq, ya, yb, yc, wb, wo)


def _ffn_kernel(x_ref, g_ref, wu_ref, wd_ref, gf_ref, o_ref, h_scr, *, final_norm):
    f = pl.program_id(1)

    @pl.when(f == 0)
    def _init():
        x = x_ref[...]
        h_scr[...] = _rmsnorm(x, g_ref[...]).astype(BF16)
        o_ref[...] = x

    up = _dot(h_scr[...], wu_ref[...])
    act = jnp.square(jnp.maximum(up, 0.0)).astype(BF16)
    o_ref[...] += _dot(act, wd_ref[...])

    if final_norm:
        @pl.when(f == pl.num_programs(1) - 1)
        def _final():
            o_ref[...] = _rmsnorm(o_ref[...], gf_ref[...])


def _ffn(x, g, wu, wd, g_final, final_norm, tm, tf):
    t, d = x.shape
    return pl.pallas_call(
        functools.partial(_ffn_kernel, final_norm=final_norm),
        grid=(t // tm, D_FF // tf),
        in_specs=[pl.BlockSpec((tm, d), lambda i, f: (i, 0)),
                  pl.BlockSpec((1, d), lambda i, f: (0, 0)),
                  pl.BlockSpec((d, tf), lambda i, f: (0, f)),
                  pl.BlockSpec((tf, d), lambda i, f: (f, 0)),
                  pl.BlockSpec((1, d), lambda i, f: (0, 0))],
        out_specs=pl.BlockSpec((tm, d), lambda i, f: (i, 0)),
        out_shape=jax.ShapeDtypeStruct((t, d), F32),
        scratch_shapes=[pltpu.VMEM((tm, d), BF16)],
        compiler_params=_cparams(("arbitrary", "arbitrary")),
        name="ffn",
    )(x, g, wu, wd, g_final)


def _state_to_pairs(s):
    b = s.shape[0]
    s = s.reshape(b, N_PAIRS, 2, HEAD_DIM, HEAD_DIM)
    z = jnp.zeros_like(s[:, :, 0])
    top = jnp.concatenate([s[:, :, 0], z], axis=-1)
    bot = jnp.concatenate([z, s[:, :, 1]], axis=-1)
    return jnp.concatenate([top, bot], axis=-2)


def _pairs_to_state(s2):
    b = s2.shape[0]
    s0 = s2[:, :, :HEAD_DIM, :HEAD_DIM]
    s1 = s2[:, :, HEAD_DIM:, HEAD_DIM:]
    return jnp.stack([s0, s1], axis=2).reshape(b, N_HEADS, HEAD_DIM, HEAD_DIM)


def _pick(t, candidates):
    for c in candidates:
        if t % c == 0:
            return c
    return t


def _group_layer(x, b, t, mem_k, mem_v, prev_row, s0_pairs, lp):
    n_tok = b * t
    tm = _pick(n_tok, (512, 256))
    g_mix = lp["norm_mix_g"]
    cols3 = _proj(x, g_mix, lp["w_in_rwkv"], BF16, tm, RWKV_COLS).reshape(b, t, RWKV_COLS)
    uv3 = _proj(x, g_mix, lp["w_in_gmlp"], BF16, tm, 2 * D_MODEL).reshape(b, t, 2 * D_MODEL)
    gq = _proj(x, g_mix, lp["w_in_gq"], BF16, tm, 4 * D_MODEL)
    gq3 = gq.reshape(b, t, 4 * D_MODEL)
    new_row = cols3[:, t - 1:t, :].astype(F32)

    tp = -(-t // (2 * CHUNK)) * 2 * CHUNK
    cols3p = cols3 if tp == t else jnp.pad(cols3, ((0, 0), (0, tp - t), (0, 0)))
    prev = (prev_row[..., :D_MODEL], prev_row[..., D_MODEL:2 * D_MODEL], prev_row[..., 2 * D_MODEL:3 * D_MODEL],
            prev_row[..., 3 * D_MODEL:])
    y_a, s_new = _rwkv(cols3p, prev, s0_pairs, lp["rwkv_vec"], lp["rwkv_mu_lo"], lp["rwkv_lw"], lp["e256"], t)
    y_a = y_a[:, :t].reshape(n_tok, D_MODEL)

    lc = min(t, GMLP_CHUNK)
    n_chunks = _pick(t // lc, (8, 4, 2, 1))
    ws = lp["sgu_w_s"][:, :lc, :lc]
    bs_full = jnp.tile(jnp.repeat(lp["sgu_b_s"][:, :lc].T, GMLP_CHUNK, axis=1), (n_chunks, 1))
    sgu_out = _sgu(uv3, lp["sgu_ln"], ws, bs_full, lc, n_chunks, want_v=t < GMLP_CHUNK)
    y_b = sgu_out[0].reshape(n_tok, D_MODEL)
    v_rows = sgu_out[1] if t < GMLP_CHUNK else None

    y_c = _xattn(gq3, mem_k, mem_v, _pick(t, (1024, 512, 256, 128))).reshape(n_tok, D_MODEL)

    x = _merge(x, gq, y_a, y_b, y_c, lp["w_branch"], lp["w_out"], _pick(n_tok, (512, 256)))
    x = _ffn(x, lp["norm_ffn_g"], lp["w_ffn_up"], lp["w_ffn_down"], lp["norm_final_g"], lp["is_last"],
             _pick(n_tok, (1024, 512, 256)), 2048)
    return x, new_row, s_new, v_rows


def kernel(x_prompt, x_sample, cache_mem_k, cache_mem_v, state_wkv, state_shift, mem_prompt, norm_mix_g, norm_mem_g, norm_ffn_g, norm_final_g, w_in, w_mem_kv, rwkv_mu, rwkv_w0, rwkv_w2, rwkv_a0, rwkv_a2, rwkv_g2, rwkv_k_k, rwkv_k_a, rwkv_r_k, rwkv_lnx_g, rwkv_lnx_b, sgu_ln_g, sgu_ln_b, sgu_w_s, sgu_b_s, w_branch, w_out, w_ffn_up, w_ffn_down):
    depth = w_in.shape[0]
    bp, tp, _ = x_prompt.shape
    bs, ts, _ = x_sample.shape

    w_in_rwkv = w_in[..., :RWKV_COLS].astype(BF16)
    w_in_gmlp = w_in[..., RWKV_COLS:COL_Q_START].astype(BF16)
    w_in_gq = jnp.concatenate([w_in[..., COL_Q_START + D_MODEL:], w_in[..., COL_Q_START:COL_Q_START + D_MODEL]],
                              axis=-1).astype(BF16)
    w_kv_b = w_mem_kv.astype(BF16)
    w_branch_b = w_branch.astype(BF16)
    w_out_b = w_out.astype(BF16)
    w_up_b = w_ffn_up.astype(BF16)
    w_down_b = w_ffn_down.astype(BF16)
    zeros = lambda n: jnp.zeros((depth, n, D_MODEL), F32)
    lw = jnp.stack([jnp.concatenate([rwkv_w2, zeros(192)], axis=1),
                    jnp.concatenate([zeros(64), rwkv_a2, zeros(128)], axis=1),
                    jnp.concatenate([zeros(128), rwkv_g2], axis=1)], axis=1).astype(BF16)
    rwkv_vec = jnp.stack([rwkv_mu[:, :D_MODEL], rwkv_mu[:, D_MODEL:2 * D_MODEL], rwkv_mu[:, 2 * D_MODEL:3 * D_MODEL],
                          rwkv_w0, rwkv_a0, rwkv_k_k, rwkv_k_a, rwkv_r_k.reshape(depth, D_MODEL),
                          rwkv_lnx_g, rwkv_lnx_b] + [jnp.zeros((depth, D_MODEL), F32)] * 6, axis=1)
    sgu_ln = jnp.stack([sgu_ln_g, sgu_ln_b] + [jnp.zeros((depth, D_MODEL), F32)] * 6, axis=1)
    hq = 4 * HEAD_DIM
    e256 = (jnp.arange(hq)[:, None] // HEAD_DIM == jnp.arange(hq)[None, :] // HEAD_DIM).astype(BF16)

    xp = x_prompt.reshape(bp * tp, D_MODEL)
    xs = x_sample.reshape(bs * ts, D_MODEL)
    mem_flat = mem_prompt.reshape(bp * N_MEM, D_MODEL)
    prompt_row0 = jnp.zeros((bp, 1, RWKV_COLS), F32)
    prompt_s0 = jnp.zeros((bp, N_PAIRS, PAIR, PAIR), F32)

    mk_p, mv_p, wkv_p, row_p, wkv_s, row_s, v_s = [], [], [], [], [], [], []
    for l in range(depth):
        lp = {
            "norm_mix_g": norm_mix_g[l][None],
            "w_in_rwkv": w_in_rwkv[l], "w_in_gmlp": w_in_gmlp[l], "w_in_gq": w_in_gq[l],
            "rwkv_vec": rwkv_vec[l], "rwkv_mu_lo": rwkv_mu[l][None, 3 * D_MODEL:], "rwkv_lw": lw[l], "e256": e256,
            "sgu_ln": sgu_ln[l], "sgu_w_s": sgu_w_s[l], "sgu_b_s": sgu_b_s[l],
            "w_branch": w_branch_b[l], "w_out": w_out_b[l],
            "norm_ffn_g": norm_ffn_g[l][None], "w_ffn_up": w_up_b[l], "w_ffn_down": w_down_b[l],
            "norm_final_g": norm_final_g[None], "is_last": l == depth - 1,
        }
        kv = _proj(mem_flat, norm_mem_g[l][None], w_kv_b[l], F32, _pick(bp * N_MEM, (512, 256)), 2 * D_MODEL)
        mem_k = kv[:, :D_MODEL].reshape(bp, N_MEM, D_MODEL)
        mem_v = kv[:, D_MODEL:].reshape(bp, N_MEM, D_MODEL)
        xp, r_p, s_p, _ = _group_layer(xp, bp, tp, mem_k.astype(BF16), mem_v.astype(BF16), prompt_row0, prompt_s0, lp)
        mk_p.append(mem_k.reshape(bp, N_MEM, X_HEADS, X_HEAD_DIM))
        mv_p.append(mem_v.reshape(bp, N_MEM, X_HEADS, X_HEAD_DIM))
        wkv_p.append(_pairs_to_state(s_p))
        row_p.append(r_p)

        ck = cache_mem_k[l].reshape(bs, N_MEM, D_MODEL).astype(BF16)
        cv = cache_mem_v[l].reshape(bs, N_MEM, D_MODEL).astype(BF16)
        xs, r_s, s_s, vr = _group_layer(xs, bs, ts, ck, cv, _permute_rwkv_prev(state_shift[l]),
                                        _state_to_pairs(state_wkv[l]), lp)
        wkv_s.append(_pairs_to_state(s_s))
        row_s.append(r_s)
        v_s.append(vr)

    y_prompt = xp.reshape(bp, tp, D_MODEL)
    y_sample = xs.reshape(bs, ts, D_MODEL)
    return (y_prompt, y_sample, jnp.stack(mk_p), jnp.stack(mv_p), jnp.stack(wkv_p), jnp.stack(row_p),
            jnp.stack(wkv_s), jnp.stack(row_s), jnp.stack(v_s))


def _permute_rwkv_prev(row):
    return row
```

```python
import functools
import math

import jax
import jax.numpy as jnp
from jax import lax
from jax.experimental import pallas as pl
from jax.experimental.pallas import tpu as pltpu

F32 = jnp.float32
BF16 = jnp.bfloat16

D_MODEL = 1024
HEAD_DIM = 64
N_HEADS = D_MODEL // HEAD_DIM
PAIR = 2 * HEAD_DIM
N_PAIRS = D_MODEL // PAIR
LORA_COLS = 256
RWKV_COLS = 3 * D_MODEL + LORA_COLS
GMLP_CHUNK = 128
GMLP_GROUPS = 8
N_MEM = 256
X_HEADS = 4
X_HEAD_DIM = D_MODEL // X_HEADS
D_FF = 4 * D_MODEL
COL_Q_START = RWKV_COLS + 2 * D_MODEL
GQ_Q_BLOCK = 3
RMS_EPS = 1e-6
LN_EPS = 1e-5
GN_EPS = 64e-5
CHUNK = 64
DECAY_SCALE = math.exp(-0.5)
VMEM_LIMIT = 56 * 1024 * 1024
PASSES_INVERSE = ((3, 3), (3, 3), (3, 3), (3, 3), (3, 3))
RWKV_SEQS_PER_STEP = 2


def _cparams(sem):
    return pltpu.CompilerParams(dimension_semantics=sem, vmem_limit_bytes=VMEM_LIMIT)


def _dot(a, b):
    return jnp.dot(a, b, preferred_element_type=F32)


def _dot_nt(a, b):
    return lax.dot_general(a, b, (((1,), (1,)), ((), ())), preferred_element_type=F32)


def _dot_tn(a, b):
    return lax.dot_general(a, b, (((0,), (0,)), ((), ())), preferred_element_type=F32)


def _split(x):
    hi = x.astype(BF16)
    lo = (x - hi.astype(F32)).astype(BF16)
    return hi, lo


def _rmsnorm(x, g):
    return x * lax.rsqrt(jnp.mean(x * x, axis=-1, keepdims=True) + RMS_EPS) * g


def _sigmoid(x):
    return 1.0 / (1.0 + jnp.exp(-x))


def _gelu(x):
    return 0.5 * x * (1.0 + jnp.tanh(math.sqrt(2.0 / math.pi) * (x + 0.044715 * (x * x * x))))


def _proj_kernel(x_ref, g_ref, w_ref, o_ref):
    h = _rmsnorm(x_ref[...], g_ref[...]).astype(BF16)
    o_ref[...] = _dot(h, w_ref[...]).astype(o_ref.dtype)


def _proj(x, g, w, out_dtype, tm, tn):
    t, d = x.shape
    nc = w.shape[1]
    return pl.pallas_call(
        _proj_kernel,
        grid=(nc // tn, t // tm),
        in_specs=[pl.BlockSpec((tm, d), lambda j, i: (i, 0)),
                  pl.BlockSpec((1, d), lambda j, i: (0, 0)),
                  pl.BlockSpec((d, tn), lambda j, i: (0, j))],
        out_specs=pl.BlockSpec((tm, tn), lambda j, i: (i, j)),
        out_shape=jax.ShapeDtypeStruct((t, nc), out_dtype),
        compiler_params=_cparams(("arbitrary", "arbitrary")),
        name="proj",
    )(x, g, w)


def _head_sum(x, e_ref):
    c = x.shape[0]
    q = 4 * HEAD_DIM
    xs = jnp.concatenate([x[:, i * q:(i + 1) * q] for i in range(D_MODEL // q)], axis=0)
    r = _dot(xs.astype(BF16), e_ref[...])
    return jnp.concatenate([r[i * c:(i + 1) * c] for i in range(D_MODEL // q)], axis=1)


def _rwkv_kernel(cols_ref, next_ref, pr_ref, pk_ref, pv_ref, plo_ref, s0_ref,
                 vec_ref, mulo_ref, lw_ref, e_ref,
                 y_ref, sout_ref,
                 s_scr, prr, prk, prv, prlo, ops_x, aux_x, wend_x, ops_y, aux_y, wend_y, *, t_valid, nb):
    i = pl.program_id(1)
    C = CHUNK
    prev = (prr, prk, prv, prlo)
    buf_x, buf_y = (ops_x, aux_x, wend_x), (ops_y, aux_y, wend_y)

    def col_views(ref, rows):
        return ([ref.at[:, rows, j * D_MODEL:(j + 1) * D_MODEL] for j in range(3)]
                + [ref.at[:, rows, 3 * D_MODEL:RWKV_COLS]])

    def prep(ref, rows, chunk, buf):
        return _rwkv_prep(col_views(ref, rows), prev, vec_ref, mulo_ref, lw_ref, e_ref, buf,
                          t_valid=t_valid, chunk=chunk, nb=nb)

    def mix(buf, rows):
        return _rwkv_mix(buf, s_scr, y_ref.at[:, rows, :], vec_ref, e_ref, nb=nb)

    def emit(main, side):
        for _ in main:
            next(side, None)
        for _ in side:
            pass

    @pl.when(i == 0)
    def _init():
        s_scr[...] = s0_ref[...]
        prr[...] = pr_ref[...]
        prk[...] = pk_ref[...]
        prv[...] = pv_ref[...]
        prlo[...] = plo_ref[...]
        emit(prep(cols_ref, slice(0, C), 0, buf_x), iter(()))

    emit(mix(buf_x, slice(0, C)), prep(cols_ref, slice(C, 2 * C), 2 * i + 1, buf_y))
    emit(mix(buf_y, slice(C, 2 * C)), prep(next_ref, slice(0, C), 2 * i + 2, buf_x))

    @pl.when(i == pl.num_programs(1) - 1)
    def _fin():
        sout_ref[...] = s_scr[...]


def _rwkv_prep(col_refs, prev, vec_ref, mulo_ref, lw_ref, e_ref, buf, *, t_valid, chunk, nb):
    r_ref, k_ref, v_ref, lo_ref = col_refs
    prr, prk, prv, prlo = prev
    ops, aux, wend = buf
    C = CHUNK
    R = nb * C

    def vec(i):
        return vec_ref[i:i + 1, :]

    row = lax.broadcasted_iota(jnp.int32, (R, 1), 0)
    tpos = row & (C - 1)
    seq_rows = [slice(b * C, (b + 1) * C) for b in range(nb)]

    def per_seq_rows(rows):
        out = rows[nb - 1]
        for b in reversed(range(nb - 1)):
            out = jnp.where(row < (b + 1) * C, rows[b], out)
        return out

    def mixed(ref, prev, mu):
        x = ref[...].astype(F32).reshape(R, ref.shape[-1])
        shifted = jnp.where(tpos == 0, per_seq_rows([prev[b] for b in range(nb)]), pltpu.roll(x, 1, 0))
        for b in range(nb):
            prev[b] = x[(b + 1) * C - 1:(b + 1) * C, :]
        return x + (shifted - x) * mu

    r = mixed(r_ref, prr, vec(0))
    k = mixed(k_ref, prk, vec(1))
    v = mixed(v_ref, prv, vec(2))
    lo = mixed(lo_ref, prlo, mulo_ref[...])
    yield

    lane_lo = lax.broadcasted_iota(jnp.int32, lo.shape, 1)
    act = jnp.where(lane_lo < 64, jnp.tanh(lo), jnp.where(lane_lo < 128, lo, _sigmoid(lo))).astype(BF16)
    z = vec(3) + _dot(act, lw_ref[0])
    a = _sigmoid(vec(4) + _dot(act, lw_ref[1]))
    g = _dot(act, lw_ref[2])
    logw = -DECAY_SCALE * _sigmoid(z)
    yield

    kk = k * vec(5)
    kk = kk * lax.rsqrt(jnp.maximum(_head_sum(kk * kk, e_ref), 1e-24))
    k = k * (1.0 + (a - 1.0) * vec(6))
    bonus = _head_sum(r * k * vec(7), e_ref)
    yield

    if t_valid is not None:
        valid = chunk * C + tpos < t_valid
        logw = jnp.where(valid, logw, 0.0)
        kk = jnp.where(valid, kk, 0.0)
        k = jnp.where(valid, k, 0.0)
        v = jnp.where(valid, v, 0.0)

    ti = lax.broadcasted_iota(jnp.int32, (R, R), 0)
    tj = lax.broadcasted_iota(jnp.int32, (R, R), 1)
    tril = jnp.where((tj <= ti) & (tj >= ti - (ti & (C - 1))), 1.0, 0.0).astype(BF16)
    lw_hi, lw_lo = _split(logw)
    cum = _dot(jnp.concatenate([tril, tril], axis=1), jnp.concatenate([lw_hi, lw_lo], axis=0))
    e_inc = jnp.exp(cum)
    e_dec = jnp.exp(-cum)
    yield
    a_t = -kk * jnp.exp(cum - logw)
    r_t = r * e_inc
    b_t = kk * a * e_dec
    k_t = k * e_dec
    w_end = [e_inc[(b + 1) * C - 1:(b + 1) * C, :] for b in range(nb)]
    w_end_rows = per_seq_rows(w_end)
    for j, op in enumerate((a_t, r_t, b_t, k_t, b_t * w_end_rows, k_t * w_end_rows, v)):
        ops[j] = op.astype(BF16)
    aux[0] = bonus * v
    aux[1] = g
    for b in range(nb):
        wend[b] = w_end[b]


N_OPS = 7


def _rwkv_mix(buf, s_scr, y_ref, vec_ref, e_ref, *, nb):
    ops, aux, wend = buf
    C = CHUNK
    seq_rows = [slice(b * C, (b + 1) * C) for b in range(nb)]

    def vec(i):
        return vec_ref[i:i + 1, :]

    lane = lax.broadcasted_iota(jnp.int32, (C, PAIR), 1)
    first = lane < HEAD_DIM
    t_row = lax.broadcasted_iota(jnp.int32, (C, PAIR), 0)
    s_lane = lane & (HEAD_DIM - 1)
    strict = s_lane < t_row
    incl = s_lane <= t_row
    bi = lax.broadcasted_iota(jnp.int32, (PAIR, PAIR), 0)
    bj = lax.broadcasted_iota(jnp.int32, (PAIR, PAIR), 1)
    same_head = (bi < HEAD_DIM) == (bj < HEAD_DIM)

    def blockdiag(x):
        z = jnp.zeros_like(x)
        return jnp.concatenate([jnp.where(first, x, z), jnp.where(first, z, x)], axis=0)

    def mm3(x, y):
        xh, xl = _split(x)
        yh, yl = _split(y)
        yh, yl = blockdiag(yh), blockdiag(yl)
        w = jnp.concatenate([jnp.concatenate([yh, yl], axis=1),
                             jnp.concatenate([yh, jnp.zeros_like(yh)], axis=1)], axis=0)
        res = _dot(jnp.concatenate([xh, xl], axis=1), w)
        return res[:, :PAIR] + res[:, PAIR:]

    def mm(x, y, passes):
        if passes == 1:
            return _dot(x.astype(BF16), blockdiag(y.astype(BF16)))
        return mm3(x, y)

    yield
    units = [(b, p) for b in range(nb) for p in range(N_PAIRS)]
    U = range(len(units))
    cut = [(seq_rows[b], slice(p * PAIR, (p + 1) * PAIR)) for b, p in units]
    op = lambda j, u: ops[(j,) + cut[u]]
    lhs = [jnp.concatenate([op(0, u), op(1, u)], axis=0) for u in U]
    rhs = [jnp.concatenate([blockdiag(op(2, u)), blockdiag(op(3, u))], axis=0) for u in U]
    pm = [_dot_nt(lhs[u], rhs[u]) for u in U]
    yield
    m = [jnp.where(strict, pm[u][0:C, 0:PAIR], 0.0) for u in U]
    akrk = [jnp.concatenate([jnp.where(strict, pm[u][0:C, PAIR:], 0.0),
                             jnp.where(incl, pm[u][C:, PAIR:], 0.0)], axis=0).astype(BF16) for u in U]
    rb = [jnp.where(incl, pm[u][C:, 0:PAIR], 0.0).astype(BF16) for u in U]
    yield
    xr = m
    for sq_passes, up_passes in PASSES_INVERSE:
        m = [mm(m[u], m[u], sq_passes) for u in U]
        xr = [xr[u] + m[u] + mm(xr[u], m[u], up_passes) for u in U]
        yield
    s2 = [s_scr[b, p] for b, p in units]
    as_rs = [_dot_nt(lhs[u], s2[u].astype(BF16)) for u in U]
    vb = [op(6, u) for u in U]
    pv = [_dot(akrk[u], blockdiag(vb[u])) for u in U]
    yield
    rhs0 = [as_rs[u][0:C] + pv[u][0:C] for u in U]
    sab = [(rhs0[u] + mm(xr[u], rhs0[u], 1)).astype(BF16) for u in U]
    yield
    y_units = [as_rs[u][C:] + pv[u][C:] + _dot(rb[u], blockdiag(sab[u])) for u in U]
    bhkh = [jnp.concatenate([op(4, u), op(5, u)], axis=0) for u in U]
    for u, (b, p) in enumerate(units):
        upd = _dot_tn(jnp.concatenate([sab[u], vb[u]], axis=0), bhkh[u])
        s_scr[b, p] = s2[u] * wend[b][:, cut[u][1]] + jnp.where(same_head, upd, 0.0)

    yield
    y = jnp.concatenate([jnp.concatenate(y_units[b * N_PAIRS:(b + 1) * N_PAIRS], axis=1) for b in range(nb)], axis=0)
    mean = _head_sum(y, e_ref) * (1.0 / HEAD_DIM)
    yield
    yc = y - mean
    var = _head_sum(yc * yc, e_ref) * (1.0 / HEAD_DIM)
    yield
    yn = yc * lax.rsqrt(var + GN_EPS) * vec(8) + vec(9)
    y_ref[...] = ((yn + aux[0]) * aux[1]).astype(y_ref.dtype).reshape(nb, C, D_MODEL)


def _rwkv(cols3, prev, s0, vec, mulo, lw, e256, t_valid):
    b, tp, _ = cols3.shape
    C = CHUNK
    nb = _pick(b, (RWKV_SEQS_PER_STEP, 1))
    n_chunks = tp // C
    rowspec = lambda w: pl.BlockSpec((nb, 1, w), lambda bi, ci: (bi, 0, 0))
    full = lambda a: pl.BlockSpec(a.shape, lambda bi, ci: (0,) * a.ndim)
    sspec = pl.BlockSpec((nb, N_PAIRS, PAIR, PAIR), lambda bi, ci: (bi, 0, 0, 0))
    operand_bufs = [pltpu.VMEM((N_OPS, nb * C, D_MODEL), BF16), pltpu.VMEM((2, nb * C, D_MODEL), F32),
                    pltpu.VMEM((nb, 1, D_MODEL), F32)]
    return pl.pallas_call(
        functools.partial(_rwkv_kernel, t_valid=t_valid if t_valid < tp else None, nb=nb),
        grid=(b // nb, n_chunks // 2),
        in_specs=[pl.BlockSpec((nb, 2 * C, RWKV_COLS), lambda bi, ci: (bi, ci, 0)),
                  pl.BlockSpec((nb, C, RWKV_COLS), lambda bi, ci: (bi, jnp.minimum(2 * ci + 2, n_chunks - 1), 0)),
                  rowspec(D_MODEL), rowspec(D_MODEL), rowspec(D_MODEL), rowspec(LORA_COLS),
                  sspec, full(vec), full(mulo), full(lw), full(e256)],
        out_specs=[pl.BlockSpec((nb, 2 * C, D_MODEL), lambda bi, ci: (bi, ci, 0)), sspec],
        out_shape=[jax.ShapeDtypeStruct((b, tp, D_MODEL), BF16),
                   jax.ShapeDtypeStruct(s0.shape, F32)],
        scratch_shapes=[pltpu.VMEM((nb, N_PAIRS, PAIR, PAIR), F32),
                        pltpu.VMEM((nb, 1, D_MODEL), F32), pltpu.VMEM((nb, 1, D_MODEL), F32),
                        pltpu.VMEM((nb, 1, D_MODEL), F32), pltpu.VMEM((nb, 1, LORA_COLS), F32)]
                       + operand_bufs + operand_bufs,
        compiler_params=_cparams(("arbitrary", "arbitrary")),
        name="rwkv7",
    )(cols3, cols3, *prev, s0, vec, mulo, lw, e256)


def _sgu_kernel(u_ref, v_ref, ln_ref, ws_ref, bs_ref, y_ref, *maybe_v_out, lc):
    u = _gelu(u_ref[...]).astype(F32)
    v = _gelu(v_ref[...]).astype(F32)
    mu = jnp.mean(v, axis=-1, keepdims=True)
    vc = v - mu
    var = jnp.mean(vc * vc, axis=-1, keepdims=True)
    v = vc * lax.rsqrt(var + LN_EPS) * ln_ref[0:1, :] + ln_ref[1:2, :]
    if maybe_v_out:
        maybe_v_out[0][...] = v
    vb = v.astype(BF16)
    n_chunks = v.shape[0] // lc
    ri = lax.broadcasted_iota(jnp.int32, (lc, lc), 0)
    ci = lax.broadcasted_iota(jnp.int32, (lc, lc), 1)
    bias = bs_ref[...]
    outs = []
    for gi in range(GMLP_GROUPS):
        gs = slice(gi * GMLP_CHUNK, (gi + 1) * GMLP_CHUNK)
        w = jnp.where(ci <= ri, ws_ref[gi], 0.0).astype(BF16)
        rhs = jnp.concatenate([vb[n * lc:(n + 1) * lc, gs] for n in range(n_chunks)], axis=1)
        sv = _dot(w, rhs)
        sv = jnp.concatenate([sv[:, n * GMLP_CHUNK:(n + 1) * GMLP_CHUNK] for n in range(n_chunks)], axis=0)
        outs.append(sv + bias[:, gs])
    sv = jnp.concatenate(outs, axis=1)
    y_ref[...] = (u * sv).astype(y_ref.dtype)


def _sgu(uv3, ln, ws, bs_full, lc, n_chunks, want_v):
    b, t, _ = uv3.shape
    tt = lc * n_chunks
    col = lambda j: pl.BlockSpec((None, tt, D_MODEL), lambda bi, ti: (bi, ti, j))
    full = lambda a: pl.BlockSpec(a.shape, lambda bi, ti: (0,) * a.ndim)
    ospec = pl.BlockSpec((None, tt, D_MODEL), lambda bi, ti: (bi, ti, 0))
    out_specs = [ospec]
    out_shape = [jax.ShapeDtypeStruct((b, t, D_MODEL), BF16)]
    if want_v:
        out_specs.append(ospec)
        out_shape.append(jax.ShapeDtypeStruct((b, t, D_MODEL), F32))
    return pl.pallas_call(
        functools.partial(_sgu_kernel, lc=lc),
        grid=(b, t // tt),
        in_specs=[col(0), col(1), full(ln), full(ws), full(bs_full)],
        out_specs=out_specs,
        out_shape=out_shape,
        compiler_params=_cparams(("arbitrary", "arbitrary")),
        name="sgu",
    )(uv3, uv3, ln, ws, bs_full)


def _xattn_kernel(q_ref, k_ref, v_ref, o_ref):
    outs = []
    for h in range(X_HEADS):
        hs = slice(h * X_HEAD_DIM, (h + 1) * X_HEAD_DIM)
        s = _dot_nt(q_ref[:, hs], k_ref[:, hs]) * (X_HEAD_DIM ** -0.5)
        s = s - jnp.max(s, axis=-1, keepdims=True)
        e = jnp.exp(s)
        p = e * (1.0 / jnp.sum(e, axis=-1, keepdims=True))
        outs.append(_dot(p.astype(BF16), v_ref[:, hs]))
    o_ref[...] = jnp.concatenate(outs, axis=1).astype(o_ref.dtype)


def _xattn(gq3, mem_k, mem_v, tq):
    b, t, _ = gq3.shape
    mspec = pl.BlockSpec((None, N_MEM, D_MODEL), lambda bi, ti: (bi, 0, 0))
    return pl.pallas_call(
        _xattn_kernel,
        grid=(b, t // tq),
        in_specs=[pl.BlockSpec((None, tq, D_MODEL), lambda bi, ti: (bi, ti, GQ_Q_BLOCK)), mspec, mspec],
        out_specs=pl.BlockSpec((None, tq, D_MODEL), lambda bi, ti: (bi, ti, 0)),
        out_shape=jax.ShapeDtypeStruct((b, t, D_MODEL), BF16),
        compiler_params=_cparams(("arbitrary", "arbitrary")),
        name="xattn",
    )(gq3, mem_k, mem_v)


def _merge_kernel(x_ref, gate_ref, ya_ref, yb_ref, yc_ref, wb_ref, wo_ref, o_ref):
    gate = _sigmoid(gate_ref[...].astype(F32))
    merged = None
    for bi, y_ref in enumerate((ya_ref, yb_ref, yc_ref)):
        term = gate[:, bi * D_MODEL:(bi + 1) * D_MODEL] * _dot(y_ref[...], wb_ref[bi])
        merged = term if merged is None else merged + term
    o_ref[...] = x_ref[...] + _dot(merged.astype(BF16), wo_ref[...])


def _merge(x, gq, ya, yb, yc, wb, wo, tm):
    t, d = x.shape
    tok = lambda w: pl.BlockSpec((tm, w), lambda i: (i, 0))
    return pl.pallas_call(
        _merge_kernel,
        grid=(t // tm,),
        in_specs=[tok(d), tok(3 * d), tok(d), tok(d), tok(d),
                  pl.BlockSpec(wb.shape, lambda i: (0, 0, 0)), pl.BlockSpec(wo.shape, lambda i: (0, 0))],
        out_specs=tok(d),
        out_shape=jax.ShapeDtypeStruct((t, d), F32),
        compiler_params=_cparams(("arbitrary",)),
        name="merge",
    )(x, gq, ya, yb, yc, wb, wo)


def _ffn_kernel(x_ref, g_ref, wu_ref, wd_ref, gf_ref, o_ref, h_scr, *, final_norm):
    f = pl.program_id(1)

    @pl.when(f == 0)
    def _init():
        x = x_ref[...]
        h_scr[...] = _rmsnorm(x, g_ref[...]).astype(BF16)
        o_ref[...] = x

    up = _dot(h_scr[...], wu_ref[...])
    act = jnp.square(jnp.maximum(up, 0.0)).astype(BF16)
    o_ref[...] += _dot(act, wd_ref[...])

    if final_norm:
        @pl.when(f == pl.num_programs(1) - 1)
        def _final():
            o_ref[...] = _rmsnorm(o_ref[...], gf_ref[...])


def _ffn(x, g, wu, wd, g_final, final_norm, tm, tf):
    t, d = x.shape
    return pl.pallas_call(
        functools.partial(_ffn_kernel, final_norm=final_norm),
        grid=(t // tm, D_FF // tf),
        in_specs=[pl.BlockSpec((tm, d), lambda i, f: (i, 0)),
                  pl.BlockSpec((1, d), lambda i, f: (0, 0)),
                  pl.BlockSpec((d, tf), lambda i, f: (0, f)),
                  pl.BlockSpec((tf, d), lambda i, f: (f, 0)),
                  pl.BlockSpec((1, d), lambda i, f: (0, 0))],
        out_specs=pl.BlockSpec((tm, d), lambda i, f: (i, 0)),
        out_shape=jax.ShapeDtypeStruct((t, d), F32),
        scratch_shapes=[pltpu.VMEM((tm, d), BF16)],
        compiler_params=_cparams(("arbitrary", "arbitrary")),
        name="ffn",
    )(x, g, wu, wd, g_final)


def _state_to_pairs(s):
    b = s.shape[0]
    s = s.reshape(b, N_PAIRS, 2, HEAD_DIM, HEAD_DIM)
    z = jnp.zeros_like(s[:, :, 0])
    top = jnp.concatenate([s[:, :, 0], z], axis=-1)
    bot = jnp.concatenate([z, s[:, :, 1]], axis=-1)
    return jnp.concatenate([top, bot], axis=-2)


def _pairs_to_state(s2):
    b = s2.shape[0]
    s0 = s2[:, :, :HEAD_DIM, :HEAD_DIM]
    s1 = s2[:, :, HEAD_DIM:, HEAD_DIM:]
    return jnp.stack([s0, s1], axis=2).reshape(b, N_HEADS, HEAD_DIM, HEAD_DIM)


def _pick(t, candidates):
    for c in candidates:
        if t % c == 0:
            return c
    return t


def _group_layer(x, b, t, mem_k, mem_v, prev_row, s0_pairs, lp):
    n_tok = b * t
    tm = _pick(n_tok, (512, 256))
    g_mix = lp["norm_mix_g"]
    cols3 = _proj(x, g_mix, lp["w_in_rwkv"], BF16, tm, RWKV_COLS).reshape(b, t, RWKV_COLS)
    uv3 = _proj(x, g_mix, lp["w_in_gmlp"], BF16, tm, 2 * D_MODEL).reshape(b, t, 2 * D_MODEL)
    gq = _proj(x, g_mix, lp["w_in_gq"], BF16, tm, 4 * D_MODEL)
    gq3 = gq.reshape(b, t, 4 * D_MODEL)
    new_row = cols3[:, t - 1:t, :].astype(F32)

    tp = -(-t // (2 * CHUNK)) * 2 * CHUNK
    cols3p = cols3 if tp == t else jnp.pad(cols3, ((0, 0), (0, tp - t), (0, 0)))
    prev = (prev_row[..., :D_MODEL], prev_row[..., D_MODEL:2 * D_MODEL], prev_row[..., 2 * D_MODEL:3 * D_MODEL],
            prev_row[..., 3 * D_MODEL:])
    y_a, s_new = _rwkv(cols3p, prev, s0_pairs, lp["rwkv_vec"], lp["rwkv_mu_lo"], lp["rwkv_lw"], lp["e256"], t)
    y_a = y_a[:, :t].reshape(n_tok, D_MODEL)

    lc = min(t, GMLP_CHUNK)
    n_chunks = _pick(t // lc, (8, 4, 2, 1))
    ws = lp["sgu_w_s"][:, :lc, :lc]
    bs_full = jnp.tile(jnp.repeat(lp["sgu_b_s"][:, :lc].T, GMLP_CHUNK, axis=1), (n_chunks, 1))
    sgu_out = _sgu(uv3, lp["sgu_ln"], ws, bs_full, lc, n_chunks, want_v=t < GMLP_CHUNK)
    y_b = sgu_out[0].reshape(n_tok, D_MODEL)
    v_rows = sgu_out[1] if t < GMLP_CHUNK else None

    y_c = _xattn(gq3, mem_k, mem_v, _pick(t, (1024, 512, 256, 128))).reshape(n_tok, D_MODEL)

    x = _merge(x, gq, y_a, y_b, y_c, lp["w_branch"], lp["w_out"], _pick(n_tok, (512, 256)))
    x = _ffn(x, lp["norm_ffn_g"], lp["w_ffn_up"], lp["w_ffn_down"], lp["norm_final_g"], lp["is_last"],
             _pick(n_tok, (1024, 512, 256)), 2048)
    return x, new_row, s_new, v_rows


def kernel(x_prompt, x_sample, cache_mem_k, cache_mem_v, state_wkv, state_shift, mem_prompt, norm_mix_g, norm_mem_g, norm_ffn_g, norm_final_g, w_in, w_mem_kv, rwkv_mu, rwkv_w0, rwkv_w2, rwkv_a0, rwkv_a2, rwkv_g2, rwkv_k_k, rwkv_k_a, rwkv_r_k, rwkv_lnx_g, rwkv_lnx_b, sgu_ln_g, sgu_ln_b, sgu_w_s, sgu_b_s, w_branch, w_out, w_ffn_up, w_ffn_down):
    depth = w_in.shape[0]
    bp, tp, _ = x_prompt.shape
    bs, ts, _ = x_sample.shape

    w_in_rwkv = w_in[..., :RWKV_COLS].astype(BF16)
    w_in_gmlp = w_in[..., RWKV_COLS:COL_Q_START].astype(BF16)
    w_in_gq = jnp.concatenate([w_in[..., COL_Q_START + D_MODEL:], w_in[..., COL_Q_START:COL_Q_START + D_MODEL]],
                              axis=-1).astype(BF16)
    w_kv_b = w_mem_kv.astype(BF16)
    w_branch_b = w_branch.astype(BF16)
    w_out_b = w_out.astype(BF16)
    w_up_b = w_ffn_up.astype(BF16)
    w_down_b = w_ffn_down.astype(BF16)
    zeros = lambda n: jnp.zeros((depth, n, D_MODEL), F32)
    lw = jnp.stack([jnp.concatenate([rwkv_w2, zeros(192)], axis=1),
                    jnp.concatenate([zeros(64), rwkv_a2, zeros(128)], axis=1),
                    jnp.concatenate([zeros(128), rwkv_g2], axis=1)], axis=1).astype(BF16)
    rwkv_vec = jnp.stack([rwkv_mu[:, :D_MODEL], rwkv_mu[:, D_MODEL:2 * D_MODEL], rwkv_mu[:, 2 * D_MODEL:3 * D_MODEL],
                          rwkv_w0, rwkv_a0, rwkv_k_k, rwkv_k_a, rwkv_r_k.reshape(depth, D_MODEL),
                          rwkv_lnx_g, rwkv_lnx_b] + [jnp.zeros((depth, D_MODEL), F32)] * 6, axis=1)
    sgu_ln = jnp.stack([sgu_ln_g, sgu_ln_b] + [jnp.zeros((depth, D_MODEL), F32)] * 6, axis=1)
    hq = 4 * HEAD_DIM
    e256 = (jnp.arange(hq)[:, None] // HEAD_DIM == jnp.arange(hq)[None, :] // HEAD_DIM).astype(BF16)

    xp = x_prompt.reshape(bp * tp, D_MODEL)
    xs = x_sample.reshape(bs * ts, D_MODEL)
    mem_flat = mem_prompt.reshape(bp * N_MEM, D_MODEL)
    prompt_row0 = jnp.zeros((bp, 1, RWKV_COLS), F32)
    prompt_s0 = jnp.zeros((bp, N_PAIRS, PAIR, PAIR), F32)

    mk_p, mv_p, wkv_p, row_p, wkv_s, row_s, v_s = [], [], [], [], [], [], []
    for l in range(depth):
        lp = {
            "norm_mix_g": norm_mix_g[l][None],
            "w_in_rwkv": w_in_rwkv[l], "w_in_gmlp": w_in_gmlp[l], "w_in_gq": w_in_gq[l],
            "rwkv_vec": rwkv_vec[l], "rwkv_mu_lo": rwkv_mu[l][None, 3 * D_MODEL:], "rwkv_lw": lw[l], "e256": e256,
            "sgu_ln": sgu_ln[l], "sgu_w_s": sgu_w_s[l], "sgu_b_s": sgu_b_s[l],
            "w_branch": w_branch_b[l], "w_out": w_out_b[l],
            "norm_ffn_g": norm_ffn_g[l][None], "w_ffn_up": w_up_b[l], "w_ffn_down": w_down_b[l],
            "norm_final_g": norm_final_g[None], "is_last": l == depth - 1,
        }
        kv = _proj(mem_flat, norm_mem_g[l][None], w_kv_b[l], F32, _pick(bp * N_MEM, (512, 256)), 2 * D_MODEL)
        mem_k = kv[:, :D_MODEL].reshape(bp, N_MEM, D_MODEL)
        mem_v = kv[:, D_MODEL:].reshape(bp, N_MEM, D_MODEL)
        xp, r_p, s_p, _ = _group_layer(xp, bp, tp, mem_k.astype(BF16), mem_v.astype(BF16), prompt_row0, prompt_s0, lp)
        mk_p.append(mem_k.reshape(bp, N_MEM, X_HEADS, X_HEAD_DIM))
        mv_p.append(mem_v.reshape(bp, N_MEM, X_HEADS, X_HEAD_DIM))
        wkv_p.append(_pairs_to_state(s_p))
        row_p.append(r_p)

        ck = cache_mem_k[l].reshape(bs, N_MEM, D_MODEL).astype(BF16)
        cv = cache_mem_v[l].reshape(bs, N_MEM, D_MODEL).astype(BF16)
        xs, r_s, s_s, vr = _group_layer(xs, bs, ts, ck, cv, _permute_rwkv_prev(state_shift[l]),
                                        _state_to_pairs(state_wkv[l]), lp)
        wkv_s.append(_pairs_to_state(s_s))
        row_s.append(r_s)
        v_s.append(vr)

    y_prompt = xp.reshape(bp, tp, D_MODEL)
    y_sample = xs.reshape(bs, ts, D_MODEL)
    return (y_prompt, y_sample, jnp.stack(mk_p), jnp.stack(mv_p), jnp.stack(wkv_p), jnp.stack(row_p),
            jnp.stack(wkv_s), jnp.stack(row_s), jnp.stack(v_s))


def _permute_rwkv_prev(row):
    return row
```

```python
import functools
import math

import jax
import jax.numpy as jnp
from jax import lax
from jax.experimental import pallas as pl
from jax.experimental.pallas import tpu as pltpu

F32 = jnp.float32
BF16 = jnp.bfloat16

D_MODEL = 1024
HEAD_DIM = 64
N_HEADS = D_MODEL // HEAD_DIM
PAIR = 2 * HEAD_DIM
N_PAIRS = D_MODEL // PAIR
LORA_COLS = 256
RWKV_COLS = 3 * D_MODEL + LORA_COLS
GMLP_CHUNK = 128
GMLP_GROUPS = 8
N_MEM = 256
X_HEADS = 4
X_HEAD_DIM = D_MODEL // X_HEADS
D_FF = 4 * D_MODEL
COL_Q_START = RWKV_COLS + 2 * D_MODEL
GQ_Q_BLOCK = 3
RMS_EPS = 1e-6
LN_EPS = 1e-5
GN_EPS = 64e-5
CHUNK = 64
DECAY_SCALE = math.exp(-0.5)
VMEM_LIMIT = 56 * 1024 * 1024
PASSES_INVERSE = 1
INV_BASE = 8
RWKV_SEQS_PER_STEP = 2


def _cparams(sem):
    return pltpu.CompilerParams(dimension_semantics=sem, vmem_limit_bytes=VMEM_LIMIT)


def _dot(a, b):
    return jnp.dot(a, b, preferred_element_type=F32)


def _dot_nt(a, b):
    return lax.dot_general(a, b, (((1,), (1,)), ((), ())), preferred_element_type=F32)


def _dot_tn(a, b):
    return lax.dot_general(a, b, (((0,), (0,)), ((), ())), preferred_element_type=F32)


def _split(x):
    hi = x.astype(BF16)
    lo = (x - hi.astype(F32)).astype(BF16)
    return hi, lo


def _rmsnorm(x, g):
    return x * lax.rsqrt(jnp.mean(x * x, axis=-1, keepdims=True) + RMS_EPS) * g


def _sigmoid(x):
    return 1.0 / (1.0 + jnp.exp(-x))


def _gelu(x):
    return 0.5 * x * (1.0 + jnp.tanh(math.sqrt(2.0 / math.pi) * (x + 0.044715 * (x * x * x))))


def _proj_kernel(x_ref, g_ref, w_ref, o_ref):
    h = _rmsnorm(x_ref[...], g_ref[...]).astype(BF16)
    o_ref[...] = _dot(h, w_ref[...]).astype(o_ref.dtype)


def _proj(x, g, w, out_dtype, tm, tn):
    t, d = x.shape
    nc = w.shape[1]
    return pl.pallas_call(
        _proj_kernel,
        grid=(nc // tn, t // tm),
        in_specs=[pl.BlockSpec((tm, d), lambda j, i: (i, 0)),
                  pl.BlockSpec((1, d), lambda j, i: (0, 0)),
                  pl.BlockSpec((d, tn), lambda j, i: (0, j))],
        out_specs=pl.BlockSpec((tm, tn), lambda j, i: (i, j)),
        out_shape=jax.ShapeDtypeStruct((t, nc), out_dtype),
        compiler_params=_cparams(("arbitrary", "arbitrary")),
        name="proj",
    )(x, g, w)


def _head_sum(x, e_ref):
    c = x.shape[0]
    q = 4 * HEAD_DIM
    xs = jnp.concatenate([x[:, i * q:(i + 1) * q] for i in range(D_MODEL // q)], axis=0)
    r = _dot(xs.astype(BF16), e_ref[...])
    return jnp.concatenate([r[i * c:(i + 1) * c] for i in range(D_MODEL // q)], axis=1)


def _rwkv_kernel(cols_ref, next_ref, pr_ref, pk_ref, pv_ref, plo_ref, s0_ref,
                 vec_ref, mulo_ref, lw_ref, e_ref,
                 y_ref, sout_ref,
                 s_scr, prr, prk, prv, prlo, ops_x, aux_x, wend_x, ops_y, aux_y, wend_y, *, t_valid, nb):
    i = pl.program_id(1)
    C = CHUNK
    prev = (prr, prk, prv, prlo)
    buf_x, buf_y = (ops_x, aux_x, wend_x), (ops_y, aux_y, wend_y)

    def col_views(ref, rows):
        return ([ref.at[:, rows, j * D_MODEL:(j + 1) * D_MODEL] for j in range(3)]
                + [ref.at[:, rows, 3 * D_MODEL:RWKV_COLS]])

    def prep(ref, rows, chunk, buf):
        return _rwkv_prep(col_views(ref, rows), prev, vec_ref, mulo_ref, lw_ref, e_ref, buf,
                          t_valid=t_valid, chunk=chunk, nb=nb)

    def mix(buf, rows):
        return _rwkv_mix(buf, s_scr, y_ref.at[:, rows, :], vec_ref, e_ref, nb=nb)

    def emit(main, side):
        for _ in main:
            next(side, None)
        for _ in side:
            pass

    @pl.when(i == 0)
    def _init():
        s_scr[...] = s0_ref[...]
        prr[...] = pr_ref[...]
        prk[...] = pk_ref[...]
        prv[...] = pv_ref[...]
        prlo[...] = plo_ref[...]
        emit(prep(cols_ref, slice(0, C), 0, buf_x), iter(()))

    emit(mix(buf_x, slice(0, C)), prep(cols_ref, slice(C, 2 * C), 2 * i + 1, buf_y))
    emit(mix(buf_y, slice(C, 2 * C)), prep(next_ref, slice(0, C), 2 * i + 2, buf_x))

    @pl.when(i == pl.num_programs(1) - 1)
    def _fin():
        sout_ref[...] = s_scr[...]


def _rwkv_prep(col_refs, prev, vec_ref, mulo_ref, lw_ref, e_ref, buf, *, t_valid, chunk, nb):
    r_ref, k_ref, v_ref, lo_ref = col_refs
    prr, prk, prv, prlo = prev
    ops, aux, wend = buf
    C = CHUNK
    R = nb * C

    def vec(i):
        return vec_ref[i:i + 1, :]

    row = lax.broadcasted_iota(jnp.int32, (R, 1), 0)
    tpos = row & (C - 1)
    seq_rows = [slice(b * C, (b + 1) * C) for b in range(nb)]

    def per_seq_rows(rows):
        out = rows[nb - 1]
        for b in reversed(range(nb - 1)):
            out = jnp.where(row < (b + 1) * C, rows[b], out)
        return out

    def mixed(ref, prev, mu):
        x = ref[...].astype(F32).reshape(R, ref.shape[-1])
        shifted = jnp.where(tpos == 0, per_seq_rows([prev[b] for b in range(nb)]), pltpu.roll(x, 1, 0))
        for b in range(nb):
            prev[b] = x[(b + 1) * C - 1:(b + 1) * C, :]
        return x + (shifted - x) * mu

    r = mixed(r_ref, prr, vec(0))
    k = mixed(k_ref, prk, vec(1))
    v = mixed(v_ref, prv, vec(2))
    lo = mixed(lo_ref, prlo, mulo_ref[...])
    yield

    lane_lo = lax.broadcasted_iota(jnp.int32, lo.shape, 1)
    act = jnp.where(lane_lo < 64, jnp.tanh(lo), jnp.where(lane_lo < 128, lo, _sigmoid(lo))).astype(BF16)
    z = vec(3) + _dot(act, lw_ref[0])
    a = _sigmoid(vec(4) + _dot(act, lw_ref[1]))
    g = _dot(act, lw_ref[2])
    logw = -DECAY_SCALE * _sigmoid(z)
    yield

    kk = k * vec(5)
    kk = kk * lax.rsqrt(jnp.maximum(_head_sum(kk * kk, e_ref), 1e-24))
    k = k * (1.0 + (a - 1.0) * vec(6))
    bonus = _head_sum(r * k * vec(7), e_ref)
    yield

    if t_valid is not None:
        valid = chunk * C + tpos < t_valid
        logw = jnp.where(valid, logw, 0.0)
        kk = jnp.where(valid, kk, 0.0)
        k = jnp.where(valid, k, 0.0)
        v = jnp.where(valid, v, 0.0)

    ti = lax.broadcasted_iota(jnp.int32, (R, R), 0)
    tj = lax.broadcasted_iota(jnp.int32, (R, R), 1)
    tril = jnp.where((tj <= ti) & (tj >= ti - (ti & (C - 1))), 1.0, 0.0).astype(BF16)
    lw_hi, lw_lo = _split(logw)
    cum = _dot(jnp.concatenate([tril, tril], axis=1), jnp.concatenate([lw_hi, lw_lo], axis=0))
    e_inc = jnp.exp(cum)
    e_dec = jnp.exp(-cum)
    yield
    a_t = -kk * jnp.exp(cum - logw)
    r_t = r * e_inc
    b_t = kk * a * e_dec
    k_t = k * e_dec
    w_end = [e_inc[(b + 1) * C - 1:(b + 1) * C, :] for b in range(nb)]
    w_end_rows = per_seq_rows(w_end)
    for j, op in enumerate((a_t, r_t, b_t, k_t, b_t * w_end_rows, k_t * w_end_rows, v)):
        ops[j] = op.astype(BF16)
    aux[0] = bonus * v
    aux[1] = g
    for b in range(nb):
        wend[b] = w_end[b]


N_OPS = 7


def _rwkv_mix(buf, s_scr, y_ref, vec_ref, e_ref, *, nb):
    ops, aux, wend = buf
    C = CHUNK
    seq_rows = [slice(b * C, (b + 1) * C) for b in range(nb)]

    def vec(i):
        return vec_ref[i:i + 1, :]

    lane = lax.broadcasted_iota(jnp.int32, (C, PAIR), 1)
    first = lane < HEAD_DIM
    t_row = lax.broadcasted_iota(jnp.int32, (C, PAIR), 0)
    s_lane = lane & (HEAD_DIM - 1)
    strict = s_lane < t_row
    incl = s_lane <= t_row
    bi = lax.broadcasted_iota(jnp.int32, (PAIR, PAIR), 0)
    bj = lax.broadcasted_iota(jnp.int32, (PAIR, PAIR), 1)
    same_head = (bi < HEAD_DIM) == (bj < HEAD_DIM)

    def blockdiag(x):
        z = jnp.zeros_like(x)
        return jnp.concatenate([jnp.where(first, x, z), jnp.where(first, z, x)], axis=0)

    def mm3(x, y):
        xh, xl = _split(x)
        yh, yl = _split(y)
        yh, yl = blockdiag(yh), blockdiag(yl)
        w = jnp.concatenate([jnp.concatenate([yh, yl], axis=1),
                             jnp.concatenate([yh, jnp.zeros_like(yh)], axis=1)], axis=0)
        res = _dot(jnp.concatenate([xh, xl], axis=1), w)
        return res[:, :PAIR] + res[:, PAIR:]

    def mm(x, y, passes):
        if passes == 1:
            return _dot(x.astype(BF16), blockdiag(y.astype(BF16)))
        return mm3(x, y)

    yield
    units = [(b, p) for b in range(nb) for p in range(N_PAIRS)]
    U = range(len(units))
    cut = [(seq_rows[b], slice(p * PAIR, (p + 1) * PAIR)) for b, p in units]
    op = lambda j, u: ops[(j,) + cut[u]]
    lhs = [jnp.concatenate([op(0, u), op(1, u)], axis=0) for u in U]
    rhs = [jnp.concatenate([blockdiag(op(2, u)), blockdiag(op(3, u))], axis=0) for u in U]
    pm = [_dot_nt(lhs[u], rhs[u]) for u in U]
    yield
    m = [jnp.where(strict, pm[u][0:C, 0:PAIR], 0.0) for u in U]
    akrk = [jnp.concatenate([jnp.where(strict, pm[u][0:C, PAIR:], 0.0),
                             jnp.where(incl, pm[u][C:, PAIR:], 0.0)], axis=0).astype(BF16) for u in U]
    rb = [jnp.where(incl, pm[u][C:, 0:PAIR], 0.0).astype(BF16) for u in U]
    yield
    def same_block(n):
        return (t_row // n) == (s_lane // n)

    d = [jnp.where(same_block(INV_BASE), m[u], 0.0) for u in U]
    xr = d
    for _ in range(int(math.log2(INV_BASE)) - 1):
        d = [mm(d[u], d[u], PASSES_INVERSE) for u in U]
        xr = [xr[u] + d[u] + mm(xr[u], d[u], PASSES_INVERSE) for u in U]
        yield
    n = INV_BASE
    while n < C:
        off = [jnp.where(same_block(2 * n) & ~same_block(n), m[u], 0.0) for u in U]
        w = [off[u] + mm(off[u], xr[u], PASSES_INVERSE) for u in U]
        xr = [xr[u] + w[u] + mm(xr[u], w[u], PASSES_INVERSE) for u in U]
        n *= 2
        yield
    s2 = [s_scr[b, p] for b, p in units]
    as_rs = [_dot_nt(lhs[u], s2[u].astype(BF16)) for u in U]
    vb = [op(6, u) for u in U]
    pv = [_dot(akrk[u], blockdiag(vb[u])) for u in U]
    yield
    rhs0 = [as_rs[u][0:C] + pv[u][0:C] for u in U]
    sab = [(rhs0[u] + mm(xr[u], rhs0[u], 1)).astype(BF16) for u in U]
    yield
    y_units = [as_rs[u][C:] + pv[u][C:] + _dot(rb[u], blockdiag(sab[u])) for u in U]
    bhkh = [jnp.concatenate([op(4, u), op(5, u)], axis=0) for u in U]
    for u, (b, p) in enumerate(units):
        upd = _dot_tn(jnp.concatenate([sab[u], vb[u]], axis=0), bhkh[u])
        s_scr[b, p] = s2[u] * wend[b][:, cut[u][1]] + jnp.where(same_head, upd, 0.0)

    yield
    y = jnp.concatenate([jnp.concatenate(y_units[b * N_PAIRS:(b + 1) * N_PAIRS], axis=1) for b in range(nb)], axis=0)
    mean = _head_sum(y, e_ref) * (1.0 / HEAD_DIM)
    yield
    yc = y - mean
    var = _head_sum(yc * yc, e_ref) * (1.0 / HEAD_DIM)
    yield
    yn = yc * lax.rsqrt(var + GN_EPS) * vec(8) + vec(9)
    y_ref[...] = ((yn + aux[0]) * aux[1]).astype(y_ref.dtype).reshape(nb, C, D_MODEL)


def _rwkv(cols3, prev, s0, vec, mulo, lw, e256, t_valid):
    b, tp, _ = cols3.shape
    C = CHUNK
    nb = _pick(b, (RWKV_SEQS_PER_STEP, 1))
    n_chunks = tp // C
    rowspec = lambda w: pl.BlockSpec((nb, 1, w), lambda bi, ci: (bi, 0, 0))
    full = lambda a: pl.BlockSpec(a.shape, lambda bi, ci: (0,) * a.ndim)
    sspec = pl.BlockSpec((nb, N_PAIRS, PAIR, PAIR), lambda bi, ci: (bi, 0, 0, 0))
    operand_bufs = [pltpu.VMEM((N_OPS, nb * C, D_MODEL), BF16), pltpu.VMEM((2, nb * C, D_MODEL), F32),
                    pltpu.VMEM((nb, 1, D_MODEL), F32)]
    return pl.pallas_call(
        functools.partial(_rwkv_kernel, t_valid=t_valid if t_valid < tp else None, nb=nb),
        grid=(b // nb, n_chunks // 2),
        in_specs=[pl.BlockSpec((nb, 2 * C, RWKV_COLS), lambda bi, ci: (bi, ci, 0)),
                  pl.BlockSpec((nb, C, RWKV_COLS), lambda bi, ci: (bi, jnp.minimum(2 * ci + 2, n_chunks - 1), 0)),
                  rowspec(D_MODEL), rowspec(D_MODEL), rowspec(D_MODEL), rowspec(LORA_COLS),
                  sspec, full(vec), full(mulo), full(lw), full(e256)],
        out_specs=[pl.BlockSpec((nb, 2 * C, D_MODEL), lambda bi, ci: (bi, ci, 0)), sspec],
        out_shape=[jax.ShapeDtypeStruct((b, tp, D_MODEL), BF16),
                   jax.ShapeDtypeStruct(s0.shape, F32)],
        scratch_shapes=[pltpu.VMEM((nb, N_PAIRS, PAIR, PAIR), F32),
                        pltpu.VMEM((nb, 1, D_MODEL), F32), pltpu.VMEM((nb, 1, D_MODEL), F32),
                        pltpu.VMEM((nb, 1, D_MODEL), F32), pltpu.VMEM((nb, 1, LORA_COLS), F32)]
                       + operand_bufs + operand_bufs,
        compiler_params=_cparams(("arbitrary", "arbitrary")),
        name="rwkv7",
    )(cols3, cols3, *prev, s0, vec, mulo, lw, e256)


def _sgu_kernel(u_ref, v_ref, ln_ref, ws_ref, bs_ref, y_ref, *maybe_v_out, lc):
    u = _gelu(u_ref[...]).astype(F32)
    v = _gelu(v_ref[...]).astype(F32)
    mu = jnp.mean(v, axis=-1, keepdims=True)
    vc = v - mu
    var = jnp.mean(vc * vc, axis=-1, keepdims=True)
    v = vc * lax.rsqrt(var + LN_EPS) * ln_ref[0:1, :] + ln_ref[1:2, :]
    if maybe_v_out:
        maybe_v_out[0][...] = v
    vb = v.astype(BF16)
    n_chunks = v.shape[0] // lc
    ri = lax.broadcasted_iota(jnp.int32, (lc, lc), 0)
    ci = lax.broadcasted_iota(jnp.int32, (lc, lc), 1)
    bias = bs_ref[...]
    outs = []
    for gi in range(GMLP_GROUPS):
        gs = slice(gi * GMLP_CHUNK, (gi + 1) * GMLP_CHUNK)
        w = jnp.where(ci <= ri, ws_ref[gi], 0.0).astype(BF16)
        rhs = jnp.concatenate([vb[n * lc:(n + 1) * lc, gs] for n in range(n_chunks)], axis=1)
        sv = _dot(w, rhs)
        sv = jnp.concatenate([sv[:, n * GMLP_CHUNK:(n + 1) * GMLP_CHUNK] for n in range(n_chunks)], axis=0)
        outs.append(sv + bias[:, gs])
    sv = jnp.concatenate(outs, axis=1)
    y_ref[...] = (u * sv).astype(y_ref.dtype)


def _sgu(uv3, ln, ws, bs_full, lc, n_chunks, want_v):
    b, t, _ = uv3.shape
    tt = lc * n_chunks
    col = lambda j: pl.BlockSpec((None, tt, D_MODEL), lambda bi, ti: (bi, ti, j))
    full = lambda a: pl.BlockSpec(a.shape, lambda bi, ti: (0,) * a.ndim)
    ospec = pl.BlockSpec((None, tt, D_MODEL), lambda bi, ti: (bi, ti, 0))
    out_specs = [ospec]
    out_shape = [jax.ShapeDtypeStruct((b, t, D_MODEL), BF16)]
    if want_v:
        out_specs.append(ospec)
        out_shape.append(jax.ShapeDtypeStruct((b, t, D_MODEL), F32))
    return pl.pallas_call(
        functools.partial(_sgu_kernel, lc=lc),
        grid=(b, t // tt),
        in_specs=[col(0), col(1), full(ln), full(ws), full(bs_full)],
        out_specs=out_specs,
        out_shape=out_shape,
        compiler_params=_cparams(("arbitrary", "arbitrary")),
        name="sgu",
    )(uv3, uv3, ln, ws, bs_full)


def _xattn_kernel(q_ref, k_ref, v_ref, o_ref):
    outs = []
    for h in range(X_HEADS):
        hs = slice(h * X_HEAD_DIM, (h + 1) * X_HEAD_DIM)
        s = _dot_nt(q_ref[:, hs], k_ref[:, hs]) * (X_HEAD_DIM ** -0.5)
        s = s - jnp.max(s, axis=-1, keepdims=True)
        e = jnp.exp(s)
        p = e * (1.0 / jnp.sum(e, axis=-1, keepdims=True))
        outs.append(_dot(p.astype(BF16), v_ref[:, hs]))
    o_ref[...] = jnp.concatenate(outs, axis=1).astype(o_ref.dtype)


def _xattn(gq3, mem_k, mem_v, tq):
    b, t, _ = gq3.shape
    mspec = pl.BlockSpec((None, N_MEM, D_MODEL), lambda bi, ti: (bi, 0, 0))
    return pl.pallas_call(
        _xattn_kernel,
        grid=(b, t // tq),
        in_specs=[pl.BlockSpec((None, tq, D_MODEL), lambda bi, ti: (bi, ti, GQ_Q_BLOCK)), mspec, mspec],
        out_specs=pl.BlockSpec((None, tq, D_MODEL), lambda bi, ti: (bi, ti, 0)),
        out_shape=jax.ShapeDtypeStruct((b, t, D_MODEL), BF16),
        compiler_params=_cparams(("arbitrary", "arbitrary")),
        name="xattn",
    )(gq3, mem_k, mem_v)


def _merge_kernel(x_ref, gate_ref, ya_ref, yb_ref, yc_ref, wb_ref, wo_ref, o_ref):
    gate = _sigmoid(gate_ref[...].astype(F32))
    merged = None
    for bi, y_ref in enumerate((ya_ref, yb_ref, yc_ref)):
        term = gate[:, bi * D_MODEL:(bi + 1) * D_MODEL] * _dot(y_ref[...], wb_ref[bi])
        merged = term if merged is None else merged + term
    o_ref[...] = x_ref[...] + _dot(merged.astype(BF16), wo_ref[...])


def _merge(x, gq, ya, yb, yc, wb, wo, tm):
    t, d = x.shape
    tok = lambda w: pl.BlockSpec((tm, w), lambda i: (i, 0))
    return pl.pallas_call(
        _merge_kernel,
        grid=(t // tm,),
        in_specs=[tok(d), tok(3 * d), tok(d), tok(d), tok(d),
                  pl.BlockSpec(wb.shape, lambda i: (0, 0, 0)), pl.BlockSpec(wo.shape, lambda i: (0, 0))],
        out_specs=tok(d),
        out_shape=jax.ShapeDtypeStruct((t, d), F32),
        compiler_params=_cparams(("arbitrary",)),
        name="merge",
    )(x, gq, ya, yb, yc, wb, wo)


def _ffn_kernel(x_ref, g_ref, wu_ref, wd_ref, gf_ref, o_ref, h_scr, *, final_norm):
    f = pl.program_id(1)

    @pl.when(f == 0)
    def _init():
        x = x_ref[...]
        h_scr[...] = _rmsnorm(x, g_ref[...]).astype(BF16)
        o_ref[...] = x

    up = _dot(h_scr[...], wu_ref[...])
    act = jnp.square(jnp.maximum(up, 0.0)).astype(BF16)
    o_ref[...] += _dot(act, wd_ref[...])

    if final_norm:
        @pl.when(f == pl.num_programs(1) - 1)
        def _final():
            o_ref[...] = _rmsnorm(o_ref[...], gf_ref[...])


def _ffn(x, g, wu, wd, g_final, final_norm, tm, tf):
    t, d = x.shape
    return pl.pallas_call(
        functools.partial(_ffn_kernel, final_norm=final_norm),
        grid=(t // tm, D_FF // tf),
        in_specs=[pl.BlockSpec((tm, d), lambda i, f: (i, 0)),
                  pl.BlockSpec((1, d), lambda i, f: (0, 0)),
                  pl.BlockSpec((d, tf), lambda i, f: (0, f)),
                  pl.BlockSpec((tf, d), lambda i, f: (f, 0)),
                  pl.BlockSpec((1, d), lambda i, f: (0, 0))],
        out_specs=pl.BlockSpec((tm, d), lambda i, f: (i, 0)),
        out_shape=jax.ShapeDtypeStruct((t, d), F32),
        scratch_shapes=[pltpu.VMEM((tm, d), BF16)],
        compiler_params=_cparams(("arbitrary", "arbitrary")),
        name="ffn",
    )(x, g, wu, wd, g_final)


def _state_to_pairs(s):
    b = s.shape[0]
    s = s.reshape(b, N_PAIRS, 2, HEAD_DIM, HEAD_DIM)
    z = jnp.zeros_like(s[:, :, 0])
    top = jnp.concatenate([s[:, :, 0], z], axis=-1)
    bot = jnp.concatenate([z, s[:, :, 1]], axis=-1)
    return jnp.concatenate([top, bot], axis=-2)


def _pairs_to_state(s2):
    b = s2.shape[0]
    s0 = s2[:, :, :HEAD_DIM, :HEAD_DIM]
    s1 = s2[:, :, HEAD_DIM:, HEAD_DIM:]
    return jnp.stack([s0, s1], axis=2).reshape(b, N_HEADS, HEAD_DIM, HEAD_DIM)


def _pick(t, candidates):
    for c in candidates:
        if t % c == 0:
            return c
    return t


def _group_layer(x, b, t, mem_k, mem_v, prev_row, s0_pairs, lp):
    n_tok = b * t
    tm = _pick(n_tok, (512, 256))
    g_mix = lp["norm_mix_g"]
    cols3 = _proj(x, g_mix, lp["w_in_rwkv"], BF16, tm, RWKV_COLS).reshape(b, t, RWKV_COLS)
    uv3 = _proj(x, g_mix, lp["w_in_gmlp"], BF16, tm, 2 * D_MODEL).reshape(b, t, 2 * D_MODEL)
    gq = _proj(x, g_mix, lp["w_in_gq"], BF16, tm, 4 * D_MODEL)
    gq3 = gq.reshape(b, t, 4 * D_MODEL)
    new_row = cols3[:, t - 1:t, :].astype(F32)

    tp = -(-t // (2 * CHUNK)) * 2 * CHUNK
    cols3p = cols3 if tp == t else jnp.pad(cols3, ((0, 0), (0, tp - t), (0, 0)))
    prev = (prev_row[..., :D_MODEL], prev_row[..., D_MODEL:2 * D_MODEL], prev_row[..., 2 * D_MODEL:3 * D_MODEL],
            prev_row[..., 3 * D_MODEL:])
    y_a, s_new = _rwkv(cols3p, prev, s0_pairs, lp["rwkv_vec"], lp["rwkv_mu_lo"], lp["rwkv_lw"], lp["e256"], t)
    y_a = y_a[:, :t].reshape(n_tok, D_MODEL)

    lc = min(t, GMLP_CHUNK)
    n_chunks = _pick(t // lc, (8, 4, 2, 1))
    ws = lp["sgu_w_s"][:, :lc, :lc]
    bs_full = jnp.tile(jnp.repeat(lp["sgu_b_s"][:, :lc].T, GMLP_CHUNK, axis=1), (n_chunks, 1))
    sgu_out = _sgu(uv3, lp["sgu_ln"], ws, bs_full, lc, n_chunks, want_v=t < GMLP_CHUNK)
    y_b = sgu_out[0].reshape(n_tok, D_MODEL)
    v_rows = sgu_out[1] if t < GMLP_CHUNK else None

    y_c = _xattn(gq3, mem_k, mem_v, _pick(t, (1024, 512, 256, 128))).reshape(n_tok, D_MODEL)

    x = _merge(x, gq, y_a, y_b, y_c, lp["w_branch"], lp["w_out"], _pick(n_tok, (512, 256)))
    x = _ffn(x, lp["norm_ffn_g"], lp["w_ffn_up"], lp["w_ffn_down"], lp["norm_final_g"], lp["is_last"],
             _pick(n_tok, (1024, 512, 256)), 2048)
    return x, new_row, s_new, v_rows


def kernel(x_prompt, x_sample, cache_mem_k, cache_mem_v, state_wkv, state_shift, mem_prompt, norm_mix_g, norm_mem_g, norm_ffn_g, norm_final_g, w_in, w_mem_kv, rwkv_mu, rwkv_w0, rwkv_w2, rwkv_a0, rwkv_a2, rwkv_g2, rwkv_k_k, rwkv_k_a, rwkv_r_k, rwkv_lnx_g, rwkv_lnx_b, sgu_ln_g, sgu_ln_b, sgu_w_s, sgu_b_s, w_branch, w_out, w_ffn_up, w_ffn_down):
    depth = w_in.shape[0]
    bp, tp, _ = x_prompt.shape
    bs, ts, _ = x_sample.shape

    w_in_rwkv = w_in[..., :RWKV_COLS].astype(BF16)
    w_in_gmlp = w_in[..., RWKV_COLS:COL_Q_START].astype(BF16)
    w_in_gq = jnp.concatenate([w_in[..., COL_Q_START + D_MODEL:], w_in[..., COL_Q_START:COL_Q_START + D_MODEL]],
                              axis=-1).astype(BF16)
    w_kv_b = w_mem_kv.astype(BF16)
    w_branch_b = w_branch.astype(BF16)
    w_out_b = w_out.astype(BF16)
    w_up_b = w_ffn_up.astype(BF16)
    w_down_b = w_ffn_down.astype(BF16)
    zeros = lambda n: jnp.zeros((depth, n, D_MODEL), F32)
    lw = jnp.stack([jnp.concatenate([rwkv_w2, zeros(192)], axis=1),
                    jnp.concatenate([zeros(64), rwkv_a2, zeros(128)], axis=1),
                    jnp.concatenate([zeros(128), rwkv_g2], axis=1)], axis=1).astype(BF16)
    rwkv_vec = jnp.stack([rwkv_mu[:, :D_MODEL], rwkv_mu[:, D_MODEL:2 * D_MODEL], rwkv_mu[:, 2 * D_MODEL:3 * D_MODEL],
                          rwkv_w0, rwkv_a0, rwkv_k_k, rwkv_k_a, rwkv_r_k.reshape(depth, D_MODEL),
                          rwkv_lnx_g, rwkv_lnx_b] + [jnp.zeros((depth, D_MODEL), F32)] * 6, axis=1)
    sgu_ln = jnp.stack([sgu_ln_g, sgu_ln_b] + [jnp.zeros((depth, D_MODEL), F32)] * 6, axis=1)
    hq = 4 * HEAD_DIM
    e256 = (jnp.arange(hq)[:, None] // HEAD_DIM == jnp.arange(hq)[None, :] // HEAD_DIM).astype(BF16)

    xp = x_prompt.reshape(bp * tp, D_MODEL)
    xs = x_sample.reshape(bs * ts, D_MODEL)
    mem_flat = mem_prompt.reshape(bp * N_MEM, D_MODEL)
    prompt_row0 = jnp.zeros((bp, 1, RWKV_COLS), F32)
    prompt_s0 = jnp.zeros((bp, N_PAIRS, PAIR, PAIR), F32)

    mk_p, mv_p, wkv_p, row_p, wkv_s, row_s, v_s = [], [], [], [], [], [], []
    for l in range(depth):
        lp = {
            "norm_mix_g": norm_mix_g[l][None],
            "w_in_rwkv": w_in_rwkv[l], "w_in_gmlp": w_in_gmlp[l], "w_in_gq": w_in_gq[l],
            "rwkv_vec": rwkv_vec[l], "rwkv_mu_lo": rwkv_mu[l][None, 3 * D_MODEL:], "rwkv_lw": lw[l], "e256": e256,
            "sgu_ln": sgu_ln[l], "sgu_w_s": sgu_w_s[l], "sgu_b_s": sgu_b_s[l],
            "w_branch": w_branch_b[l], "w_out": w_out_b[l],
            "norm_ffn_g": norm_ffn_g[l][None], "w_ffn_up": w_up_b[l], "w_ffn_down": w_down_b[l],
            "norm_final_g": norm_final_g[None], "is_last": l == depth - 1,
        }
        kv = _proj(mem_flat, norm_mem_g[l][None], w_kv_b[l], F32, _pick(bp * N_MEM, (512, 256)), 2 * D_MODEL)
        mem_k = kv[:, :D_MODEL].reshape(bp, N_MEM, D_MODEL)
        mem_v = kv[:, D_MODEL:].reshape(bp, N_MEM, D_MODEL)
        xp, r_p, s_p, _ = _group_layer(xp, bp, tp, mem_k.astype(BF16), mem_v.astype(BF16), prompt_row0, prompt_s0, lp)
        mk_p.append(mem_k.reshape(bp, N_MEM, X_HEADS, X_HEAD_DIM))
        mv_p.append(mem_v.reshape(bp, N_MEM, X_HEADS, X_HEAD_DIM))
        wkv_p.append(_pairs_to_state(s_p))
        row_p.append(r_p)

        ck = cache_mem_k[l].reshape(bs, N_MEM, D_MODEL).astype(BF16)
        cv = cache_mem_v[l].reshape(bs, N_MEM, D_MODEL).astype(BF16)
        xs, r_s, s_s, vr = _group_layer(xs, bs, ts, ck, cv, _permute_rwkv_prev(state_shift[l]),
                                        _state_to_pairs(state_wkv[l]), lp)
        wkv_s.append(_pairs_to_state(s_s))
        row_s.append(r_s)
        v_s.append(vr)

    y_prompt = xp.reshape(bp, tp, D_MODEL)
    y_sample = xs.reshape(bs, ts, D_MODEL)
    return (y_prompt, y_sample, jnp.stack(mk_p), jnp.stack(mv_p), jnp.stack(wkv_p), jnp.stack(row_p),
            jnp.stack(wkv_s), jnp.stack(row_s), jnp.stack(v_s))


def _permute_rwkv_prev(row):
    return row
```

```python
import functools
import math

import jax
import jax.numpy as jnp
from jax import lax
from jax.experimental import pallas as pl
from jax.experimental.pallas import tpu as pltpu

F32 = jnp.float32
BF16 = jnp.bfloat16

D_MODEL = 1024
HEAD_DIM = 64
N_HEADS = D_MODEL // HEAD_DIM
PAIR = 2 * HEAD_DIM
N_PAIRS = D_MODEL // PAIR
DECAY_LORA, ICLR_LORA, GATE_LORA = 64, 64, 128
LORA_COLS = DECAY_LORA + ICLR_LORA + GATE_LORA
RWKV_COLS = 3 * D_MODEL + LORA_COLS
GMLP_CHUNK = 128
GMLP_GROUPS = 8
N_MEM = 256
X_HEADS = 4
X_HEAD_DIM = D_MODEL // X_HEADS
D_FF = 4 * D_MODEL
COL_Q_START = RWKV_COLS + 2 * D_MODEL
GQ_Q_BLOCK = 3
RMS_EPS = 1e-6
LN_EPS = 1e-5
GN_EPS = 64e-5
CHUNK = 64
DECAY_SCALE = math.exp(-0.5)
VMEM_LIMIT = 56 * 1024 * 1024
INV_BASE = 8
RWKV_SEQS_PER_STEP = 2


def _cparams(sem):
    return pltpu.CompilerParams(dimension_semantics=sem, vmem_limit_bytes=VMEM_LIMIT)


def _dot(a, b):
    return jnp.dot(a, b, preferred_element_type=F32)


def _dot_nt(a, b):
    return lax.dot_general(a, b, (((1,), (1,)), ((), ())), preferred_element_type=F32)


def _dot_tn(a, b):
    return lax.dot_general(a, b, (((0,), (0,)), ((), ())), preferred_element_type=F32)


def _split(x):
    hi = x.astype(BF16)
    lo = (x - hi.astype(F32)).astype(BF16)
    return hi, lo


def _rmsnorm(x, g):
    return x * lax.rsqrt(jnp.mean(x * x, axis=-1, keepdims=True) + RMS_EPS) * g


def _sigmoid(x):
    return 1.0 / (1.0 + jnp.exp(-x))


def _gelu(x):
    return 0.5 * x * (1.0 + jnp.tanh(math.sqrt(2.0 / math.pi) * (x + 0.044715 * (x * x * x))))


def _proj_kernel(x_ref, g_ref, w_ref, o_ref):
    h = _rmsnorm(x_ref[...], g_ref[...]).astype(BF16)
    o_ref[...] = _dot(h, w_ref[...]).astype(o_ref.dtype)


def _proj(x, g, w, out_dtype, tm, tn):
    t, d = x.shape
    nc = w.shape[1]
    return pl.pallas_call(
        _proj_kernel,
        grid=(nc // tn, t // tm),
        in_specs=[pl.BlockSpec((tm, d), lambda j, i: (i, 0)),
                  pl.BlockSpec((1, d), lambda j, i: (0, 0)),
                  pl.BlockSpec((d, tn), lambda j, i: (0, j))],
        out_specs=pl.BlockSpec((tm, tn), lambda j, i: (i, j)),
        out_shape=jax.ShapeDtypeStruct((t, nc), out_dtype),
        compiler_params=_cparams(("arbitrary", "arbitrary")),
        name="proj",
    )(x, g, w)


def _head_sum(x, e_ref):
    c = x.shape[0]
    q = 4 * HEAD_DIM
    xs = jnp.concatenate([x[:, i * q:(i + 1) * q] for i in range(D_MODEL // q)], axis=0)
    r = _dot(xs.astype(BF16), e_ref[...])
    return jnp.concatenate([r[i * c:(i + 1) * c] for i in range(D_MODEL // q)], axis=1)


def _rwkv_kernel(cols_ref, next_ref, pr_ref, pk_ref, pv_ref, plo_ref, s0_ref,
                 vec_ref, mulo_ref, lw_ref, e_ref,
                 y_ref, sout_ref,
                 s_scr, prr, prk, prv, prlo, ops_x, aux_x, wend_x, ops_y, aux_y, wend_y, *, t_valid, nb):
    i = pl.program_id(1)
    C = CHUNK
    prev = (prr, prk, prv, prlo)
    buf_x, buf_y = (ops_x, aux_x, wend_x), (ops_y, aux_y, wend_y)

    def col_views(ref, rows):
        return ([ref.at[:, rows, j * D_MODEL:(j + 1) * D_MODEL] for j in range(3)]
                + [ref.at[:, rows, 3 * D_MODEL:RWKV_COLS]])

    def prep(ref, rows, chunk, buf):
        return _rwkv_prep(col_views(ref, rows), prev, vec_ref, mulo_ref, lw_ref, e_ref, buf,
                          t_valid=t_valid, chunk=chunk, nb=nb)

    def mix(buf, rows):
        return _rwkv_mix(buf, s_scr, y_ref.at[:, rows, :], vec_ref, e_ref, nb=nb)

    def emit(main, side):
        for _ in main:
            next(side, None)
        for _ in side:
            pass

    @pl.when(i == 0)
    def _init():
        s_scr[...] = s0_ref[...]
        prr[...] = pr_ref[...]
        prk[...] = pk_ref[...]
        prv[...] = pv_ref[...]
        prlo[...] = plo_ref[...]
        emit(prep(cols_ref, slice(0, C), 0, buf_x), iter(()))

    emit(mix(buf_x, slice(0, C)), prep(cols_ref, slice(C, 2 * C), 2 * i + 1, buf_y))
    emit(mix(buf_y, slice(C, 2 * C)), prep(next_ref, slice(0, C), 2 * i + 2, buf_x))

    @pl.when(i == pl.num_programs(1) - 1)
    def _fin():
        sout_ref[...] = s_scr[...]


def _rwkv_prep(col_refs, prev, vec_ref, mulo_ref, lw_ref, e_ref, buf, *, t_valid, chunk, nb):
    r_ref, k_ref, v_ref, lo_ref = col_refs
    prr, prk, prv, prlo = prev
    ops, aux, wend = buf
    C = CHUNK
    R = nb * C

    def vec(i):
        return vec_ref[i:i + 1, :]

    row = lax.broadcasted_iota(jnp.int32, (R, 1), 0)
    tpos = row & (C - 1)
    seq_rows = [slice(b * C, (b + 1) * C) for b in range(nb)]

    def per_seq_rows(rows):
        out = rows[nb - 1]
        for b in reversed(range(nb - 1)):
            out = jnp.where(row < (b + 1) * C, rows[b], out)
        return out

    def mixed(ref, prev, mu):
        x = ref[...].astype(F32).reshape(R, ref.shape[-1])
        shifted = jnp.where(tpos == 0, per_seq_rows([prev[b] for b in range(nb)]), pltpu.roll(x, 1, 0))
        for b in range(nb):
            prev[b] = x[(b + 1) * C - 1:(b + 1) * C, :]
        return x + (shifted - x) * mu

    r = mixed(r_ref, prr, vec(0))
    k = mixed(k_ref, prk, vec(1))
    v = mixed(v_ref, prv, vec(2))
    lo = mixed(lo_ref, prlo, mulo_ref[...])
    yield

    lane_lo = lax.broadcasted_iota(jnp.int32, lo.shape, 1)
    act = jnp.where(lane_lo < DECAY_LORA, jnp.tanh(lo),
                    jnp.where(lane_lo < DECAY_LORA + ICLR_LORA, lo, _sigmoid(lo))).astype(BF16)
    z = vec(3) + _dot(act, lw_ref[0])
    a = _sigmoid(vec(4) + _dot(act, lw_ref[1]))
    g = _dot(act, lw_ref[2])
    logw = -DECAY_SCALE * _sigmoid(z)
    yield

    kk = k * vec(5)
    kk = kk * lax.rsqrt(jnp.maximum(_head_sum(kk * kk, e_ref), 1e-24))
    k = k * (1.0 + (a - 1.0) * vec(6))
    bonus = _head_sum(r * k * vec(7), e_ref)
    yield

    if t_valid is not None:
        valid = chunk * C + tpos < t_valid
        logw = jnp.where(valid, logw, 0.0)
        kk = jnp.where(valid, kk, 0.0)
        k = jnp.where(valid, k, 0.0)
        v = jnp.where(valid, v, 0.0)

    ti = lax.broadcasted_iota(jnp.int32, (R, R), 0)
    tj = lax.broadcasted_iota(jnp.int32, (R, R), 1)
    tril = jnp.where((tj <= ti) & (tj >= ti - (ti & (C - 1))), 1.0, 0.0).astype(BF16)
    lw_hi, lw_lo = _split(logw)
    cum = _dot(jnp.concatenate([tril, tril], axis=1), jnp.concatenate([lw_hi, lw_lo], axis=0))
    e_inc = jnp.exp(cum)
    e_dec = jnp.exp(-cum)
    yield
    a_t = -kk * jnp.exp(cum - logw)
    r_t = r * e_inc
    b_t = kk * a * e_dec
    k_t = k * e_dec
    w_end = [e_inc[(b + 1) * C - 1:(b + 1) * C, :] for b in range(nb)]
    w_end_rows = per_seq_rows(w_end)
    for j, op in enumerate((a_t, r_t, b_t, k_t, b_t * w_end_rows, k_t * w_end_rows, v)):
        ops[j] = op.astype(BF16)
    aux[0] = bonus * v
    aux[1] = g
    for b in range(nb):
        wend[b] = w_end[b]


N_OPS = 7


def _rwkv_mix(buf, s_scr, y_ref, vec_ref, e_ref, *, nb):
    ops, aux, wend = buf
    C = CHUNK
    seq_rows = [slice(b * C, (b + 1) * C) for b in range(nb)]

    def vec(i):
        return vec_ref[i:i + 1, :]

    lane = lax.broadcasted_iota(jnp.int32, (C, PAIR), 1)
    first = lane < HEAD_DIM
    t_row = lax.broadcasted_iota(jnp.int32, (C, PAIR), 0)
    s_lane = lane & (HEAD_DIM - 1)
    strict = s_lane < t_row
    incl = s_lane <= t_row
    bi = lax.broadcasted_iota(jnp.int32, (PAIR, PAIR), 0)
    bj = lax.broadcasted_iota(jnp.int32, (PAIR, PAIR), 1)
    same_head = (bi < HEAD_DIM) == (bj < HEAD_DIM)

    def blockdiag(x):
        z = jnp.zeros_like(x)
        return jnp.concatenate([jnp.where(first, x, z), jnp.where(first, z, x)], axis=0)

    def mm(x, y):
        return _dot(x.astype(BF16), blockdiag(y.astype(BF16)))

    yield
    units = [(b, p) for b in range(nb) for p in range(N_PAIRS)]
    U = range(len(units))
    cut = [(seq_rows[b], slice(p * PAIR, (p + 1) * PAIR)) for b, p in units]
    op = lambda j, u: ops[(j,) + cut[u]]
    lhs = [jnp.concatenate([op(0, u), op(1, u)], axis=0) for u in U]
    rhs = [jnp.concatenate([blockdiag(op(2, u)), blockdiag(op(3, u))], axis=0) for u in U]
    pm = [_dot_nt(lhs[u], rhs[u]) for u in U]
    yield
    m = [jnp.where(strict, pm[u][0:C, 0:PAIR], 0.0) for u in U]
    akrk = [jnp.concatenate([jnp.where(strict, pm[u][0:C, PAIR:], 0.0),
                             jnp.where(incl, pm[u][C:, PAIR:], 0.0)], axis=0).astype(BF16) for u in U]
    rb = [jnp.where(incl, pm[u][C:, 0:PAIR], 0.0).astype(BF16) for u in U]
    yield
    def same_block(n):
        return (t_row // n) == (s_lane // n)

    d = [jnp.where(same_block(INV_BASE), m[u], 0.0) for u in U]
    xr = d
    for _ in range(int(math.log2(INV_BASE)) - 1):
        d = [mm(d[u], d[u]) for u in U]
        xr = [xr[u] + d[u] + mm(xr[u], d[u]) for u in U]
        yield
    n = INV_BASE
    while n < C:
        off = [jnp.where(same_block(2 * n) & ~same_block(n), m[u], 0.0) for u in U]
        w = [off[u] + mm(off[u], xr[u]) for u in U]
        xr = [xr[u] + w[u] + mm(xr[u], w[u]) for u in U]
        n *= 2
        yield
    s2 = [s_scr[b, p] for b, p in units]
    as_rs = [_dot_nt(lhs[u], s2[u].astype(BF16)) for u in U]
    vb = [op(6, u) for u in U]
    pv = [_dot(akrk[u], blockdiag(vb[u])) for u in U]
    yield
    rhs0 = [as_rs[u][0:C] + pv[u][0:C] for u in U]
    sab = [(rhs0[u] + mm(xr[u], rhs0[u])).astype(BF16) for u in U]
    yield
    y_units = [as_rs[u][C:] + pv[u][C:] + _dot(rb[u], blockdiag(sab[u])) for u in U]
    bhkh = [jnp.concatenate([op(4, u), op(5, u)], axis=0) for u in U]
    for u, (b, p) in enumerate(units):
        upd = _dot_tn(jnp.concatenate([sab[u], vb[u]], axis=0), bhkh[u])
        s_scr[b, p] = s2[u] * wend[b][:, cut[u][1]] + jnp.where(same_head, upd, 0.0)

    yield
    y = jnp.concatenate([jnp.concatenate(y_units[b * N_PAIRS:(b + 1) * N_PAIRS], axis=1) for b in range(nb)], axis=0)
    mean = _head_sum(y, e_ref) * (1.0 / HEAD_DIM)
    yield
    yc = y - mean
    var = _head_sum(yc * yc, e_ref) * (1.0 / HEAD_DIM)
    yield
    yn = yc * lax.rsqrt(var + GN_EPS) * vec(8) + vec(9)
    y_ref[...] = ((yn + aux[0]) * aux[1]).astype(y_ref.dtype).reshape(nb, C, D_MODEL)


def _rwkv(cols3, prev, s0, vec, mulo, lw, e256, t_valid):
    b, tp, _ = cols3.shape
    C = CHUNK
    nb = _pick(b, (RWKV_SEQS_PER_STEP, 1))
    n_chunks = tp // C
    rowspec = lambda w: pl.BlockSpec((nb, 1, w), lambda bi, ci: (bi, 0, 0))
    full = lambda a: pl.BlockSpec(a.shape, lambda bi, ci: (0,) * a.ndim)
    sspec = pl.BlockSpec((nb, N_PAIRS, PAIR, PAIR), lambda bi, ci: (bi, 0, 0, 0))
    operand_bufs = [pltpu.VMEM((N_OPS, nb * C, D_MODEL), BF16), pltpu.VMEM((2, nb * C, D_MODEL), F32),
                    pltpu.VMEM((nb, 1, D_MODEL), F32)]
    return pl.pallas_call(
        functools.partial(_rwkv_kernel, t_valid=t_valid if t_valid < tp else None, nb=nb),
        grid=(b // nb, n_chunks // 2),
        in_specs=[pl.BlockSpec((nb, 2 * C, RWKV_COLS), lambda bi, ci: (bi, ci, 0)),
                  pl.BlockSpec((nb, C, RWKV_COLS), lambda bi, ci: (bi, jnp.minimum(2 * ci + 2, n_chunks - 1), 0)),
                  rowspec(D_MODEL), rowspec(D_MODEL), rowspec(D_MODEL), rowspec(LORA_COLS),
                  sspec, full(vec), full(mulo), full(lw), full(e256)],
        out_specs=[pl.BlockSpec((nb, 2 * C, D_MODEL), lambda bi, ci: (bi, ci, 0)), sspec],
        out_shape=[jax.ShapeDtypeStruct((b, tp, D_MODEL), BF16),
                   jax.ShapeDtypeStruct(s0.shape, F32)],
        scratch_shapes=[pltpu.VMEM((nb, N_PAIRS, PAIR, PAIR), F32),
                        pltpu.VMEM((nb, 1, D_MODEL), F32), pltpu.VMEM((nb, 1, D_MODEL), F32),
                        pltpu.VMEM((nb, 1, D_MODEL), F32), pltpu.VMEM((nb, 1, LORA_COLS), F32)]
                       + operand_bufs + operand_bufs,
        compiler_params=_cparams(("arbitrary", "arbitrary")),
        name="rwkv7",
    )(cols3, cols3, *prev, s0, vec, mulo, lw, e256)


def _sgu_kernel(u_ref, v_ref, ln_ref, ws_ref, bs_ref, y_ref, *maybe_v_out, lc):
    u = _gelu(u_ref[...]).astype(F32)
    v = _gelu(v_ref[...]).astype(F32)
    mu = jnp.mean(v, axis=-1, keepdims=True)
    vc = v - mu
    var = jnp.mean(vc * vc, axis=-1, keepdims=True)
    v = vc * lax.rsqrt(var + LN_EPS) * ln_ref[0:1, :] + ln_ref[1:2, :]
    if maybe_v_out:
        maybe_v_out[0][...] = v
    vb = v.astype(BF16)
    n_chunks = v.shape[0] // lc
    ri = lax.broadcasted_iota(jnp.int32, (lc, lc), 0)
    ci = lax.broadcasted_iota(jnp.int32, (lc, lc), 1)
    bias = bs_ref[...]
    outs = []
    for gi in range(GMLP_GROUPS):
        gs = slice(gi * GMLP_CHUNK, (gi + 1) * GMLP_CHUNK)
        w = jnp.where(ci <= ri, ws_ref[gi], 0.0).astype(BF16)
        rhs = jnp.concatenate([vb[n * lc:(n + 1) * lc, gs] for n in range(n_chunks)], axis=1)
        sv = _dot(w, rhs)
        sv = jnp.concatenate([sv[:, n * GMLP_CHUNK:(n + 1) * GMLP_CHUNK] for n in range(n_chunks)], axis=0)
        outs.append(sv + bias[:, gs])
    sv = jnp.concatenate(outs, axis=1)
    y_ref[...] = (u * sv).astype(y_ref.dtype)


def _sgu(uv3, ln, ws, bs_full, lc, n_chunks, want_v):
    b, t, _ = uv3.shape
    tt = lc * n_chunks
    col = lambda j: pl.BlockSpec((None, tt, D_MODEL), lambda bi, ti: (bi, ti, j))
    full = lambda a: pl.BlockSpec(a.shape, lambda bi, ti: (0,) * a.ndim)
    ospec = pl.BlockSpec((None, tt, D_MODEL), lambda bi, ti: (bi, ti, 0))
    out_specs = [ospec]
    out_shape = [jax.ShapeDtypeStruct((b, t, D_MODEL), BF16)]
    if want_v:
        out_specs.append(ospec)
        out_shape.append(jax.ShapeDtypeStruct((b, t, D_MODEL), F32))
    return pl.pallas_call(
        functools.partial(_sgu_kernel, lc=lc),
        grid=(b, t // tt),
        in_specs=[col(0), col(1), full(ln), full(ws), full(bs_full)],
        out_specs=out_specs,
        out_shape=out_shape,
        compiler_params=_cparams(("arbitrary", "arbitrary")),
        name="sgu",
    )(uv3, uv3, ln, ws, bs_full)


def _xattn_kernel(q_ref, k_ref, v_ref, o_ref):
    outs = []
    for h in range(X_HEADS):
        hs = slice(h * X_HEAD_DIM, (h + 1) * X_HEAD_DIM)
        s = _dot_nt(q_ref[:, hs], k_ref[:, hs]) * (X_HEAD_DIM ** -0.5)
        s = s - jnp.max(s, axis=-1, keepdims=True)
        e = jnp.exp(s)
        p = e * (1.0 / jnp.sum(e, axis=-1, keepdims=True))
        outs.append(_dot(p.astype(BF16), v_ref[:, hs]))
    o_ref[...] = jnp.concatenate(outs, axis=1).astype(o_ref.dtype)


def _xattn(gq3, mem_k, mem_v, tq):
    b, t, _ = gq3.shape
    mspec = pl.BlockSpec((None, N_MEM, D_MODEL), lambda bi, ti: (bi, 0, 0))
    return pl.pallas_call(
        _xattn_kernel,
        grid=(b, t // tq),
        in_specs=[pl.BlockSpec((None, tq, D_MODEL), lambda bi, ti: (bi, ti, GQ_Q_BLOCK)), mspec, mspec],
        out_specs=pl.BlockSpec((None, tq, D_MODEL), lambda bi, ti: (bi, ti, 0)),
        out_shape=jax.ShapeDtypeStruct((b, t, D_MODEL), BF16),
        compiler_params=_cparams(("arbitrary", "arbitrary")),
        name="xattn",
    )(gq3, mem_k, mem_v)


def _merge_kernel(x_ref, gate_ref, ya_ref, yb_ref, yc_ref, wb_ref, wo_ref, o_ref):
    gate = _sigmoid(gate_ref[...].astype(F32))
    merged = None
    for bi, y_ref in enumerate((ya_ref, yb_ref, yc_ref)):
        term = gate[:, bi * D_MODEL:(bi + 1) * D_MODEL] * _dot(y_ref[...], wb_ref[bi])
        merged = term if merged is None else merged + term
    o_ref[...] = x_ref[...] + _dot(merged.astype(BF16), wo_ref[...])


def _merge(x, gq, ya, yb, yc, wb, wo, tm):
    t, d = x.shape
    tok = lambda w: pl.BlockSpec((tm, w), lambda i: (i, 0))
    return pl.pallas_call(
        _merge_kernel,
        grid=(t // tm,),
        in_specs=[tok(d), tok(3 * d), tok(d), tok(d), tok(d),
                  pl.BlockSpec(wb.shape, lambda i: (0, 0, 0)), pl.BlockSpec(wo.shape, lambda i: (0, 0))],
        out_specs=tok(d),
        out_shape=jax.ShapeDtypeStruct((t, d), F32),
        compiler_params=_cparams(("arbitrary",)),
        name="merge",
    )(x, gq, ya, yb, yc, wb, wo)


def _ffn_kernel(x_ref, g_ref, wu_ref, wd_ref, gf_ref, o_ref, h_scr, *, final_norm):
    f = pl.program_id(1)

    @pl.when(f == 0)
    def _init():
        x = x_ref[...]
        h_scr[...] = _rmsnorm(x, g_ref[...]).astype(BF16)
        o_ref[...] = x

    up = _dot(h_scr[...], wu_ref[...])
    act = jnp.square(jnp.maximum(up, 0.0)).astype(BF16)
    o_ref[...] += _dot(act, wd_ref[...])

    if final_norm:
        @pl.when(f == pl.num_programs(1) - 1)
        def _final():
            o_ref[...] = _rmsnorm(o_ref[...], gf_ref[...])


def _ffn(x, g, wu, wd, g_final, final_norm, tm, tf):
    t, d = x.shape
    return pl.pallas_call(
        functools.partial(_ffn_kernel, final_norm=final_norm),
        grid=(t // tm, D_FF // tf),
        in_specs=[pl.BlockSpec((tm, d), lambda i, f: (i, 0)),
                  pl.BlockSpec((1, d), lambda i, f: (0, 0)),
                  pl.BlockSpec((d, tf), lambda i, f: (0, f)),
                  pl.BlockSpec((tf, d), lambda i, f: (f, 0)),
                  pl.BlockSpec((1, d), lambda i, f: (0, 0))],
        out_specs=pl.BlockSpec((tm, d), lambda i, f: (i, 0)),
        out_shape=jax.ShapeDtypeStruct((t, d), F32),
        scratch_shapes=[pltpu.VMEM((tm, d), BF16)],
        compiler_params=_cparams(("arbitrary", "arbitrary")),
        name="ffn",
    )(x, g, wu, wd, g_final)


def _state_to_pairs(s):
    b = s.shape[0]
    s = s.reshape(b, N_PAIRS, 2, HEAD_DIM, HEAD_DIM)
    z = jnp.zeros_like(s[:, :, 0])
    top = jnp.concatenate([s[:, :, 0], z], axis=-1)
    bot = jnp.concatenate([z, s[:, :, 1]], axis=-1)
    return jnp.concatenate([top, bot], axis=-2)


def _pairs_to_state(s2):
    b = s2.shape[0]
    s0 = s2[:, :, :HEAD_DIM, :HEAD_DIM]
    s1 = s2[:, :, HEAD_DIM:, HEAD_DIM:]
    return jnp.stack([s0, s1], axis=2).reshape(b, N_HEADS, HEAD_DIM, HEAD_DIM)


def _pick(t, candidates):
    for c in candidates:
        if t % c == 0:
            return c
    return t


def _group_layer(x, b, t, mem_k, mem_v, prev_row, s0_pairs, lp):
    n_tok = b * t
    tm = _pick(n_tok, (512, 256))
    g_mix = lp["norm_mix_g"]
    cols3 = _proj(x, g_mix, lp["w_in_rwkv"], BF16, tm, RWKV_COLS).reshape(b, t, RWKV_COLS)
    uv3 = _proj(x, g_mix, lp["w_in_gmlp"], BF16, tm, 2 * D_MODEL).reshape(b, t, 2 * D_MODEL)
    gq = _proj(x, g_mix, lp["w_in_gq"], BF16, tm, 4 * D_MODEL)
    gq3 = gq.reshape(b, t, 4 * D_MODEL)
    new_row = cols3[:, t - 1:t, :].astype(F32)

    tp = -(-t // (2 * CHUNK)) * 2 * CHUNK
    cols3p = cols3 if tp == t else jnp.pad(cols3, ((0, 0), (0, tp - t), (0, 0)))
    prev = (prev_row[..., :D_MODEL], prev_row[..., D_MODEL:2 * D_MODEL], prev_row[..., 2 * D_MODEL:3 * D_MODEL],
            prev_row[..., 3 * D_MODEL:])
    y_a, s_new = _rwkv(cols3p, prev, s0_pairs, lp["rwkv_vec"], lp["rwkv_mu_lo"], lp["rwkv_lw"], lp["e256"], t)
    y_a = y_a[:, :t].reshape(n_tok, D_MODEL)

    lc = min(t, GMLP_CHUNK)
    n_chunks = _pick(t // lc, (8, 4, 2, 1))
    ws = lp["sgu_w_s"][:, :lc, :lc]
    bs_full = jnp.tile(jnp.repeat(lp["sgu_b_s"][:, :lc].T, GMLP_CHUNK, axis=1), (n_chunks, 1))
    sgu_out = _sgu(uv3, lp["sgu_ln"], ws, bs_full, lc, n_chunks, want_v=t < GMLP_CHUNK)
    y_b = sgu_out[0].reshape(n_tok, D_MODEL)
    v_rows = sgu_out[1] if t < GMLP_CHUNK else None

    y_c = _xattn(gq3, mem_k, mem_v, _pick(t, (1024, 512, 256, 128))).reshape(n_tok, D_MODEL)

    x = _merge(x, gq, y_a, y_b, y_c, lp["w_branch"], lp["w_out"], _pick(n_tok, (512, 256)))
    x = _ffn(x, lp["norm_ffn_g"], lp["w_ffn_up"], lp["w_ffn_down"], lp["norm_final_g"], lp["is_last"],
             _pick(n_tok, (1024, 512, 256)), 2048)
    return x, new_row, s_new, v_rows


def kernel(x_prompt, x_sample, cache_mem_k, cache_mem_v, state_wkv, state_shift, mem_prompt, norm_mix_g, norm_mem_g, norm_ffn_g, norm_final_g, w_in, w_mem_kv, rwkv_mu, rwkv_w0, rwkv_w2, rwkv_a0, rwkv_a2, rwkv_g2, rwkv_k_k, rwkv_k_a, rwkv_r_k, rwkv_lnx_g, rwkv_lnx_b, sgu_ln_g, sgu_ln_b, sgu_w_s, sgu_b_s, w_branch, w_out, w_ffn_up, w_ffn_down):
    depth = w_in.shape[0]
    bp, tp, _ = x_prompt.shape
    bs, ts, _ = x_sample.shape

    w_in_rwkv = w_in[..., :RWKV_COLS].astype(BF16)
    w_in_gmlp = w_in[..., RWKV_COLS:COL_Q_START].astype(BF16)
    w_in_gq = jnp.concatenate([w_in[..., COL_Q_START + D_MODEL:], w_in[..., COL_Q_START:COL_Q_START + D_MODEL]],
                              axis=-1).astype(BF16)
    w_kv_b = w_mem_kv.astype(BF16)
    w_branch_b = w_branch.astype(BF16)
    w_out_b = w_out.astype(BF16)
    w_up_b = w_ffn_up.astype(BF16)
    w_down_b = w_ffn_down.astype(BF16)
    zeros = lambda n: jnp.zeros((depth, n, D_MODEL), F32)
    lw = jnp.stack([jnp.concatenate([rwkv_w2, zeros(ICLR_LORA + GATE_LORA)], axis=1),
                    jnp.concatenate([zeros(DECAY_LORA), rwkv_a2, zeros(GATE_LORA)], axis=1),
                    jnp.concatenate([zeros(DECAY_LORA + ICLR_LORA), rwkv_g2], axis=1)], axis=1).astype(BF16)
    rwkv_vec = jnp.stack([rwkv_mu[:, :D_MODEL], rwkv_mu[:, D_MODEL:2 * D_MODEL], rwkv_mu[:, 2 * D_MODEL:3 * D_MODEL],
                          rwkv_w0, rwkv_a0, rwkv_k_k, rwkv_k_a, rwkv_r_k.reshape(depth, D_MODEL),
                          rwkv_lnx_g, rwkv_lnx_b] + [jnp.zeros((depth, D_MODEL), F32)] * 6, axis=1)
    sgu_ln = jnp.stack([sgu_ln_g, sgu_ln_b] + [jnp.zeros((depth, D_MODEL), F32)] * 6, axis=1)
    hq = 4 * HEAD_DIM
    e256 = (jnp.arange(hq)[:, None] // HEAD_DIM == jnp.arange(hq)[None, :] // HEAD_DIM).astype(BF16)

    xp = x_prompt.reshape(bp * tp, D_MODEL)
    xs = x_sample.reshape(bs * ts, D_MODEL)
    mem_flat = mem_prompt.reshape(bp * N_MEM, D_MODEL)
    prompt_row0 = jnp.zeros((bp, 1, RWKV_COLS), F32)
    prompt_s0 = jnp.zeros((bp, N_PAIRS, PAIR, PAIR), F32)

    mk_p, mv_p, wkv_p, row_p, wkv_s, row_s, v_s = [], [], [], [], [], [], []
    for l in range(depth):
        lp = {
            "norm_mix_g": norm_mix_g[l][None],
            "w_in_rwkv": w_in_rwkv[l], "w_in_gmlp": w_in_gmlp[l], "w_in_gq": w_in_gq[l],
            "rwkv_vec": rwkv_vec[l], "rwkv_mu_lo": rwkv_mu[l][None, 3 * D_MODEL:], "rwkv_lw": lw[l], "e256": e256,
            "sgu_ln": sgu_ln[l], "sgu_w_s": sgu_w_s[l], "sgu_b_s": sgu_b_s[l],
            "w_branch": w_branch_b[l], "w_out": w_out_b[l],
            "norm_ffn_g": norm_ffn_g[l][None], "w_ffn_up": w_up_b[l], "w_ffn_down": w_down_b[l],
            "norm_final_g": norm_final_g[None], "is_last": l == depth - 1,
        }
        kv = _proj(mem_flat, norm_mem_g[l][None], w_kv_b[l], F32, _pick(bp * N_MEM, (512, 256)), 2 * D_MODEL)
        mem_k = kv[:, :D_MODEL].reshape(bp, N_MEM, D_MODEL)
        mem_v = kv[:, D_MODEL:].reshape(bp, N_MEM, D_MODEL)
        xp, r_p, s_p, _ = _group_layer(xp, bp, tp, mem_k.astype(BF16), mem_v.astype(BF16), prompt_row0, prompt_s0, lp)
        mk_p.append(mem_k.reshape(bp, N_MEM, X_HEADS, X_HEAD_DIM))
        mv_p.append(mem_v.reshape(bp, N_MEM, X_HEADS, X_HEAD_DIM))
        wkv_p.append(_pairs_to_state(s_p))
        row_p.append(r_p)

        ck = cache_mem_k[l].reshape(bs, N_MEM, D_MODEL).astype(BF16)
        cv = cache_mem_v[l].reshape(bs, N_MEM, D_MODEL).astype(BF16)
        xs, r_s, s_s, vr = _group_layer(xs, bs, ts, ck, cv, state_shift[l],
                                        _state_to_pairs(state_wkv[l]), lp)
        wkv_s.append(_pairs_to_state(s_s))
        row_s.append(r_s)
        v_s.append(vr)

    y_prompt = xp.reshape(bp, tp, D_MODEL)
    y_sample = xs.reshape(bs, ts, D_MODEL)
    return (y_prompt, y_sample, jnp.stack(mk_p), jnp.stack(mv_p), jnp.stack(wkv_p), jnp.stack(row_p),
            jnp.stack(wkv_s), jnp.stack(row_s), jnp.stack(v_s))
```

```python
import functools
import math

import jax
import jax.numpy as jnp
from jax import lax
from jax.experimental import pallas as pl
from jax.experimental.pallas import tpu as pltpu

F32 = jnp.float32
BF16 = jnp.bfloat16

D_MODEL = 1024
HEAD_DIM = 64
N_HEADS = D_MODEL // HEAD_DIM
PAIR = 2 * HEAD_DIM
N_PAIRS = D_MODEL // PAIR
DECAY_LORA, ICLR_LORA, GATE_LORA = 64, 64, 128
LORA_COLS = DECAY_LORA + ICLR_LORA + GATE_LORA
RWKV_COLS = 3 * D_MODEL + LORA_COLS
GMLP_CHUNK = 128
GMLP_GROUPS = 8
N_MEM = 256
X_HEADS = 4
X_HEAD_DIM = D_MODEL // X_HEADS
D_FF = 4 * D_MODEL
COL_Q_START = RWKV_COLS + 2 * D_MODEL
GQ_Q_BLOCK = 3
RMS_EPS = 1e-6
LN_EPS = 1e-5
GN_EPS = 64e-5
CHUNK = 64
DECAY_SCALE = math.exp(-0.5)
VMEM_LIMIT = 56 * 1024 * 1024
INV_BASE = 8
RWKV_SEQS_PER_STEP = 2


def _cparams(sem):
    return pltpu.CompilerParams(dimension_semantics=sem, vmem_limit_bytes=VMEM_LIMIT)


def _dot(a, b):
    return jnp.dot(a, b, preferred_element_type=F32)


def _dot_nt(a, b):
    return lax.dot_general(a, b, (((1,), (1,)), ((), ())), preferred_element_type=F32)


def _dot_tn(a, b):
    return lax.dot_general(a, b, (((0,), (0,)), ((), ())), preferred_element_type=F32)


def _split(x):
    hi = x.astype(BF16)
    lo = (x - hi.astype(F32)).astype(BF16)
    return hi, lo


def _rmsnorm(x, g):
    return x * lax.rsqrt(jnp.mean(x * x, axis=-1, keepdims=True) + RMS_EPS) * g


def _sigmoid(x):
    return 1.0 / (1.0 + jnp.exp(-x))


def _gelu(x):
    return 0.5 * x * (1.0 + jnp.tanh(math.sqrt(2.0 / math.pi) * (x + 0.044715 * (x * x * x))))


def _proj_kernel(x_ref, g_ref, w_ref, o_ref):
    h = _rmsnorm(x_ref[...], g_ref[...]).astype(BF16)
    o_ref[...] = _dot(h, w_ref[...]).astype(o_ref.dtype)


def _proj(x, g, w, out_dtype, tm, tn):
    t, d = x.shape
    nc = w.shape[1]
    return pl.pallas_call(
        _proj_kernel,
        grid=(nc // tn, t // tm),
        in_specs=[pl.BlockSpec((tm, d), lambda j, i: (i, 0)),
                  pl.BlockSpec((1, d), lambda j, i: (0, 0)),
                  pl.BlockSpec((d, tn), lambda j, i: (0, j))],
        out_specs=pl.BlockSpec((tm, tn), lambda j, i: (i, j)),
        out_shape=jax.ShapeDtypeStruct((t, nc), out_dtype),
        compiler_params=_cparams(("arbitrary", "arbitrary")),
        name="proj",
    )(x, g, w)


def _head_sum(x, e_ref):
    c = x.shape[0]
    q = 4 * HEAD_DIM
    xs = jnp.concatenate([x[:, i * q:(i + 1) * q] for i in range(D_MODEL // q)], axis=0)
    r = _dot(xs.astype(BF16), e_ref[...])
    return jnp.concatenate([r[i * c:(i + 1) * c] for i in range(D_MODEL // q)], axis=1)


def _rwkv_kernel(cols_ref, next_ref, pr_ref, pk_ref, pv_ref, plo_ref, s0_ref,
                 vec_ref, mulo_ref, lw_ref, e_ref,
                 y_ref, sout_ref,
                 s_scr, prr, prk, prv, prlo, ops_x, aux_x, wend_x, ops_y, aux_y, wend_y, *, t_valid, nb):
    i = pl.program_id(1)
    C = CHUNK
    prev = (prr, prk, prv, prlo)
    buf_x, buf_y = (ops_x, aux_x, wend_x), (ops_y, aux_y, wend_y)

    def col_views(ref, rows):
        return ([ref.at[:, rows, j * D_MODEL:(j + 1) * D_MODEL] for j in range(3)]
                + [ref.at[:, rows, 3 * D_MODEL:RWKV_COLS]])

    def prep(ref, rows, chunk, buf):
        return _rwkv_prep(col_views(ref, rows), prev, vec_ref, mulo_ref, lw_ref, e_ref, buf,
                          t_valid=t_valid, chunk=chunk, nb=nb)

    def mix(buf, rows):
        return _rwkv_mix(buf, s_scr, y_ref.at[:, rows, :], vec_ref, e_ref, nb=nb)

    def emit(main, side):
        for _ in main:
            next(side, None)
        for _ in side:
            pass

    @pl.when(i == 0)
    def _init():
        s_scr[...] = s0_ref[...]
        prr[...] = pr_ref[...]
        prk[...] = pk_ref[...]
        prv[...] = pv_ref[...]
        prlo[...] = plo_ref[...]
        emit(prep(cols_ref, slice(0, C), 0, buf_x), iter(()))

    emit(mix(buf_x, slice(0, C)), prep(cols_ref, slice(C, 2 * C), 2 * i + 1, buf_y))
    emit(mix(buf_y, slice(C, 2 * C)), prep(next_ref, slice(0, C), 2 * i + 2, buf_x))

    @pl.when(i == pl.num_programs(1) - 1)
    def _fin():
        sout_ref[...] = s_scr[...]


def _rwkv_prep(col_refs, prev, vec_ref, mulo_ref, lw_ref, e_ref, buf, *, t_valid, chunk, nb):
    r_ref, k_ref, v_ref, lo_ref = col_refs
    prr, prk, prv, prlo = prev
    ops, aux, wend = buf
    C = CHUNK
    R = nb * C

    def vec(i):
        return vec_ref[i:i + 1, :]

    row = lax.broadcasted_iota(jnp.int32, (R, 1), 0)
    tpos = row & (C - 1)
    seq_rows = [slice(b * C, (b + 1) * C) for b in range(nb)]

    def per_seq_rows(rows):
        out = rows[nb - 1]
        for b in reversed(range(nb - 1)):
            out = jnp.where(row < (b + 1) * C, rows[b], out)
        return out

    def mixed(ref, prev, mu):
        x = ref[...].astype(F32).reshape(R, ref.shape[-1])
        shifted = jnp.where(tpos == 0, per_seq_rows([prev[b] for b in range(nb)]), pltpu.roll(x, 1, 0))
        for b in range(nb):
            prev[b] = x[(b + 1) * C - 1:(b + 1) * C, :]
        return x + (shifted - x) * mu

    r = mixed(r_ref, prr, vec(0))
    k = mixed(k_ref, prk, vec(1))
    v = mixed(v_ref, prv, vec(2))
    lo = mixed(lo_ref, prlo, mulo_ref[...])
    yield

    lane_lo = lax.broadcasted_iota(jnp.int32, lo.shape, 1)
    act = jnp.where(lane_lo < DECAY_LORA, jnp.tanh(lo),
                    jnp.where(lane_lo < DECAY_LORA + ICLR_LORA, lo, _sigmoid(lo))).astype(BF16)
    z = vec(3) + _dot(act, lw_ref[0])
    a = _sigmoid(vec(4) + _dot(act, lw_ref[1]))
    g = _dot(act, lw_ref[2])
    logw = -DECAY_SCALE * _sigmoid(z)
    yield

    kk = k * vec(5)
    kk = kk * lax.rsqrt(jnp.maximum(_head_sum(kk * kk, e_ref), 1e-24))
    k = k * (1.0 + (a - 1.0) * vec(6))
    bonus = _head_sum(r * k * vec(7), e_ref)
    yield

    if t_valid is not None:
        valid = chunk * C + tpos < t_valid
        logw = jnp.where(valid, logw, 0.0)
        kk = jnp.where(valid, kk, 0.0)
        k = jnp.where(valid, k, 0.0)
        v = jnp.where(valid, v, 0.0)

    ti = lax.broadcasted_iota(jnp.int32, (R, R), 0)
    tj = lax.broadcasted_iota(jnp.int32, (R, R), 1)
    tril = jnp.where((tj <= ti) & (tj >= ti - (ti & (C - 1))), 1.0, 0.0).astype(BF16)
    lw_hi, lw_lo = _split(logw)
    cum = _dot(jnp.concatenate([tril, tril], axis=1), jnp.concatenate([lw_hi, lw_lo], axis=0))
    e_inc = jnp.exp(cum)
    e_dec = jnp.exp(-cum)
    yield
    a_t = -kk * jnp.exp(cum - logw)
    r_t = r * e_inc
    b_t = kk * a * e_dec
    k_t = k * e_dec
    w_end = [e_inc[(b + 1) * C - 1:(b + 1) * C, :] for b in range(nb)]
    w_end_rows = per_seq_rows(w_end)
    for j, op in enumerate((a_t, r_t, b_t, k_t, b_t * w_end_rows, k_t * w_end_rows, v)):
        ops[j] = op.astype(BF16)
    aux[0] = bonus * v
    aux[1] = g
    for b in range(nb):
        wend[b] = w_end[b]


N_OPS = 7


def _rwkv_mix(buf, s_scr, y_ref, vec_ref, e_ref, *, nb):
    ops, aux, wend = buf
    C = CHUNK
    seq_rows = [slice(b * C, (b + 1) * C) for b in range(nb)]

    def vec(i):
        return vec_ref[i:i + 1, :]

    lane = lax.broadcasted_iota(jnp.int32, (C, PAIR), 1)
    first = lane < HEAD_DIM
    t_row = lax.broadcasted_iota(jnp.int32, (C, PAIR), 0)
    s_lane = lane & (HEAD_DIM - 1)
    strict = s_lane < t_row
    incl = s_lane <= t_row
    bi = lax.broadcasted_iota(jnp.int32, (PAIR, PAIR), 0)
    bj = lax.broadcasted_iota(jnp.int32, (PAIR, PAIR), 1)
    same_head = (bi < HEAD_DIM) == (bj < HEAD_DIM)

    def blockdiag(x):
        z = jnp.zeros_like(x)
        return jnp.concatenate([jnp.where(first, x, z), jnp.where(first, z, x)], axis=0)

    def mm(x, y):
        return _dot(x.astype(BF16), blockdiag(y.astype(BF16)))

    yield
    units = [(b, p) for b in range(nb) for p in range(N_PAIRS)]
    U = range(len(units))
    cut = [(seq_rows[b], slice(p * PAIR, (p + 1) * PAIR)) for b, p in units]
    op = lambda j, u: ops[(j,) + cut[u]]
    lhs = [jnp.concatenate([op(0, u), op(1, u)], axis=0) for u in U]
    rhs = [jnp.concatenate([blockdiag(op(2, u)), blockdiag(op(3, u))], axis=0) for u in U]
    pm = [_dot_nt(lhs[u], rhs[u]) for u in U]
    yield
    m = [jnp.where(strict, pm[u][0:C, 0:PAIR], 0.0) for u in U]
    akrk = [jnp.concatenate([jnp.where(strict, pm[u][0:C, PAIR:], 0.0),
                             jnp.where(incl, pm[u][C:, PAIR:], 0.0)], axis=0).astype(BF16) for u in U]
    rb = [jnp.where(incl, pm[u][C:, 0:PAIR], 0.0).astype(BF16) for u in U]
    yield
    def same_block(n):
        return (t_row // n) == (s_lane // n)

    d = [jnp.where(same_block(INV_BASE), m[u], 0.0) for u in U]
    xr = d
    for _ in range(int(math.log2(INV_BASE)) - 1):
        d = [mm(d[u], d[u]) for u in U]
        xr = [xr[u] + d[u] + mm(xr[u], d[u]) for u in U]
        yield
    n = INV_BASE
    while n < C:
        off = [jnp.where(same_block(2 * n) & ~same_block(n), m[u], 0.0) for u in U]
        w = [off[u] + mm(off[u], xr[u]) for u in U]
        xr = [xr[u] + w[u] + mm(xr[u], w[u]) for u in U]
        n *= 2
        yield
    s2 = [s_scr[b, p] for b, p in units]
    as_rs = [_dot_nt(lhs[u], s2[u].astype(BF16)) for u in U]
    vb = [op(6, u) for u in U]
    pv = [_dot(akrk[u], blockdiag(vb[u])) for u in U]
    yield
    rhs0 = [as_rs[u][0:C] + pv[u][0:C] for u in U]
    sab = [(rhs0[u] + mm(xr[u], rhs0[u])).astype(BF16) for u in U]
    yield
    y_units = [as_rs[u][C:] + pv[u][C:] + _dot(rb[u], blockdiag(sab[u])) for u in U]
    bhkh = [jnp.concatenate([op(4, u), op(5, u)], axis=0) for u in U]
    for u, (b, p) in enumerate(units):
        upd = _dot_tn(jnp.concatenate([sab[u], vb[u]], axis=0), bhkh[u])
        s_scr[b, p] = s2[u] * wend[b][:, cut[u][1]] + jnp.where(same_head, upd, 0.0)

    yield
    y = jnp.concatenate([jnp.concatenate(y_units[b * N_PAIRS:(b + 1) * N_PAIRS], axis=1) for b in range(nb)], axis=0)
    mean = _head_sum(y, e_ref) * (1.0 / HEAD_DIM)
    yield
    yc = y - mean
    var = _head_sum(yc * yc, e_ref) * (1.0 / HEAD_DIM)
    yield
    yn = yc * lax.rsqrt(var + GN_EPS) * vec(8) + vec(9)
    y_ref[...] = ((yn + aux[0]) * aux[1]).astype(y_ref.dtype).reshape(nb, C, D_MODEL)


def _rwkv(cols3, prev, s0, vec, mulo, lw, e256, t_valid):
    b, tp, _ = cols3.shape
    C = CHUNK
    nb = _pick(b, (RWKV_SEQS_PER_STEP, 1))
    n_chunks = tp // C
    rowspec = lambda w: pl.BlockSpec((nb, 1, w), lambda bi, ci: (bi, 0, 0))
    full = lambda a: pl.BlockSpec(a.shape, lambda bi, ci: (0,) * a.ndim)
    sspec = pl.BlockSpec((nb, N_PAIRS, PAIR, PAIR), lambda bi, ci: (bi, 0, 0, 0))
    operand_bufs = [pltpu.VMEM((N_OPS, nb * C, D_MODEL), BF16), pltpu.VMEM((2, nb * C, D_MODEL), F32),
                    pltpu.VMEM((nb, 1, D_MODEL), F32)]
    return pl.pallas_call(
        functools.partial(_rwkv_kernel, t_valid=t_valid if t_valid < tp else None, nb=nb),
        grid=(b // nb, n_chunks // 2),
        in_specs=[pl.BlockSpec((nb, 2 * C, RWKV_COLS), lambda bi, ci: (bi, ci, 0)),
                  pl.BlockSpec((nb, C, RWKV_COLS), lambda bi, ci: (bi, jnp.minimum(2 * ci + 2, n_chunks - 1), 0)),
                  rowspec(D_MODEL), rowspec(D_MODEL), rowspec(D_MODEL), rowspec(LORA_COLS),
                  sspec, full(vec), full(mulo), full(lw), full(e256)],
        out_specs=[pl.BlockSpec((nb, 2 * C, D_MODEL), lambda bi, ci: (bi, ci, 0)), sspec],
        out_shape=[jax.ShapeDtypeStruct((b, tp, D_MODEL), BF16),
                   jax.ShapeDtypeStruct(s0.shape, F32)],
        scratch_shapes=[pltpu.VMEM((nb, N_PAIRS, PAIR, PAIR), F32),
                        pltpu.VMEM((nb, 1, D_MODEL), F32), pltpu.VMEM((nb, 1, D_MODEL), F32),
                        pltpu.VMEM((nb, 1, D_MODEL), F32), pltpu.VMEM((nb, 1, LORA_COLS), F32)]
                       + operand_bufs + operand_bufs,
        compiler_params=_cparams(("arbitrary", "arbitrary")),
        name="rwkv7",
    )(cols3, cols3, *prev, s0, vec, mulo, lw, e256)


def _sgu_kernel(u_ref, v_ref, ln_ref, ws_ref, bs_ref, y_ref, *maybe_v_out, lc):
    u = _gelu(u_ref[...]).astype(F32)
    v = _gelu(v_ref[...]).astype(F32)
    mu = jnp.mean(v, axis=-1, keepdims=True)
    vc = v - mu
    var = jnp.mean(vc * vc, axis=-1, keepdims=True)
    v = vc * lax.rsqrt(var + LN_EPS) * ln_ref[0:1, :] + ln_ref[1:2, :]
    if maybe_v_out:
        maybe_v_out[0][...] = v
    vb = v.astype(BF16)
    n_chunks = v.shape[0] // lc
    ri = lax.broadcasted_iota(jnp.int32, (lc, lc), 0)
    ci = lax.broadcasted_iota(jnp.int32, (lc, lc), 1)
    bias = bs_ref[...]
    outs = []
    for gi in range(GMLP_GROUPS):
        gs = slice(gi * GMLP_CHUNK, (gi + 1) * GMLP_CHUNK)
        w = jnp.where(ci <= ri, ws_ref[gi], 0.0).astype(BF16)
        rhs = jnp.concatenate([vb[n * lc:(n + 1) * lc, gs] for n in range(n_chunks)], axis=1)
        sv = _dot(w, rhs)
        sv = jnp.concatenate([sv[:, n * GMLP_CHUNK:(n + 1) * GMLP_CHUNK] for n in range(n_chunks)], axis=0)
        outs.append(sv + bias[:, gs])
    sv = jnp.concatenate(outs, axis=1)
    y_ref[...] = (u * sv).astype(y_ref.dtype)


def _sgu(uv3, ln, ws, bs_full, lc, n_chunks, want_v):
    b, t, _ = uv3.shape
    tt = lc * n_chunks
    col = lambda j: pl.BlockSpec((None, tt, D_MODEL), lambda bi, ti: (bi, ti, j))
    full = lambda a: pl.BlockSpec(a.shape, lambda bi, ti: (0,) * a.ndim)
    ospec = pl.BlockSpec((None, tt, D_MODEL), lambda bi, ti: (bi, ti, 0))
    out_specs = [ospec]
    out_shape = [jax.ShapeDtypeStruct((b, t, D_MODEL), BF16)]
    if want_v:
        out_specs.append(ospec)
        out_shape.append(jax.ShapeDtypeStruct((b, t, D_MODEL), F32))
    return pl.pallas_call(
        functools.partial(_sgu_kernel, lc=lc),
        grid=(b, t // tt),
        in_specs=[col(0), col(1), full(ln), full(ws), full(bs_full)],
        out_specs=out_specs,
        out_shape=out_shape,
        compiler_params=_cparams(("arbitrary", "arbitrary")),
        name="sgu",
    )(uv3, uv3, ln, ws, bs_full)


def _xattn_tile(q_ref, k_ref, v_ref):
    outs = []
    for h in range(X_HEADS):
        hs = slice(h * X_HEAD_DIM, (h + 1) * X_HEAD_DIM)
        s = _dot_nt(q_ref[:, hs], k_ref[:, hs]) * (X_HEAD_DIM ** -0.5)
        s = s - jnp.max(s, axis=-1, keepdims=True)
        e = jnp.exp(s)
        p = e * (1.0 / jnp.sum(e, axis=-1, keepdims=True))
        outs.append(_dot(p.astype(BF16), v_ref[:, hs]))
    return jnp.concatenate(outs, axis=1).astype(BF16)


def _merge_kernel(x_ref, gate_ref, q_ref, ya_ref, yb_ref, mk_ref, mv_ref, wb_ref, wo_ref, o_ref):
    gate = _sigmoid(gate_ref[...].astype(F32))
    merged = gate[:, :D_MODEL] * _dot(ya_ref[...], wb_ref[0])
    merged = merged + gate[:, D_MODEL:2 * D_MODEL] * _dot(yb_ref[...], wb_ref[1])
    y_c = _xattn_tile(q_ref, mk_ref, mv_ref)
    merged = merged + gate[:, 2 * D_MODEL:] * _dot(y_c, wb_ref[2])
    o_ref[...] = x_ref[...] + _dot(merged.astype(BF16), wo_ref[...])


def _merge(x, gq, ya, yb, mem_k, mem_v, wb, wo, tm, tiles_per_seq):
    t, d = x.shape
    tok = lambda w: pl.BlockSpec((tm, w), lambda i: (i, 0))
    mspec = pl.BlockSpec((None, N_MEM, d), lambda i: (i // tiles_per_seq, 0, 0))
    return pl.pallas_call(
        _merge_kernel,
        grid=(t // tm,),
        in_specs=[tok(d), tok(3 * d), pl.BlockSpec((tm, d), lambda i: (i, GQ_Q_BLOCK)), tok(d), tok(d), mspec, mspec,
                  pl.BlockSpec(wb.shape, lambda i: (0, 0, 0)), pl.BlockSpec(wo.shape, lambda i: (0, 0))],
        out_specs=tok(d),
        out_shape=jax.ShapeDtypeStruct((t, d), F32),
        compiler_params=_cparams(("arbitrary",)),
        name="merge",
    )(x, gq, gq, ya, yb, mem_k, mem_v, wb, wo)


def _ffn_kernel(x_ref, g_ref, wu_ref, wd_ref, gf_ref, o_ref, h_scr, *, final_norm):
    f = pl.program_id(1)

    @pl.when(f == 0)
    def _init():
        x = x_ref[...]
        h_scr[...] = _rmsnorm(x, g_ref[...]).astype(BF16)
        o_ref[...] = x

    up = _dot(h_scr[...], wu_ref[...])
    act = jnp.square(jnp.maximum(up, 0.0)).astype(BF16)
    o_ref[...] += _dot(act, wd_ref[...])

    if final_norm:
        @pl.when(f == pl.num_programs(1) - 1)
        def _final():
            o_ref[...] = _rmsnorm(o_ref[...], gf_ref[...])


def _ffn(x, g, wu, wd, g_final, final_norm, tm, tf):
    t, d = x.shape
    return pl.pallas_call(
        functools.partial(_ffn_kernel, final_norm=final_norm),
        grid=(t // tm, D_FF // tf),
        in_specs=[pl.BlockSpec((tm, d), lambda i, f: (i, 0)),
                  pl.BlockSpec((1, d), lambda i, f: (0, 0)),
                  pl.BlockSpec((d, tf), lambda i, f: (0, f)),
                  pl.BlockSpec((tf, d), lambda i, f: (f, 0)),
                  pl.BlockSpec((1, d), lambda i, f: (0, 0))],
        out_specs=pl.BlockSpec((tm, d), lambda i, f: (i, 0)),
        out_shape=jax.ShapeDtypeStruct((t, d), F32),
        scratch_shapes=[pltpu.VMEM((tm, d), BF16)],
        compiler_params=_cparams(("arbitrary", "arbitrary")),
        name="ffn",
    )(x, g, wu, wd, g_final)


def _state_to_pairs(s):
    b = s.shape[0]
    s = s.reshape(b, N_PAIRS, 2, HEAD_DIM, HEAD_DIM)
    z = jnp.zeros_like(s[:, :, 0])
    top = jnp.concatenate([s[:, :, 0], z], axis=-1)
    bot = jnp.concatenate([z, s[:, :, 1]], axis=-1)
    return jnp.concatenate([top, bot], axis=-2)


def _pairs_to_state(s2):
    b = s2.shape[0]
    s0 = s2[:, :, :HEAD_DIM, :HEAD_DIM]
    s1 = s2[:, :, HEAD_DIM:, HEAD_DIM:]
    return jnp.stack([s0, s1], axis=2).reshape(b, N_HEADS, HEAD_DIM, HEAD_DIM)


def _pick(t, candidates):
    for c in candidates:
        if t % c == 0:
            return c
    return t


def _group_layer(x, b, t, mem_k, mem_v, prev_row, s0_pairs, lp):
    n_tok = b * t
    tm = _pick(n_tok, (512, 256))
    g_mix = lp["norm_mix_g"]
    cols3 = _proj(x, g_mix, lp["w_in_rwkv"], BF16, tm, RWKV_COLS).reshape(b, t, RWKV_COLS)
    uv3 = _proj(x, g_mix, lp["w_in_gmlp"], BF16, tm, 2 * D_MODEL).reshape(b, t, 2 * D_MODEL)
    gq = _proj(x, g_mix, lp["w_in_gq"], BF16, tm, 4 * D_MODEL)
    new_row = cols3[:, t - 1:t, :].astype(F32)

    tp = -(-t // (2 * CHUNK)) * 2 * CHUNK
    cols3p = cols3 if tp == t else jnp.pad(cols3, ((0, 0), (0, tp - t), (0, 0)))
    prev = (prev_row[..., :D_MODEL], prev_row[..., D_MODEL:2 * D_MODEL], prev_row[..., 2 * D_MODEL:3 * D_MODEL],
            prev_row[..., 3 * D_MODEL:])
    y_a, s_new = _rwkv(cols3p, prev, s0_pairs, lp["rwkv_vec"], lp["rwkv_mu_lo"], lp["rwkv_lw"], lp["e256"], t)
    y_a = y_a[:, :t].reshape(n_tok, D_MODEL)

    lc = min(t, GMLP_CHUNK)
    n_chunks = _pick(t // lc, (8, 4, 2, 1))
    ws = lp["sgu_w_s"][:, :lc, :lc]
    bs_full = jnp.tile(jnp.repeat(lp["sgu_b_s"][:, :lc].T, GMLP_CHUNK, axis=1), (n_chunks, 1))
    sgu_out = _sgu(uv3, lp["sgu_ln"], ws, bs_full, lc, n_chunks, want_v=t < GMLP_CHUNK)
    y_b = sgu_out[0].reshape(n_tok, D_MODEL)
    v_rows = sgu_out[1] if t < GMLP_CHUNK else None

    tm_merge = _pick(t, (512, 256, 128))
    x = _merge(x, gq, y_a, y_b, mem_k, mem_v, lp["w_branch"], lp["w_out"], tm_merge, t // tm_merge)
    x = _ffn(x, lp["norm_ffn_g"], lp["w_ffn_up"], lp["w_ffn_down"], lp["norm_final_g"], lp["is_last"],
             _pick(n_tok, (1024, 512, 256)), 2048)
    return x, new_row, s_new, v_rows


def kernel(x_prompt, x_sample, cache_mem_k, cache_mem_v, state_wkv, state_shift, mem_prompt, norm_mix_g, norm_mem_g, norm_ffn_g, norm_final_g, w_in, w_mem_kv, rwkv_mu, rwkv_w0, rwkv_w2, rwkv_a0, rwkv_a2, rwkv_g2, rwkv_k_k, rwkv_k_a, rwkv_r_k, rwkv_lnx_g, rwkv_lnx_b, sgu_ln_g, sgu_ln_b, sgu_w_s, sgu_b_s, w_branch, w_out, w_ffn_up, w_ffn_down):
    depth = w_in.shape[0]
    bp, tp, _ = x_prompt.shape
    bs, ts, _ = x_sample.shape

    w_in_rwkv = w_in[..., :RWKV_COLS].astype(BF16)
    w_in_gmlp = w_in[..., RWKV_COLS:COL_Q_START].astype(BF16)
    w_in_gq = jnp.concatenate([w_in[..., COL_Q_START + D_MODEL:], w_in[..., COL_Q_START:COL_Q_START + D_MODEL]],
                              axis=-1).astype(BF16)
    w_kv_b = w_mem_kv.astype(BF16)
    w_branch_b = w_branch.astype(BF16)
    w_out_b = w_out.astype(BF16)
    w_up_b = w_ffn_up.astype(BF16)
    w_down_b = w_ffn_down.astype(BF16)
    zeros = lambda n: jnp.zeros((depth, n, D_MODEL), F32)
    lw = jnp.stack([jnp.concatenate([rwkv_w2, zeros(ICLR_LORA + GATE_LORA)], axis=1),
                    jnp.concatenate([zeros(DECAY_LORA), rwkv_a2, zeros(GATE_LORA)], axis=1),
                    jnp.concatenate([zeros(DECAY_LORA + ICLR_LORA), rwkv_g2], axis=1)], axis=1).astype(BF16)
    rwkv_vec = jnp.stack([rwkv_mu[:, :D_MODEL], rwkv_mu[:, D_MODEL:2 * D_MODEL], rwkv_mu[:, 2 * D_MODEL:3 * D_MODEL],
                          rwkv_w0, rwkv_a0, rwkv_k_k, rwkv_k_a, rwkv_r_k.reshape(depth, D_MODEL),
                          rwkv_lnx_g, rwkv_lnx_b] + [jnp.zeros((depth, D_MODEL), F32)] * 6, axis=1)
    sgu_ln = jnp.stack([sgu_ln_g, sgu_ln_b] + [jnp.zeros((depth, D_MODEL), F32)] * 6, axis=1)
    hq = 4 * HEAD_DIM
    e256 = (jnp.arange(hq)[:, None] // HEAD_DIM == jnp.arange(hq)[None, :] // HEAD_DIM).astype(BF16)

    xp = x_prompt.reshape(bp * tp, D_MODEL)
    xs = x_sample.reshape(bs * ts, D_MODEL)
    mem_flat = mem_prompt.reshape(bp * N_MEM, D_MODEL)
    prompt_row0 = jnp.zeros((bp, 1, RWKV_COLS), F32)
    prompt_s0 = jnp.zeros((bp, N_PAIRS, PAIR, PAIR), F32)

    mk_p, mv_p, wkv_p, row_p, wkv_s, row_s, v_s = [], [], [], [], [], [], []
    for l in range(depth):
        lp = {
            "norm_mix_g": norm_mix_g[l][None],
            "w_in_rwkv": w_in_rwkv[l], "w_in_gmlp": w_in_gmlp[l], "w_in_gq": w_in_gq[l],
            "rwkv_vec": rwkv_vec[l], "rwkv_mu_lo": rwkv_mu[l][None, 3 * D_MODEL:], "rwkv_lw": lw[l], "e256": e256,
            "sgu_ln": sgu_ln[l], "sgu_w_s": sgu_w_s[l], "sgu_b_s": sgu_b_s[l],
            "w_branch": w_branch_b[l], "w_out": w_out_b[l],
            "norm_ffn_g": norm_ffn_g[l][None], "w_ffn_up": w_up_b[l], "w_ffn_down": w_down_b[l],
            "norm_final_g": norm_final_g[None], "is_last": l == depth - 1,
        }
        kv = _proj(mem_flat, norm_mem_g[l][None], w_kv_b[l], F32, _pick(bp * N_MEM, (512, 256)), 2 * D_MODEL)
        mem_k = kv[:, :D_MODEL].reshape(bp, N_MEM, D_MODEL)
        mem_v = kv[:, D_MODEL:].reshape(bp, N_MEM, D_MODEL)
        xp, r_p, s_p, _ = _group_layer(xp, bp, tp, mem_k.astype(BF16), mem_v.astype(BF16), prompt_row0, prompt_s0, lp)
        mk_p.append(mem_k.reshape(bp, N_MEM, X_HEADS, X_HEAD_DIM))
        mv_p.append(mem_v.reshape(bp, N_MEM, X_HEADS, X_HEAD_DIM))
        wkv_p.append(_pairs_to_state(s_p))
        row_p.append(r_p)

        ck = cache_mem_k[l].reshape(bs, N_MEM, D_MODEL).astype(BF16)
        cv = cache_mem_v[l].reshape(bs, N_MEM, D_MODEL).astype(BF16)
        xs, r_s, s_s, vr = _group_layer(xs, bs, ts, ck, cv, state_shift[l],
                                        _state_to_pairs(state_wkv[l]), lp)
        wkv_s.append(_pairs_to_state(s_s))
        row_s.append(r_s)
        v_s.append(vr)

    y_prompt = xp.reshape(bp, tp, D_MODEL)
    y_sample = xs.reshape(bs, ts, D_MODEL)
    return (y_prompt, y_sample, jnp.stack(mk_p), jnp.stack(mv_p), jnp.stack(wkv_p), jnp.stack(row_p),
            jnp.stack(wkv_s), jnp.stack(row_s), jnp.stack(v_s))
```
